```python
import jax, jax.numpy as jnp
from jax import lax
import numpy as np

D_MODEL = 1024
BATCH = 16
SEQ = 4096
DEPTH = 2

EPS = 1e-6
MLA_HEADS = 8
QK_NOPE = 64
QK_ROPE = 32
QK_HEAD = QK_NOPE + QK_ROPE
V_HEAD = 64
Q_RANK = 384
KV_RANK = 256
ROPE_BASE = 10000.0
Q_BLOCK = 128
MLA_WIDTH = MLA_HEADS * V_HEAD
SG_GROUPS = 4
SG_GROUP_DIM = 128
SG_WIDTH = SG_GROUPS * SG_GROUP_DIM
CHUNK = 128
CONV_WIDTH = 512
CONV_K = 3
N_GROUPS = 4
EXPERTS_PER_GROUP = 8
N_EXPERTS = N_GROUPS * EXPERTS_PER_GROUP
TOP_K = 2
D_EXPERT = 256
IN_SIZES = (Q_RANK, KV_RANK, QK_ROPE, SG_WIDTH, SG_WIDTH,
            CONV_WIDTH, CONV_WIDTH, CONV_WIDTH, D_MODEL, D_MODEL, D_MODEL)
D_IN = Q_RANK + KV_RANK + QK_ROPE + 2 * SG_WIDTH + 3 * CONV_WIDTH + 3 * D_MODEL

kernel_name = "hybrid_mla_sgmlp_shortconv_hmoe"


def rmsnorm(x, g):
    xf = x.astype(jnp.float32)
    y = xf * lax.rsqrt(jnp.mean(xf * xf, axis=-1, keepdims=True) + EPS)
    return (y * g.astype(jnp.float32)).astype(x.dtype)


def split_cols(a, sizes):
    out, start = [], 0
    for s in sizes:
        out.append(a[..., start:start + s])
        start += s
    return out


def rope_tables(positions):
    inv = ROPE_BASE ** (-jnp.arange(0, QK_ROPE, 2, dtype=jnp.float32) / QK_ROPE)
    ang = positions.astype(jnp.float32)[..., None] * inv
    return jnp.cos(ang), jnp.sin(ang)


def apply_rope(x, cos, sin):
    xf = x.astype(jnp.float32)
    x1, x2 = xf[..., :QK_ROPE // 2], xf[..., QK_ROPE // 2:]
    return jnp.concatenate([x1 * cos - x2 * sin, x2 * cos + x1 * sin], axis=-1).astype(x.dtype)


def mla_branch(c_q, c_kv, k_rope_raw, cos, sin, q_norm, w_q_up, kv_norm, w_kv_up):
    B, S, _ = c_q.shape
    q = (rmsnorm(c_q, q_norm) @ w_q_up).reshape(B, S, MLA_HEADS, QK_HEAD)
    q_nope, q_rope = q[..., :QK_NOPE], q[..., QK_NOPE:]
    q_rope = apply_rope(q_rope, cos[:, :, None, :], sin[:, :, None, :])
    q = jnp.concatenate([q_nope, q_rope], axis=-1)
    kv = (rmsnorm(c_kv, kv_norm) @ w_kv_up).reshape(B, S, MLA_HEADS, QK_NOPE + V_HEAD)
    k_nope, v = kv[..., :QK_NOPE], kv[..., QK_NOPE:]
    k_rope = apply_rope(k_rope_raw[:, :, None, :], cos[:, :, None, :], sin[:, :, None, :])
    k = jnp.concatenate([k_nope, jnp.broadcast_to(k_rope, (B, S, MLA_HEADS, QK_ROPE))], axis=-1)
    scale = QK_HEAD ** -0.5
    nb = S // Q_BLOCK
    q_blocks = q.reshape(B, nb, Q_BLOCK, MLA_HEADS, QK_HEAD).transpose(1, 0, 2, 3, 4)
    k_pos = jnp.arange(S)
    neg = jnp.finfo(jnp.float32).min

    def attend(args):
        q_blk, blk = args
        q_pos = blk * Q_BLOCK + jnp.arange(Q_BLOCK)
        s = jnp.einsum('bqhd,bkhd->bhqk', q_blk, k).astype(jnp.float32) * scale
        s = jnp.where(k_pos[None, :] <= q_pos[:, None], s, neg)
        p = jax.nn.softmax(s, axis=-1).astype(v.dtype)
        return jnp.einsum('bhqk,bkhd->bqhd', p, v)

    o = lax.map(attend, (q_blocks, jnp.arange(nb)))
    return o.transpose(1, 0, 2, 3, 4).reshape(B, S, MLA_WIDTH)


def spatial_gating_branch(u, v, sg_norm, w_s, b_s):
    B, S, _ = u.shape
    u = jax.nn.gelu(u)
    v = rmsnorm(jax.nn.gelu(v), sg_norm)
    v = v.reshape(B, S // CHUNK, CHUNK, SG_GROUPS, SG_GROUP_DIM)
    w = w_s * jnp.tril(jnp.ones((CHUNK, CHUNK), dtype=w_s.dtype))[None]
    mixed = jnp.einsum('gts,bcsgd->bctgd', w, v) + b_s.T[:, :, None]
    return u * mixed.reshape(B, S, SG_WIDTH)


def short_conv_branch(b_gate, c_gate, h, conv_w):
    S = h.shape[1]
    z = c_gate * h
    zp = jnp.pad(z, ((0, 0), (CONV_K - 1, 0), (0, 0)))
    y = conv_w[0] * zp[:, 0:S]
    for j in range(1, CONV_K):
        y = y + conv_w[j] * zp[:, j:j + S]
    return b_gate * y


def hier_moe(h, w_group_router, b_group_router, w_expert_router, b_expert_router, w_gate_up, w_down):
    B, S, D = h.shape
    t = h.reshape(B * S, D)
    T = t.shape[0]
    g_prob = jax.nn.softmax((t @ w_group_router).astype(jnp.float32) + b_group_router.astype(jnp.float32), axis=-1)
    g_p, g_idx = lax.top_k(g_prob, 1)
    e_logits = ((t @ w_expert_router).astype(jnp.float32) + b_expert_router.astype(jnp.float32))
    e_logits = e_logits.reshape(T, N_GROUPS, EXPERTS_PER_GROUP)
    e_sel = jnp.take_along_axis(e_logits, g_idx[:, :, None], axis=1)[:, 0]
    e_prob = jax.nn.softmax(e_sel, axis=-1)
    top_p, top_i = lax.top_k(e_prob, TOP_K)
    weights = g_p * (top_p / jnp.sum(top_p, axis=-1, keepdims=True))
    expert_id = g_idx * EXPERTS_PER_GROUP + top_i
    combine = jnp.einsum('tk,tke->te', weights, jax.nn.one_hot(expert_id, N_EXPERTS, dtype=jnp.float32))
    combine = combine.astype(t.dtype)
    out = jnp.zeros_like(t)
    for e in range(N_EXPERTS):
        gu = t @ w_gate_up[e]
        y = (jax.nn.silu(gu[:, :D_EXPERT]) * gu[:, D_EXPERT:]) @ w_down[e]
        out = out + combine[:, e:e + 1] * y
    return out.reshape(B, S, D)


def setup_inputs(seed: int = 0) -> dict:
    key = jax.random.key(seed)
    ks = jax.random.split(key, 32)
    L, D, f32 = DEPTH, D_MODEL, jnp.float32

    def nrm(k, shape, fan_in):
        return jax.random.normal(k, shape, f32) * (fan_in ** -0.5)

    def gain(k, shape):
        return 1.0 + 0.05 * jax.random.normal(k, shape, f32)

    offsets = jax.random.randint(ks[1], (BATCH,), 0, 1024, dtype=jnp.int32)
    positions = offsets[:, None] + jnp.arange(SEQ, dtype=jnp.int32)[None, :]
    return {
        "x": jax.random.normal(ks[0], (BATCH, SEQ, D), f32),
        "positions": positions,
        "norm_mix": gain(ks[2], (L, D)),
        "w_in": nrm(ks[3], (L, D, D_IN), D),
        "q_norm": gain(ks[4], (L, Q_RANK)),
        "w_q_up": nrm(ks[5], (L, Q_RANK, MLA_HEADS * QK_HEAD), Q_RANK),
        "kv_norm": gain(ks[6], (L, KV_RANK)),
        "w_kv_up": nrm(ks[7], (L, KV_RANK, MLA_HEADS * (QK_NOPE + V_HEAD)), KV_RANK),
        "sg_norm": gain(ks[8], (L, SG_WIDTH)),
        "w_s": nrm(ks[9], (L, SG_GROUPS, CHUNK, CHUNK), CHUNK),
        "b_s": 1.0 + 0.1 * jax.random.normal(ks[10], (L, SG_GROUPS, CHUNK), f32),
        "conv_w": nrm(ks[11], (L, CONV_K, CONV_WIDTH), CONV_K),
        "w_a_out": nrm(ks[12], (L, MLA_WIDTH, D), MLA_WIDTH),
        "w_b_out": nrm(ks[13], (L, SG_WIDTH, D), SG_WIDTH),
        "w_c_out": nrm(ks[14], (L, CONV_WIDTH, D), CONV_WIDTH),
        "w_o": nrm(ks[15], (L, D, D), D),
        "norm_ffn": gain(ks[16], (L, D)),
        "w_group_router": nrm(ks[17], (L, D, N_GROUPS), D),
        "b_group_router": 0.01 * jax.random.normal(ks[18], (L, N_GROUPS), f32),
        "w_expert_router": nrm(ks[19], (L, D, N_EXPERTS), D),
        "b_expert_router": 0.01 * jax.random.normal(ks[20], (L, N_EXPERTS), f32),
        "w_gate_up": nrm(ks[21], (L, N_EXPERTS, D, 2 * D_EXPERT), D),
        "w_down": nrm(ks[22], (L, N_EXPERTS, D_EXPERT, D), D_EXPERT),
        "final_norm": gain(ks[23], (D,)),
    }


def reference(x, positions, norm_mix, w_in, q_norm, w_q_up, kv_norm, w_kv_up, sg_norm, w_s, b_s,
              conv_w, w_a_out, w_b_out, w_c_out, w_o, norm_ffn, w_group_router, b_group_router,
              w_expert_router, b_expert_router, w_gate_up, w_down, final_norm):
    cos, sin = rope_tables(positions)
    for l in range(DEPTH):
        h = rmsnorm(x, norm_mix[l])
        (c_q, c_kv, k_r, sg_u, sg_v, cb, cc, ch, ga, gb, gc) = split_cols(h @ w_in[l], IN_SIZES)
        y_a = mla_branch(c_q, c_kv, k_r, cos, sin, q_norm[l], w_q_up[l], kv_norm[l], w_kv_up[l]) @ w_a_out[l]
        y_b = spatial_gating_branch(sg_u, sg_v, sg_norm[l], w_s[l], b_s[l]) @ w_b_out[l]
        y_c = short_conv_branch(cb, cc, ch, conv_w[l]) @ w_c_out[l]
        merged = jax.nn.sigmoid(ga) * y_a + jax.nn.sigmoid(gb) * y_b + jax.nn.sigmoid(gc) * y_c
        x = x + merged @ w_o[l]
        h = rmsnorm(x, norm_ffn[l])
        x = x + hier_moe(h, w_group_router[l], b_group_router[l], w_expert_router[l],
                         b_expert_router[l], w_gate_up[l], w_down[l])
    return rmsnorm(x, final_norm)
```

```python
import functools

import jax
import jax.numpy as jnp
from jax import lax
from jax.experimental import pallas as pl
from jax.experimental.pallas import tpu as pltpu

F32 = jnp.float32
BF16 = jnp.bfloat16

EPS = 1e-6
MLA_HEADS = 8
QK_NOPE = 64
QK_ROPE = 32
QK_HEAD = QK_NOPE + QK_ROPE
V_HEAD = 64
Q_RANK = 384
KV_RANK = 256
ROPE_BASE = 10000.0
SG_GROUPS = 4
SG_GROUP_DIM = 128
SG_WIDTH = SG_GROUPS * SG_GROUP_DIM
CHUNK = 128
CONV_WIDTH = 512
CONV_K = 3
N_GROUPS = 4
EXPERTS_PER_GROUP = 8
N_EXPERTS = N_GROUPS * EXPERTS_PER_GROUP
D_EXPERT = 256

LANES = 128
HEAD_PAD = 128
MASK_VALUE = -1e30
VMEM_LIMIT = 56 * 1024 * 1024

_P1 = Q_RANK + KV_RANK + HEAD_PAD
_P2 = _P1 + 2 * SG_WIDTH
_P3 = _P2 + 3 * CONV_WIDTH
GROUP_LANE0 = N_EXPERTS


def _dot(a, b):
    return jnp.dot(a, b, preferred_element_type=F32)


def _rms(xf, g):
    return xf * lax.rsqrt(jnp.mean(xf * xf, axis=-1, keepdims=True) + EPS) * g


def _gelu(x):
    return 0.5 * x * (1.0 + jnp.tanh(0.7978845608028654 * (x + 0.044715 * (x * x * x))))


def _rope(t, rc, rm, rp):
    return t * rc + pltpu.roll(t, HEAD_PAD - QK_ROPE // 2, 1) * rm + pltpu.roll(t, QK_ROPE // 2, 1) * rp


def _const_spec(shape):
    nd = len(shape)
    return pl.BlockSpec(shape, lambda *_: (0,) * nd, pipeline_mode=pl.Buffered(1))


def _mixer_pre_kernel(x_ref, nm_ref, win_ref, qn_ref, wq_ref, kvn_ref, wk_ref, wv_ref, sgn_ref,
                      ws_ref, bs_ref, cw_ref, wb_ref, wc_ref, rc_ref, rm_ref, rp_ref,
                      q_ref, k_ref, v_ref, part_ref, ga_ref,
                      halo_ref, ybin_ref, *, tiles_per_seq, tm):
    i = pl.program_id(0)
    d_model = x_ref.shape[1]
    h = _rms(x_ref[...], nm_ref[...]).astype(BF16)

    p1 = _dot(h, win_ref[:, 0:_P1])
    rc, rm, rp = rc_ref[...], rm_ref[...], rp_ref[...]
    qn = _rms(p1[:, :Q_RANK], qn_ref[...]).astype(BF16)
    qf = _dot(qn, wq_ref[...])
    scale = QK_HEAD ** -0.5
    for hd in range(MLA_HEADS):
        sl = slice(hd * HEAD_PAD, (hd + 1) * HEAD_PAD)
        q_ref[:, sl] = (_rope(qf[:, sl], rc, rm, rp) * scale).astype(BF16)
    kvn = _rms(p1[:, Q_RANK:Q_RANK + KV_RANK], kvn_ref[...]).astype(BF16)
    kr = _rope(p1[:, Q_RANK + KV_RANK:_P1], rc, rm, rp)
    kf = _dot(kvn, wk_ref[...])
    for hd in range(MLA_HEADS):
        sl = slice(hd * HEAD_PAD, (hd + 1) * HEAD_PAD)
        k_ref[:, sl] = (kf[:, sl] + kr).astype(BF16)
    v_ref[...] = _dot(kvn, wv_ref[...]).astype(BF16)

    p2 = _dot(h, win_ref[:, _P1:_P2])
    u = _gelu(p2[:, :SG_WIDTH])
    vb = _rms(_gelu(p2[:, SG_WIDTH:]), sgn_ref[...]).astype(BF16)
    r_i = lax.broadcasted_iota(jnp.int32, (CHUNK, CHUNK), 0)
    c_i = lax.broadcasted_iota(jnp.int32, (CHUNK, CHUNK), 1)
    for g in range(SG_GROUPS):
        wg = jnp.where(c_i <= r_i, ws_ref[g], 0.0).astype(BF16)
        gs = slice(g * SG_GROUP_DIM, (g + 1) * SG_GROUP_DIM)
        for c in range(tm // CHUNK):
            cs = slice(c * CHUNK, (c + 1) * CHUNK)
            mixed = _dot(wg, vb[cs, gs]) + bs_ref[g]
            ybin_ref[cs, gs] = (u[cs, gs] * mixed).astype(BF16)

    @pl.when(i % tiles_per_seq == 0)
    def _():
        halo_ref[...] = jnp.zeros_like(halo_ref)

    p3 = _dot(h, win_ref[:, _P2:_P3])
    z = p3[:, CONV_WIDTH:2 * CONV_WIDTH] * p3[:, 2 * CONV_WIDTH:]
    row = lax.broadcasted_iota(jnp.int32, z.shape, 0)
    halo = halo_ref[...]
    last1 = halo[7:8, :]
    last2 = halo[6:7, :]
    z1 = jnp.where(row == 0, last1, pltpu.roll(z, 1, 0))
    z2 = jnp.where(row == 0, last2, jnp.where(row == 1, last1, pltpu.roll(z, 2, 0)))
    y = cw_ref[0:1, :] * z2 + cw_ref[1:2, :] * z1 + cw_ref[2:3, :] * z
    yc_in = (p3[:, :CONV_WIDTH] * y).astype(BF16)
    halo_ref[...] = z[tm - 8:tm, :]

    p4 = _dot(h, win_ref[:, _P3:_P3 + 3 * d_model])
    yb = _dot(ybin_ref[...], wb_ref[...])
    yc = _dot(yc_in, wc_ref[...])
    part = (jax.nn.sigmoid(p4[:, d_model:2 * d_model]) * yb
            + jax.nn.sigmoid(p4[:, 2 * d_model:]) * yc)
    part_ref[...] = part.astype(BF16)
    ga_ref[...] = jax.nn.sigmoid(p4[:, :d_model]).astype(BF16)


def _mixer_pre(x2, lw, rc, rm, rp, *, seq, tm):
    t, d = x2.shape
    row = lambda w: pl.BlockSpec((tm, w), lambda i: (i, 0))
    consts = [lw["norm_mix"], lw["w_in"], lw["q_norm"], lw["w_q"], lw["kv_norm"], lw["w_k"], lw["w_v"],
              lw["sg_norm"], lw["w_s"], lw["b_s"], lw["conv_w"], lw["w_b_out"], lw["w_c_out"]]
    kern = functools.partial(_mixer_pre_kernel, tiles_per_seq=seq // tm, tm=tm)
    return pl.pallas_call(
        kern,
        grid=(t // tm,),
        in_specs=[row(d)] + [_const_spec(c.shape) for c in consts] + [row(HEAD_PAD)] * 3,
        out_specs=[row(MLA_HEADS * HEAD_PAD), row(MLA_HEADS * HEAD_PAD), row(MLA_HEADS * V_HEAD), row(d), row(d)],
        out_shape=[jax.ShapeDtypeStruct((t, MLA_HEADS * HEAD_PAD), BF16),
                   jax.ShapeDtypeStruct((t, MLA_HEADS * HEAD_PAD), BF16),
                   jax.ShapeDtypeStruct((t, MLA_HEADS * V_HEAD), BF16),
                   jax.ShapeDtypeStruct((t, d), BF16),
                   jax.ShapeDtypeStruct((t, d), BF16)],
        scratch_shapes=[pltpu.VMEM((8, CONV_WIDTH), F32), pltpu.VMEM((tm, SG_WIDTH), BF16)],
        compiler_params=pltpu.CompilerParams(dimension_semantics=("arbitrary",), vmem_limit_bytes=VMEM_LIMIT),
        name="mixer_pre",
    )(x2, *consts, rc, rm, rp)


def _attn_kernel(q_ref, k_ref, v_ref, o_ref, m_ref, l_ref, acc_ref, *, tq):
    qi = pl.program_id(2)
    m_ref[...] = jnp.full_like(m_ref, MASK_VALUE)
    l_ref[...] = jnp.zeros_like(l_ref)
    acc_ref[...] = jnp.zeros_like(acc_ref)

    def step(j, masked):
        start = pl.multiple_of(j * tq, tq)
        v_j = v_ref[pl.ds(start, tq), :]
        for sub in range(2):
            sl = slice(sub * HEAD_PAD, (sub + 1) * HEAD_PAD)
            k_j = k_ref[pl.ds(start, tq), sl]
            s = lax.dot_general(q_ref[:, sl], k_j, (((1,), (1,)), ((), ())), preferred_element_type=F32)
            if masked:
                r_i = lax.broadcasted_iota(jnp.int32, s.shape, 0)
                c_i = lax.broadcasted_iota(jnp.int32, s.shape, 1)
                s = jnp.where(c_i <= r_i, s, MASK_VALUE)
            m_prev = m_ref[sub]
            m_next = jnp.maximum(m_prev, jnp.max(s, axis=1, keepdims=True))
            alpha = jnp.exp(m_prev - m_next)
            p = jnp.exp(s - pltpu.repeat(m_next, tq // LANES, 1))
            l_ref[sub] = alpha * l_ref[sub] + jnp.sum(p, axis=1, keepdims=True)
            m_ref[sub] = m_next
            acc_ref[sub] = alpha * acc_ref[sub] + _dot(p.astype(BF16), v_j)

    def body(j, carry):
        step(j, False)
        return carry

    lax.fori_loop(0, qi, body, 0)
    step(qi, True)
    lane = lax.broadcasted_iota(jnp.int32, (tq, LANES), 1)
    o0 = acc_ref[0] / l_ref[0]
    o1 = acc_ref[1] / l_ref[1]
    o_ref[...] = jnp.where(lane < V_HEAD, o0, o1).astype(BF16)


def _attention(q, k, v, *, batch, seq, tq):
    t = q.shape[0]
    nq = seq // tq
    pairs = MLA_HEADS // 2
    return pl.pallas_call(
        functools.partial(_attn_kernel, tq=tq),
        grid=(batch, pairs, nq),
        in_specs=[pl.BlockSpec((tq, 2 * HEAD_PAD), lambda b, hp, qi: (b * nq + qi, hp)),
                  pl.BlockSpec((seq, 2 * HEAD_PAD), lambda b, hp, qi: (b, hp)),
                  pl.BlockSpec((seq, 2 * V_HEAD), lambda b, hp, qi: (b, hp))],
        out_specs=pl.BlockSpec((tq, 2 * V_HEAD), lambda b, hp, qi: (b * nq + qi, hp)),
        out_shape=jax.ShapeDtypeStruct((t, MLA_HEADS * V_HEAD), BF16),
        scratch_shapes=[pltpu.VMEM((2, tq, LANES), F32), pltpu.VMEM((2, tq, LANES), F32),
                        pltpu.VMEM((2, tq, LANES), F32)],
        compiler_params=pltpu.CompilerParams(dimension_semantics=("arbitrary",) * 3, vmem_limit_bytes=VMEM_LIMIT),
        name="attention",
    )(q, k, v)


def _route(logits):
    lane = lax.broadcasted_iota(jnp.int32, logits.shape, 1)
    lane_f = lane.astype(F32)
    big = float(LANES)
    is_grp = (lane >= GROUP_LANE0) & (lane < GROUP_LANE0 + N_GROUPS)
    g_log = jnp.where(is_grp, logits, MASK_VALUE)
    g_max = jnp.max(g_log, axis=1, keepdims=True)
    g_sum = jnp.sum(jnp.exp(g_log - g_max), axis=1, keepdims=True)
    g_p = 1.0 / g_sum
    g_idx = jnp.min(jnp.where(g_log == g_max, lane_f, big), axis=1, keepdims=True) - float(GROUP_LANE0)
    lo = g_idx * float(EXPERTS_PER_GROUP)
    in_grp = (lane_f >= lo) & (lane_f < lo + float(EXPERTS_PER_GROUP))
    e_log = jnp.where(in_grp, logits, MASK_VALUE)
    e_max = jnp.max(e_log, axis=1, keepdims=True)
    e_exp = jnp.exp(e_log - e_max)
    e_prob = e_exp / jnp.sum(e_exp, axis=1, keepdims=True)
    cand = jnp.where(in_grp, e_prob, -1.0)
    p1 = jnp.max(cand, axis=1, keepdims=True)
    i1 = jnp.min(jnp.where(cand == p1, lane_f, big), axis=1, keepdims=True)
    cand2 = jnp.where(lane_f == i1, -1.0, cand)
    p2 = jnp.max(cand2, axis=1, keepdims=True)
    i2 = jnp.min(jnp.where(cand2 == p2, lane_f, big), axis=1, keepdims=True)
    denom = p1 + p2
    w1 = g_p * (p1 / denom)
    w2 = g_p * (p2 / denom)
    return jnp.where(lane_f == i1, w1, 0.0) + jnp.where(lane_f == i2, w2, 0.0)


def _mixer_post_kernel(o_ref, part_ref, ga_ref, x_ref, wa_ref, wo_ref, nf_ref, wr_ref, br_ref,
                       x1_ref, h2_ref, comb_ref):
    ya = _dot(o_ref[...], wa_ref[...])
    merged = ga_ref[...].astype(F32) * ya + part_ref[...].astype(F32)
    x1 = x_ref[...] + _dot(merged.astype(BF16), wo_ref[...])
    x1_ref[...] = x1
    h2 = _rms(x1, nf_ref[...]).astype(BF16)
    h2_ref[...] = h2
    comb_ref[...] = _route(_dot(h2, wr_ref[...]) + br_ref[...])


def _mixer_post(o, part, ga, x2, lw, *, tm):
    t, d = x2.shape
    row = lambda w: pl.BlockSpec((tm, w), lambda i: (i, 0))
    consts = [lw["w_a_out"], lw["w_o"], lw["norm_ffn"], lw["w_router"], lw["b_router"]]
    return pl.pallas_call(
        _mixer_post_kernel,
        grid=(t // tm,),
        in_specs=[row(o.shape[1]), row(d), row(d), row(d)] + [_const_spec(c.shape) for c in consts],
        out_specs=[row(d), row(d), row(LANES)],
        out_shape=[jax.ShapeDtypeStruct((t, d), F32), jax.ShapeDtypeStruct((t, d), BF16),
                   jax.ShapeDtypeStruct((t, LANES), F32)],
        compiler_params=pltpu.CompilerParams(dimension_semantics=("arbitrary",), vmem_limit_bytes=VMEM_LIMIT),
        name="mixer_post",
    )(o, part, ga, x2, *consts)


def _moe_kernel(h2_ref, x1_ref, comb_ref, wgu_ref, wdn_ref, fn_ref, out_ref, *, final_norm):
    e = pl.program_id(1)

    @pl.when(e == 0)
    def _():
        out_ref[...] = x1_ref[...]

    gu = _dot(h2_ref[...], wgu_ref[0])
    act = (jax.nn.silu(gu[:, :D_EXPERT]) * gu[:, D_EXPERT:]).astype(BF16)
    y = _dot(act, wdn_ref[0])
    lane = lax.broadcasted_iota(jnp.int32, comb_ref.shape, 1)
    c_e = jnp.sum(jnp.where(lane == e, comb_ref[...], 0.0), axis=1, keepdims=True)
    out_ref[...] += c_e * y

    if final_norm:
        @pl.when(e == N_EXPERTS - 1)
        def _():
            out_ref[...] = _rms(out_ref[...], fn_ref[...])


def _moe(h2, x1, comb, lw, fn, *, tm, final_norm):
    t, d = x1.shape
    return pl.pallas_call(
        functools.partial(_moe_kernel, final_norm=final_norm),
        grid=(t // tm, N_EXPERTS),
        in_specs=[pl.BlockSpec((tm, d), lambda i, e: (i, 0)),
                  pl.BlockSpec((tm, d), lambda i, e: (i, 0)),
                  pl.BlockSpec((tm, LANES), lambda i, e: (i, 0)),
                  pl.BlockSpec((1, d, 2 * D_EXPERT), lambda i, e: (e, 0, 0)),
                  pl.BlockSpec((1, D_EXPERT, d), lambda i, e: (e, 0, 0)),
                  pl.BlockSpec((1, d), lambda i, e: (0, 0))],
        out_specs=pl.BlockSpec((tm, d), lambda i, e: (i, 0)),
        out_shape=jax.ShapeDtypeStruct((t, d), F32),
        compiler_params=pltpu.CompilerParams(dimension_semantics=("arbitrary", "arbitrary"),
                                             vmem_limit_bytes=VMEM_LIMIT),
        name="moe",
    )(h2, x1, comb, lw["w_gate_up"], lw["w_down"], fn)


def _pack_layer(l, w_in, norm_mix, q_norm, w_q_up, kv_norm, w_kv_up, sg_norm, w_s, b_s, conv_w,
                w_a_out, w_b_out, w_c_out, w_o, norm_ffn, w_group_router, b_group_router,
                w_expert_router, b_expert_router, w_gate_up, w_down):
    d = w_in.shape[1]
    wi = w_in[l]
    o_kr = Q_RANK + KV_RANK
    kr_slot = jnp.zeros((d, HEAD_PAD), F32).at[:, QK_NOPE:QK_HEAD].set(wi[:, o_kr:o_kr + QK_ROPE])
    w_in_p = jnp.concatenate([wi[:, :o_kr], kr_slot, wi[:, o_kr + QK_ROPE:]], axis=1).astype(BF16)
    wq = w_q_up[l].reshape(Q_RANK, MLA_HEADS, QK_HEAD)
    wq = jnp.pad(wq, ((0, 0), (0, 0), (0, HEAD_PAD - QK_HEAD))).reshape(Q_RANK, MLA_HEADS * HEAD_PAD)
    wkv = w_kv_up[l].reshape(KV_RANK, MLA_HEADS, QK_NOPE + V_HEAD)
    wk = jnp.pad(wkv[:, :, :QK_NOPE], ((0, 0), (0, 0), (0, HEAD_PAD - QK_NOPE)))
    wk = wk.reshape(KV_RANK, MLA_HEADS * HEAD_PAD)
    wv = wkv[:, :, QK_NOPE:].reshape(KV_RANK, MLA_HEADS * V_HEAD)
    w_r = jnp.zeros((d, LANES), F32)
    w_r = w_r.at[:, :N_EXPERTS].set(w_expert_router[l])
    w_r = w_r.at[:, GROUP_LANE0:GROUP_LANE0 + N_GROUPS].set(w_group_router[l])
    b_r = jnp.zeros((1, LANES), F32)
    b_r = b_r.at[0, :N_EXPERTS].set(b_expert_router[l])
    b_r = b_r.at[0, GROUP_LANE0:GROUP_LANE0 + N_GROUPS].set(b_group_router[l])
    return {
        "norm_mix": norm_mix[l][None, :], "w_in": w_in_p,
        "q_norm": q_norm[l][None, :], "w_q": wq.astype(BF16),
        "kv_norm": kv_norm[l][None, :], "w_k": wk.astype(BF16), "w_v": wv.astype(BF16),
        "sg_norm": sg_norm[l][None, :], "w_s": w_s[l],
        "b_s": jnp.broadcast_to(b_s[l][:, :, None], (SG_GROUPS, CHUNK, SG_GROUP_DIM)),
        "conv_w": conv_w[l],
        "w_a_out": w_a_out[l].astype(BF16), "w_b_out": w_b_out[l].astype(BF16),
        "w_c_out": w_c_out[l].astype(BF16), "w_o": w_o[l].astype(BF16),
        "norm_ffn": norm_ffn[l][None, :], "w_router": w_r.astype(BF16), "b_router": b_r,
        "w_gate_up": w_gate_up[l].astype(BF16), "w_down": w_down[l].astype(BF16),
    }


def _rope_tables(positions):
    half = QK_ROPE // 2
    inv = ROPE_BASE ** (-jnp.arange(0, QK_ROPE, 2, dtype=F32) / QK_ROPE)
    ang = positions.astype(F32).reshape(-1)[:, None] * inv
    cos, sin = jnp.cos(ang), jnp.sin(ang)
    t = ang.shape[0]
    ones = jnp.ones((t, QK_NOPE), F32)
    z16 = jnp.zeros((t, half), F32)
    z32 = jnp.zeros((t, HEAD_PAD - QK_HEAD), F32)
    z64 = jnp.zeros((t, QK_NOPE), F32)
    rc = jnp.concatenate([ones, cos, cos, z32], axis=1)
    rm = jnp.concatenate([z64, -sin, z16, z32], axis=1)
    rp = jnp.concatenate([z64, z16, sin, z32], axis=1)
    return rc, rm, rp


def kernel(x, positions, norm_mix, w_in, q_norm, w_q_up, kv_norm, w_kv_up, sg_norm, w_s, b_s, conv_w,
           w_a_out, w_b_out, w_c_out, w_o, norm_ffn, w_group_router, b_group_router, w_expert_router,
           b_expert_router, w_gate_up, w_down, final_norm):
    batch, seq, d = x.shape
    depth = w_in.shape[0]
    tm = min(512, seq)
    tq = min(512, seq)
    tm_moe = min(1024, seq)
    rc, rm, rp = _rope_tables(positions)
    x2 = x.reshape(batch * seq, d)
    fn = final_norm[None, :]
    for l in range(depth):
        lw = _pack_layer(l, w_in, norm_mix, q_norm, w_q_up, kv_norm, w_kv_up, sg_norm, w_s, b_s, conv_w,
                         w_a_out, w_b_out, w_c_out, w_o, norm_ffn, w_group_router, b_group_router,
                         w_expert_router, b_expert_router, w_gate_up, w_down)
        q, k, v, part, ga = _mixer_pre(x2, lw, rc, rm, rp, seq=seq, tm=tm)
        o = _attention(q, k, v, batch=batch, seq=seq, tq=tq)
        x1, h2, comb = _mixer_post(o, part, ga, x2, lw, tm=tm)
        x2 = _moe(h2, x1, comb, lw, fn, tm=tm_moe, final_norm=(l == depth - 1))
    return x2.reshape(batch, seq, d)
```

```python
import functools

import jax
import jax.numpy as jnp
from jax import lax
from jax.experimental import pallas as pl
from jax.experimental.pallas import tpu as pltpu

F32 = jnp.float32
BF16 = jnp.bfloat16

EPS = 1e-6
MLA_HEADS = 8
QK_NOPE = 64
QK_ROPE = 32
QK_HEAD = QK_NOPE + QK_ROPE
V_HEAD = 64
Q_RANK = 384
KV_RANK = 256
ROPE_BASE = 10000.0
SG_GROUPS = 4
SG_GROUP_DIM = 128
SG_WIDTH = SG_GROUPS * SG_GROUP_DIM
CHUNK = 128
CONV_WIDTH = 512
CONV_K = 3
N_GROUPS = 4
EXPERTS_PER_GROUP = 8
N_EXPERTS = N_GROUPS * EXPERTS_PER_GROUP
D_EXPERT = 256

LANES = 128
HEAD_PAD = 128
MASK_VALUE = -1e30
VMEM_LIMIT = 56 * 1024 * 1024

_P1 = Q_RANK + KV_RANK + HEAD_PAD
_P2 = _P1 + 2 * SG_WIDTH
_P3 = _P2 + 3 * CONV_WIDTH
GROUP_LANE0 = N_EXPERTS


def _dot(a, b):
    return jnp.dot(a, b, preferred_element_type=F32)


def _rms(xf, g):
    return xf * lax.rsqrt(jnp.mean(xf * xf, axis=-1, keepdims=True) + EPS) * g


def _gelu(x):
    return 0.5 * x * (1.0 + jnp.tanh(0.7978845608028654 * (x + 0.044715 * (x * x * x))))


def _rope(t, rc, rm, rp):
    return t * rc + pltpu.roll(t, HEAD_PAD - QK_ROPE // 2, 1) * rm + pltpu.roll(t, QK_ROPE // 2, 1) * rp


def _const_spec(shape):
    nd = len(shape)
    return pl.BlockSpec(shape, lambda *_: (0,) * nd, pipeline_mode=pl.Buffered(1))


def _mixer_pre_kernel(x_ref, nm_ref, win_ref, qn_ref, wq_ref, kvn_ref, wk_ref, wv_ref, sgn_ref,
                      ws_ref, bs_ref, cw_ref, wb_ref, wc_ref, rc_ref, rm_ref, rp_ref,
                      q_ref, k_ref, v_ref, part_ref, ga_ref,
                      halo_ref, ybin_ref, *, tiles_per_seq, tm):
    i = pl.program_id(0)
    d_model = x_ref.shape[1]
    h = _rms(x_ref[...], nm_ref[...]).astype(BF16)

    p1 = _dot(h, win_ref[:, 0:_P1])
    rc, rm, rp = rc_ref[...], rm_ref[...], rp_ref[...]
    qn = _rms(p1[:, :Q_RANK], qn_ref[...]).astype(BF16)
    qf = _dot(qn, wq_ref[...])
    scale = QK_HEAD ** -0.5
    for hd in range(MLA_HEADS):
        sl = slice(hd * HEAD_PAD, (hd + 1) * HEAD_PAD)
        q_ref[:, sl] = (_rope(qf[:, sl], rc, rm, rp) * scale).astype(BF16)
    kvn = _rms(p1[:, Q_RANK:Q_RANK + KV_RANK], kvn_ref[...]).astype(BF16)
    kr = _rope(p1[:, Q_RANK + KV_RANK:_P1], rc, rm, rp)
    kf = _dot(kvn, wk_ref[...])
    for hd in range(MLA_HEADS):
        sl = slice(hd * HEAD_PAD, (hd + 1) * HEAD_PAD)
        k_ref[:, sl] = (kf[:, sl] + kr).astype(BF16)
    v_ref[...] = _dot(kvn, wv_ref[...]).astype(BF16)

    p2 = _dot(h, win_ref[:, _P1:_P2])
    u = _gelu(p2[:, :SG_WIDTH])
    vb = _rms(_gelu(p2[:, SG_WIDTH:]), sgn_ref[...]).astype(BF16)
    r_i = lax.broadcasted_iota(jnp.int32, (CHUNK, CHUNK), 0)
    c_i = lax.broadcasted_iota(jnp.int32, (CHUNK, CHUNK), 1)
    for g in range(SG_GROUPS):
        wg = jnp.where(c_i <= r_i, ws_ref[g], 0.0).astype(BF16)
        gs = slice(g * SG_GROUP_DIM, (g + 1) * SG_GROUP_DIM)
        for c in range(tm // CHUNK):
            cs = slice(c * CHUNK, (c + 1) * CHUNK)
            mixed = _dot(wg, vb[cs, gs]) + bs_ref[g]
            ybin_ref[cs, gs] = (u[cs, gs] * mixed).astype(BF16)

    @pl.when(i % tiles_per_seq == 0)
    def _():
        halo_ref[...] = jnp.zeros_like(halo_ref)

    p3 = _dot(h, win_ref[:, _P2:_P3])
    z = p3[:, CONV_WIDTH:2 * CONV_WIDTH] * p3[:, 2 * CONV_WIDTH:]
    row = lax.broadcasted_iota(jnp.int32, z.shape, 0)
    halo = halo_ref[...]
    last1 = halo[7:8, :]
    last2 = halo[6:7, :]
    z1 = jnp.where(row == 0, last1, pltpu.roll(z, 1, 0))
    z2 = jnp.where(row == 0, last2, jnp.where(row == 1, last1, pltpu.roll(z, 2, 0)))
    y = cw_ref[0:1, :] * z2 + cw_ref[1:2, :] * z1 + cw_ref[2:3, :] * z
    yc_in = (p3[:, :CONV_WIDTH] * y).astype(BF16)
    halo_ref[...] = z[tm - 8:tm, :]

    p4 = _dot(h, win_ref[:, _P3:_P3 + 3 * d_model])
    yb = _dot(ybin_ref[...], wb_ref[...])
    yc = _dot(yc_in, wc_ref[...])
    part = (jax.nn.sigmoid(p4[:, d_model:2 * d_model]) * yb
            + jax.nn.sigmoid(p4[:, 2 * d_model:]) * yc)
    part_ref[...] = part.astype(BF16)
    ga_ref[...] = jax.nn.sigmoid(p4[:, :d_model]).astype(BF16)


def _mixer_pre(x2, lw, rc, rm, rp, *, seq, tm):
    t, d = x2.shape
    row = lambda w: pl.BlockSpec((tm, w), lambda i: (i, 0))
    consts = [lw["norm_mix"], lw["w_in"], lw["q_norm"], lw["w_q"], lw["kv_norm"], lw["w_k"], lw["w_v"],
              lw["sg_norm"], lw["w_s"], lw["b_s"], lw["conv_w"], lw["w_b_out"], lw["w_c_out"]]
    kern = functools.partial(_mixer_pre_kernel, tiles_per_seq=seq // tm, tm=tm)
    return pl.pallas_call(
        kern,
        grid=(t // tm,),
        in_specs=[row(d)] + [_const_spec(c.shape) for c in consts] + [row(HEAD_PAD)] * 3,
        out_specs=[row(MLA_HEADS * HEAD_PAD), row(MLA_HEADS * HEAD_PAD), row(MLA_HEADS * V_HEAD), row(d), row(d)],
        out_shape=[jax.ShapeDtypeStruct((t, MLA_HEADS * HEAD_PAD), BF16),
                   jax.ShapeDtypeStruct((t, MLA_HEADS * HEAD_PAD), BF16),
                   jax.ShapeDtypeStruct((t, MLA_HEADS * V_HEAD), BF16),
                   jax.ShapeDtypeStruct((t, d), BF16),
                   jax.ShapeDtypeStruct((t, d), BF16)],
        scratch_shapes=[pltpu.VMEM((8, CONV_WIDTH), F32), pltpu.VMEM((tm, SG_WIDTH), BF16)],
        compiler_params=pltpu.CompilerParams(dimension_semantics=("arbitrary",), vmem_limit_bytes=VMEM_LIMIT),
        name="mixer_pre",
    )(x2, *consts, rc, rm, rp)


def _attn_kernel(q_ref, k_ref, v_ref, o_ref, m_ref, l_ref, acc_ref, *, tq):
    qi = pl.program_id(2)
    m_ref[...] = jnp.full_like(m_ref, MASK_VALUE)
    l_ref[...] = jnp.zeros_like(l_ref)
    acc_ref[...] = jnp.zeros_like(acc_ref)

    def step(j, masked):
        start = pl.multiple_of(j * tq, tq)
        v_j = v_ref[pl.ds(start, tq), :]
        for sub in range(2):
            sl = slice(sub * HEAD_PAD, (sub + 1) * HEAD_PAD)
            k_j = k_ref[pl.ds(start, tq), sl]
            s = lax.dot_general(q_ref[:, sl], k_j, (((1,), (1,)), ((), ())), preferred_element_type=F32)
            if masked:
                r_i = lax.broadcasted_iota(jnp.int32, s.shape, 0)
                c_i = lax.broadcasted_iota(jnp.int32, s.shape, 1)
                s = jnp.where(c_i <= r_i, s, MASK_VALUE)
            m_prev = m_ref[sub]
            m_next = jnp.maximum(m_prev, jnp.max(s, axis=1, keepdims=True))
            alpha = jnp.exp(m_prev - m_next)
            p = jnp.exp(s - jnp.tile(m_next, (1, tq // LANES)))
            l_ref[sub] = alpha * l_ref[sub] + jnp.sum(p, axis=1, keepdims=True)
            m_ref[sub] = m_next
            acc_ref[sub] = alpha * acc_ref[sub] + _dot(p.astype(BF16), v_j)

    def body(j, carry):
        step(j, False)
        return carry

    lax.fori_loop(0, qi, body, 0)
    step(qi, True)
    lane = lax.broadcasted_iota(jnp.int32, (tq, LANES), 1)
    o0 = acc_ref[0] / l_ref[0]
    o1 = acc_ref[1] / l_ref[1]
    o_ref[...] = jnp.where(lane < V_HEAD, o0, o1).astype(BF16)


def _attention(q, k, v, *, batch, seq, tq):
    t = q.shape[0]
    nq = seq // tq
    pairs = MLA_HEADS // 2
    return pl.pallas_call(
        functools.partial(_attn_kernel, tq=tq),
        grid=(batch, pairs, nq),
        in_specs=[pl.BlockSpec((tq, 2 * HEAD_PAD), lambda b, hp, qi: (b * nq + qi, hp)),
                  pl.BlockSpec((seq, 2 * HEAD_PAD), lambda b, hp, qi: (b, hp)),
                  pl.BlockSpec((seq, 2 * V_HEAD), lambda b, hp, qi: (b, hp))],
        out_specs=pl.BlockSpec((tq, 2 * V_HEAD), lambda b, hp, qi: (b * nq + qi, hp)),
        out_shape=jax.ShapeDtypeStruct((t, MLA_HEADS * V_HEAD), BF16),
        scratch_shapes=[pltpu.VMEM((2, tq, LANES), F32), pltpu.VMEM((2, tq, LANES), F32),
                        pltpu.VMEM((2, tq, LANES), F32)],
        compiler_params=pltpu.CompilerParams(dimension_semantics=("arbitrary",) * 3, vmem_limit_bytes=VMEM_LIMIT),
        name="attention",
    )(q, k, v)


def _pack_pair(lo, hi):
    lo_b = lax.bitcast_convert_type(lo, jnp.uint32)
    hi_b = lax.bitcast_convert_type(hi, jnp.uint32)
    return (hi_b & jnp.uint32(0xFFFF0000)) | (lo_b >> 16)


def _unpack_pair(p):
    lo = lax.bitcast_convert_type(p << 16, F32)
    hi = lax.bitcast_convert_type(p & jnp.uint32(0xFFFF0000), F32)
    return lo, hi


def _route(logits):
    lane = lax.broadcasted_iota(jnp.int32, logits.shape, 1)
    lane_f = lane.astype(F32)
    big = float(LANES)
    is_grp = (lane >= GROUP_LANE0) & (lane < GROUP_LANE0 + N_GROUPS)
    g_log = jnp.where(is_grp, logits, MASK_VALUE)
    g_max = jnp.max(g_log, axis=1, keepdims=True)
    g_sum = jnp.sum(jnp.exp(g_log - g_max), axis=1, keepdims=True)
    g_p = 1.0 / g_sum
    g_idx = jnp.min(jnp.where(g_log == g_max, lane_f, big), axis=1, keepdims=True) - float(GROUP_LANE0)
    lo = g_idx * float(EXPERTS_PER_GROUP)
    in_grp = (lane_f >= lo) & (lane_f < lo + float(EXPERTS_PER_GROUP))
    e_log = jnp.where(in_grp, logits, MASK_VALUE)
    e_max = jnp.max(e_log, axis=1, keepdims=True)
    e_exp = jnp.exp(e_log - e_max)
    e_prob = e_exp / jnp.sum(e_exp, axis=1, keepdims=True)
    cand = jnp.where(in_grp, e_prob, -1.0)
    p1 = jnp.max(cand, axis=1, keepdims=True)
    i1 = jnp.min(jnp.where(cand == p1, lane_f, big), axis=1, keepdims=True)
    cand2 = jnp.where(lane_f == i1, -1.0, cand)
    p2 = jnp.max(cand2, axis=1, keepdims=True)
    i2 = jnp.min(jnp.where(cand2 == p2, lane_f, big), axis=1, keepdims=True)
    denom = p1 + p2
    return g_p * (p1 / denom), g_p * (p2 / denom), i1, i2


_L_W1, _L_W2, _L_E1, _L_E2, _L_R1, _L_R2 = range(6)


def _mixer_post_kernel(o_ref, part_ref, ga_ref, x_ref, wa_ref, wo_ref, nf_ref, wr_ref, br_ref,
                       x1_ref, h2p_ref, info_ref, cnt_ref, run_ref, *, tm):
    i = pl.program_id(0)
    half = x_ref.shape[1] // 2

    @pl.when(i == 0)
    def _():
        run_ref[...] = jnp.zeros_like(run_ref)

    ya = _dot(o_ref[...], wa_ref[...])
    merged = ga_ref[...].astype(F32) * ya + part_ref[...].astype(F32)
    x1 = x_ref[...] + _dot(merged.astype(BF16), wo_ref[...])
    x1_ref[...] = x1
    h2 = _rms(x1, nf_ref[...]).astype(BF16)
    h2f = h2.astype(F32)
    h2p_ref[...] = _pack_pair(h2f[:, :half], h2f[:, half:])
    w1, w2, i1, i2 = _route(_dot(h2, wr_ref[...]) + br_ref[...])

    lane_f = lax.broadcasted_iota(jnp.int32, (tm, LANES), 1).astype(F32)
    oh1 = lane_f == i1
    oh2 = lane_f == i2
    both = jnp.where(oh1, 1.0, 0.0) + jnp.where(oh2, 1.0, 0.0)
    r_i = lax.broadcasted_iota(jnp.int32, (tm, tm), 0)
    c_i = lax.broadcasted_iota(jnp.int32, (tm, tm), 1)
    earlier = jnp.where(c_i < r_i, 1.0, 0.0).astype(BF16)
    before = _dot(earlier, both.astype(BF16)) + run_ref[0:1, :]
    rank1 = jnp.sum(jnp.where(oh1, before, 0.0), axis=1, keepdims=True)
    rank2 = jnp.sum(jnp.where(oh2, before, 0.0), axis=1, keepdims=True)
    run_ref[0:1, :] = run_ref[0:1, :] + jnp.sum(both, axis=0, keepdims=True)
    cnt_ref[...] = run_ref[...]

    lane = lax.broadcasted_iota(jnp.int32, (tm, LANES), 1)
    info = jnp.zeros((tm, LANES), F32)
    for ln, val in ((_L_W1, w1), (_L_W2, w2), (_L_E1, i1), (_L_E2, i2), (_L_R1, rank1), (_L_R2, rank2)):
        info = jnp.where(lane == ln, val, info)
    info_ref[...] = info


def _mixer_post(o, part, ga, x2, lw, *, tm):
    t, d = x2.shape
    row = lambda w: pl.BlockSpec((tm, w), lambda i: (i, 0))
    consts = [lw["w_a_out"], lw["w_o"], lw["norm_ffn"], lw["w_router"], lw["b_router"]]
    return pl.pallas_call(
        functools.partial(_mixer_post_kernel, tm=tm),
        grid=(t // tm,),
        in_specs=[row(o.shape[1]), row(d), row(d), row(d)] + [_const_spec(c.shape) for c in consts],
        out_specs=[row(d), row(d // 2), row(LANES), pl.BlockSpec((8, LANES), lambda i: (0, 0))],
        out_shape=[jax.ShapeDtypeStruct((t, d), F32), jax.ShapeDtypeStruct((t, d // 2), jnp.uint32),
                   jax.ShapeDtypeStruct((t, LANES), F32), jax.ShapeDtypeStruct((8, LANES), F32)],
        scratch_shapes=[pltpu.VMEM((8, LANES), F32)],
        compiler_params=pltpu.CompilerParams(dimension_semantics=("arbitrary",), vmem_limit_bytes=VMEM_LIMIT),
        name="mixer_post",
    )(o, part, ga, x2, *consts)


def _gather_rows(idx_ref, n_rows, src_hbm, dst_buf, slot, sem):
    def body(r, carry):
        src = idx_ref[0, 0, r]
        pltpu.make_async_copy(src_hbm.at[pl.ds(src, 1)], dst_buf.at[slot, pl.ds(r, 1)], sem.at[slot]).start()
        return carry

    lax.fori_loop(0, n_rows, body, 0, unroll=8)


def _wait_rows(n_rows, src_hbm, dst_buf, slot, sem):
    pltpu.make_async_copy(src_hbm.at[pl.ds(0, n_rows)], dst_buf.at[slot], sem.at[slot]).wait()


def _expert_kernel(te_ref, nu_ref, idx_cur_ref, idx_nxt_ref, h2p_hbm, wgu_ref, wdn_ref, ys_ref,
                   xbuf, sem, *, tme):
    i = pl.program_id(0)
    n_used = nu_ref[0]
    half = wgu_ref.shape[1] // 2

    @pl.when(i == 0)
    def _():
        _gather_rows(idx_cur_ref, tme, h2p_hbm, xbuf, 0, sem)

    @pl.when(i + 1 < n_used)
    def _():
        _gather_rows(idx_nxt_ref, tme, h2p_hbm, xbuf, (i + 1) % 2, sem)

    @pl.when(i < n_used)
    def _():
        slot = i % 2
        _wait_rows(tme, h2p_hbm, xbuf, slot, sem)
        lo, hi = _unpack_pair(xbuf[slot])
        gu = _dot(lo.astype(BF16), wgu_ref[0, :half, :]) + _dot(hi.astype(BF16), wgu_ref[0, half:, :])
        act = (jax.nn.silu(gu[:, :D_EXPERT]) * gu[:, D_EXPERT:]).astype(BF16)
        y = _dot(act, wdn_ref[0]).astype(BF16).astype(F32)
        ys_ref[...] = _pack_pair(y[:, :half], y[:, half:])

    @pl.when(i >= n_used)
    def _():
        ys_ref[...] = jnp.zeros_like(ys_ref)


def _experts(h2p, tok_of_slot, tile_expert, n_used, lw, *, tme):
    t, half = h2p.shape
    n_tiles = tok_of_slot.shape[0]
    d = 2 * half
    grid_spec = pltpu.PrefetchScalarGridSpec(
        num_scalar_prefetch=2,
        grid=(n_tiles,),
        in_specs=[pl.BlockSpec((1, 1, tme), lambda i, te, nu: (i, 0, 0), memory_space=pltpu.SMEM),
                  pl.BlockSpec((1, 1, tme), lambda i, te, nu: (jnp.minimum(i + 1, n_tiles - 1), 0, 0),
                               memory_space=pltpu.SMEM),
                  pl.BlockSpec(memory_space=pl.ANY),
                  pl.BlockSpec((1, d, 2 * D_EXPERT), lambda i, te, nu: (te[i], 0, 0)),
                  pl.BlockSpec((1, D_EXPERT, d), lambda i, te, nu: (te[i], 0, 0))],
        out_specs=pl.BlockSpec((tme, half), lambda i, te, nu: (i, 0)),
        scratch_shapes=[pltpu.VMEM((2, tme, half), jnp.uint32), pltpu.SemaphoreType.DMA((2,))],
    )
    return pl.pallas_call(
        functools.partial(_expert_kernel, tme=tme),
        grid_spec=grid_spec,
        out_shape=jax.ShapeDtypeStruct((n_tiles * tme, half), jnp.uint32),
        compiler_params=pltpu.CompilerParams(dimension_semantics=("arbitrary",), vmem_limit_bytes=VMEM_LIMIT),
        name="experts",
    )(tile_expert, n_used, tok_of_slot, tok_of_slot, h2p, lw["w_gate_up"], lw["w_down"])


def _combine_kernel(s_cur_ref, s_nxt_ref, x1_ref, info_ref, ys_hbm, fn_ref, out_ref, ybuf, sem,
                    *, tmc, final_norm):
    i = pl.program_id(0)
    n = pl.num_programs(0)
    half = x1_ref.shape[1] // 2

    @pl.when(i == 0)
    def _():
        _gather_rows(s_cur_ref, 2 * tmc, ys_hbm, ybuf, 0, sem)

    @pl.when(i + 1 < n)
    def _():
        _gather_rows(s_nxt_ref, 2 * tmc, ys_hbm, ybuf, (i + 1) % 2, sem)

    slot = i % 2
    _wait_rows(2 * tmc, ys_hbm, ybuf, slot, sem)
    lo1, hi1 = _unpack_pair(ybuf[slot, 0:tmc, :])
    lo2, hi2 = _unpack_pair(ybuf[slot, tmc:2 * tmc, :])
    info = info_ref[...]
    w1 = info[:, _L_W1:_L_W1 + 1]
    w2 = info[:, _L_W2:_L_W2 + 1]
    lo = x1_ref[:, :half] + w1 * lo1 + w2 * lo2
    hi = x1_ref[:, half:] + w1 * hi1 + w2 * hi2
    if final_norm:
        ms = (jnp.sum(lo * lo, axis=-1, keepdims=True) + jnp.sum(hi * hi, axis=-1, keepdims=True)) / (2 * half)
        inv = lax.rsqrt(ms + EPS)
        lo = lo * inv * fn_ref[:, :half]
        hi = hi * inv * fn_ref[:, half:]
    out_ref[:, :half] = lo
    out_ref[:, half:] = hi


def _combine(x1, info, ys, slots, fn, *, tmc, final_norm):
    t, d = x1.shape
    n = t // tmc
    return pl.pallas_call(
        functools.partial(_combine_kernel, tmc=tmc, final_norm=final_norm),
        grid=(n,),
        in_specs=[pl.BlockSpec((1, 1, 2 * tmc), lambda i: (i, 0, 0), memory_space=pltpu.SMEM),
                  pl.BlockSpec((1, 1, 2 * tmc), lambda i: (jnp.minimum(i + 1, n - 1), 0, 0),
                               memory_space=pltpu.SMEM),
                  pl.BlockSpec((tmc, d), lambda i: (i, 0)),
                  pl.BlockSpec((tmc, LANES), lambda i: (i, 0)),
                  pl.BlockSpec(memory_space=pl.ANY),
                  pl.BlockSpec((1, d), lambda i: (0, 0))],
        out_specs=pl.BlockSpec((tmc, d), lambda i: (i, 0)),
        out_shape=jax.ShapeDtypeStruct((t, d), F32),
        scratch_shapes=[pltpu.VMEM((2, 2 * tmc, d // 2), jnp.uint32), pltpu.SemaphoreType.DMA((2,))],
        compiler_params=pltpu.CompilerParams(dimension_semantics=("arbitrary",), vmem_limit_bytes=VMEM_LIMIT),
        name="combine",
    )(slots, slots, x1, info, ys, fn)


def _moe(x1, h2p, info, cnt, lw, fn, *, tme, tmc, final_norm):
    t = x1.shape[0]
    n_tiles = (2 * t) // tme + N_EXPERTS
    counts = cnt[0, :N_EXPERTS].astype(jnp.int32)
    padded = ((counts + tme - 1) // tme) * tme
    ends = jnp.cumsum(padded)
    starts = ends - padded
    e1 = info[:, _L_E1].astype(jnp.int32)
    e2 = info[:, _L_E2].astype(jnp.int32)
    slot1 = starts[e1] + info[:, _L_R1].astype(jnp.int32)
    slot2 = starts[e2] + info[:, _L_R2].astype(jnp.int32)
    tok = jnp.arange(t, dtype=jnp.int32)
    tok_of_slot = jnp.zeros((n_tiles * tme,), jnp.int32).at[slot1].set(tok).at[slot2].set(tok)
    tile_start = jnp.arange(n_tiles, dtype=jnp.int32) * tme
    tile_expert = jnp.minimum(jnp.searchsorted(ends, tile_start, side="right"), N_EXPERTS - 1).astype(jnp.int32)
    n_used = (ends[-1:] // tme).astype(jnp.int32)
    ys = _experts(h2p, tok_of_slot.reshape(n_tiles, 1, tme), tile_expert, n_used, lw, tme=tme)
    slots = jnp.concatenate([slot1.reshape(t // tmc, 1, tmc), slot2.reshape(t // tmc, 1, tmc)], axis=2)
    return _combine(x1, info, ys, slots, fn, tmc=tmc, final_norm=final_norm)


def _pack_layer(l, w_in, norm_mix, q_norm, w_q_up, kv_norm, w_kv_up, sg_norm, w_s, b_s, conv_w,
                w_a_out, w_b_out, w_c_out, w_o, norm_ffn, w_group_router, b_group_router,
                w_expert_router, b_expert_router, w_gate_up, w_down):
    d = w_in.shape[1]
    wi = w_in[l]
    o_kr = Q_RANK + KV_RANK
    kr_slot = jnp.zeros((d, HEAD_PAD), F32).at[:, QK_NOPE:QK_HEAD].set(wi[:, o_kr:o_kr + QK_ROPE])
    w_in_p = jnp.concatenate([wi[:, :o_kr], kr_slot, wi[:, o_kr + QK_ROPE:]], axis=1).astype(BF16)
    wq = w_q_up[l].reshape(Q_RANK, MLA_HEADS, QK_HEAD)
    wq = jnp.pad(wq, ((0, 0), (0, 0), (0, HEAD_PAD - QK_HEAD))).reshape(Q_RANK, MLA_HEADS * HEAD_PAD)
    wkv = w_kv_up[l].reshape(KV_RANK, MLA_HEADS, QK_NOPE + V_HEAD)
    wk = jnp.pad(wkv[:, :, :QK_NOPE], ((0, 0), (0, 0), (0, HEAD_PAD - QK_NOPE)))
    wk = wk.reshape(KV_RANK, MLA_HEADS * HEAD_PAD)
    wv = wkv[:, :, QK_NOPE:].reshape(KV_RANK, MLA_HEADS * V_HEAD)
    w_r = jnp.zeros((d, LANES), F32)
    w_r = w_r.at[:, :N_EXPERTS].set(w_expert_router[l])
    w_r = w_r.at[:, GROUP_LANE0:GROUP_LANE0 + N_GROUPS].set(w_group_router[l])
    b_r = jnp.zeros((1, LANES), F32)
    b_r = b_r.at[0, :N_EXPERTS].set(b_expert_router[l])
    b_r = b_r.at[0, GROUP_LANE0:GROUP_LANE0 + N_GROUPS].set(b_group_router[l])
    return {
        "norm_mix": norm_mix[l][None, :], "w_in": w_in_p,
        "q_norm": q_norm[l][None, :], "w_q": wq.astype(BF16),
        "kv_norm": kv_norm[l][None, :], "w_k": wk.astype(BF16), "w_v": wv.astype(BF16),
        "sg_norm": sg_norm[l][None, :], "w_s": w_s[l],
        "b_s": jnp.broadcast_to(b_s[l][:, :, None], (SG_GROUPS, CHUNK, SG_GROUP_DIM)),
        "conv_w": conv_w[l],
        "w_a_out": w_a_out[l].astype(BF16), "w_b_out": w_b_out[l].astype(BF16),
        "w_c_out": w_c_out[l].astype(BF16), "w_o": w_o[l].astype(BF16),
        "norm_ffn": norm_ffn[l][None, :], "w_router": w_r.astype(BF16), "b_router": b_r,
        "w_gate_up": w_gate_up[l].astype(BF16), "w_down": w_down[l].astype(BF16),
    }


def _rope_tables(positions):
    half = QK_ROPE // 2
    inv = ROPE_BASE ** (-jnp.arange(0, QK_ROPE, 2, dtype=F32) / QK_ROPE)
    ang = positions.astype(F32).reshape(-1)[:, None] * inv
    cos, sin = jnp.cos(ang), jnp.sin(ang)
    t = ang.shape[0]
    ones = jnp.ones((t, QK_NOPE), F32)
    z16 = jnp.zeros((t, half), F32)
    z32 = jnp.zeros((t, HEAD_PAD - QK_HEAD), F32)
    z64 = jnp.zeros((t, QK_NOPE), F32)
    rc = jnp.concatenate([ones, cos, cos, z32], axis=1)
    rm = jnp.concatenate([z64, -sin, z16, z32], axis=1)
    rp = jnp.concatenate([z64, z16, sin, z32], axis=1)
    return rc, rm, rp


def kernel(x, positions, norm_mix, w_in, q_norm, w_q_up, kv_norm, w_kv_up, sg_norm, w_s, b_s, conv_w,
           w_a_out, w_b_out, w_c_out, w_o, norm_ffn, w_group_router, b_group_router, w_expert_router,
           b_expert_router, w_gate_up, w_down, final_norm):
    batch, seq, d = x.shape
    depth = w_in.shape[0]
    tm = min(512, seq)
    tq = min(512, seq)
    tme = 256
    tmc = min(256, seq)
    rc, rm, rp = _rope_tables(positions)
    x2 = x.reshape(batch * seq, d)
    fn = final_norm[None, :]
    for l in range(depth):
        lw = _pack_layer(l, w_in, norm_mix, q_norm, w_q_up, kv_norm, w_kv_up, sg_norm, w_s, b_s, conv_w,
                         w_a_out, w_b_out, w_c_out, w_o, norm_ffn, w_group_router, b_group_router,
                         w_expert_router, b_expert_router, w_gate_up, w_down)
        q, k, v, part, ga = _mixer_pre(x2, lw, rc, rm, rp, seq=seq, tm=tm)
        o = _attention(q, k, v, batch=batch, seq=seq, tq=tq)
        x1, h2p, info, cnt = _mixer_post(o, part, ga, x2, lw, tm=tm)
        x2 = _moe(x1, h2p, info, cnt, lw, fn, tme=tme, tmc=tmc, final_norm=(l == depth - 1))
    return x2.reshape(batch, seq, d)
```

```python
import functools

import jax
import jax.numpy as jnp
from jax import lax
from jax.experimental import pallas as pl
from jax.experimental.pallas import tpu as pltpu

F32 = jnp.float32
BF16 = jnp.bfloat16

EPS = 1e-6
MLA_HEADS = 8
QK_NOPE = 64
QK_ROPE = 32
QK_HEAD = QK_NOPE + QK_ROPE
V_HEAD = 64
Q_RANK = 384
KV_RANK = 256
ROPE_BASE = 10000.0
SG_GROUPS = 4
SG_GROUP_DIM = 128
SG_WIDTH = SG_GROUPS * SG_GROUP_DIM
CHUNK = 128
CONV_WIDTH = 512
CONV_K = 3
N_GROUPS = 4
EXPERTS_PER_GROUP = 8
N_EXPERTS = N_GROUPS * EXPERTS_PER_GROUP
D_EXPERT = 256

LANES = 128
HEAD_PAD = 128
MASK_VALUE = -1e30
VMEM_LIMIT = 56 * 1024 * 1024

_P1 = Q_RANK + KV_RANK + HEAD_PAD
_P2 = _P1 + 2 * SG_WIDTH
_P3 = _P2 + 3 * CONV_WIDTH
GROUP_LANE0 = N_EXPERTS


def _dot(a, b):
    return jnp.dot(a, b, preferred_element_type=F32)


def _rms(xf, g):
    return xf * lax.rsqrt(jnp.mean(xf * xf, axis=-1, keepdims=True) + EPS) * g


def _gelu(x):
    return 0.5 * x * (1.0 + jnp.tanh(0.7978845608028654 * (x + 0.044715 * (x * x * x))))


def _rope(t, rc, rm, rp):
    return t * rc + pltpu.roll(t, HEAD_PAD - QK_ROPE // 2, 1) * rm + pltpu.roll(t, QK_ROPE // 2, 1) * rp


def _const_spec(shape):
    nd = len(shape)
    return pl.BlockSpec(shape, lambda *_: (0,) * nd, pipeline_mode=pl.Buffered(1))


def _mixer_pre_kernel(x_ref, nm_ref, win_ref, qn_ref, wq_ref, kvn_ref, wk_ref, wv_ref, sgn_ref,
                      ws_ref, bs_ref, cw_ref, wb_ref, wc_ref, rc_ref, rm_ref, rp_ref,
                      q_ref, k_ref, v_ref, part_ref, ga_ref,
                      halo_ref, ybin_ref, *, tiles_per_seq, tm):
    i = pl.program_id(0)
    d_model = x_ref.shape[1]
    h = _rms(x_ref[...], nm_ref[...]).astype(BF16)

    p1 = _dot(h, win_ref[:, 0:_P1])
    rc, rm, rp = rc_ref[...], rm_ref[...], rp_ref[...]
    qn = _rms(p1[:, :Q_RANK], qn_ref[...]).astype(BF16)
    qf = _dot(qn, wq_ref[...])
    scale = QK_HEAD ** -0.5
    for hd in range(MLA_HEADS):
        sl = slice(hd * HEAD_PAD, (hd + 1) * HEAD_PAD)
        q_ref[:, sl] = (_rope(qf[:, sl], rc, rm, rp) * scale).astype(BF16)
    kvn = _rms(p1[:, Q_RANK:Q_RANK + KV_RANK], kvn_ref[...]).astype(BF16)
    kr = _rope(p1[:, Q_RANK + KV_RANK:_P1], rc, rm, rp)
    kf = _dot(kvn, wk_ref[...])
    for hd in range(MLA_HEADS):
        sl = slice(hd * HEAD_PAD, (hd + 1) * HEAD_PAD)
        k_ref[:, sl] = (kf[:, sl] + kr).astype(BF16)
    v_ref[...] = _dot(kvn, wv_ref[...]).astype(BF16)

    p2 = _dot(h, win_ref[:, _P1:_P2])
    u = _gelu(p2[:, :SG_WIDTH])
    vb = _rms(_gelu(p2[:, SG_WIDTH:]), sgn_ref[...]).astype(BF16)
    r_i = lax.broadcasted_iota(jnp.int32, (CHUNK, CHUNK), 0)
    c_i = lax.broadcasted_iota(jnp.int32, (CHUNK, CHUNK), 1)
    for g in range(SG_GROUPS):
        wg = jnp.where(c_i <= r_i, ws_ref[g], 0.0).astype(BF16)
        gs = slice(g * SG_GROUP_DIM, (g + 1) * SG_GROUP_DIM)
        for c in range(tm // CHUNK):
            cs = slice(c * CHUNK, (c + 1) * CHUNK)
            mixed = _dot(wg, vb[cs, gs]) + bs_ref[g]
            ybin_ref[cs, gs] = (u[cs, gs] * mixed).astype(BF16)

    @pl.when(i % tiles_per_seq == 0)
    def _():
        halo_ref[...] = jnp.zeros_like(halo_ref)

    p3 = _dot(h, win_ref[:, _P2:_P3])
    z = p3[:, CONV_WIDTH:2 * CONV_WIDTH] * p3[:, 2 * CONV_WIDTH:]
    row = lax.broadcasted_iota(jnp.int32, z.shape, 0)
    halo = halo_ref[...]
    last1 = halo[7:8, :]
    last2 = halo[6:7, :]
    z1 = jnp.where(row == 0, last1, pltpu.roll(z, 1, 0))
    z2 = jnp.where(row == 0, last2, jnp.where(row == 1, last1, pltpu.roll(z, 2, 0)))
    y = cw_ref[0:1, :] * z2 + cw_ref[1:2, :] * z1 + cw_ref[2:3, :] * z
    yc_in = (p3[:, :CONV_WIDTH] * y).astype(BF16)
    halo_ref[...] = z[tm - 8:tm, :]

    p4 = _dot(h, win_ref[:, _P3:_P3 + 3 * d_model])
    yb = _dot(ybin_ref[...], wb_ref[...])
    yc = _dot(yc_in, wc_ref[...])
    part = (jax.nn.sigmoid(p4[:, d_model:2 * d_model]) * yb
            + jax.nn.sigmoid(p4[:, 2 * d_model:]) * yc)
    part_ref[...] = part.astype(BF16)
    ga_ref[...] = jax.nn.sigmoid(p4[:, :d_model]).astype(BF16)


def _mixer_pre(x2, lw, rc, rm, rp, *, seq, tm):
    t, d = x2.shape
    row = lambda w: pl.BlockSpec((tm, w), lambda i: (i, 0))
    consts = [lw["norm_mix"], lw["w_in"], lw["q_norm"], lw["w_q"], lw["kv_norm"], lw["w_k"], lw["w_v"],
              lw["sg_norm"], lw["w_s"], lw["b_s"], lw["conv_w"], lw["w_b_out"], lw["w_c_out"]]
    kern = functools.partial(_mixer_pre_kernel, tiles_per_seq=seq // tm, tm=tm)
    return pl.pallas_call(
        kern,
        grid=(t // tm,),
        in_specs=[row(d)] + [_const_spec(c.shape) for c in consts] + [row(HEAD_PAD)] * 3,
        out_specs=[row(MLA_HEADS * HEAD_PAD), row(MLA_HEADS * HEAD_PAD), row(MLA_HEADS * V_HEAD), row(d), row(d)],
        out_shape=[jax.ShapeDtypeStruct((t, MLA_HEADS * HEAD_PAD), BF16),
                   jax.ShapeDtypeStruct((t, MLA_HEADS * HEAD_PAD), BF16),
                   jax.ShapeDtypeStruct((t, MLA_HEADS * V_HEAD), BF16),
                   jax.ShapeDtypeStruct((t, d), BF16),
                   jax.ShapeDtypeStruct((t, d), BF16)],
        scratch_shapes=[pltpu.VMEM((8, CONV_WIDTH), F32), pltpu.VMEM((tm, SG_WIDTH), BF16)],
        compiler_params=pltpu.CompilerParams(dimension_semantics=("arbitrary",), vmem_limit_bytes=VMEM_LIMIT),
        name="mixer_pre",
    )(x2, *consts, rc, rm, rp)


def _attn_kernel(q_ref, k_ref, v_ref, o_ref, m_ref, l_ref, acc_ref, *, tq):
    qi = pl.program_id(2)
    m_ref[...] = jnp.full_like(m_ref, MASK_VALUE)
    l_ref[...] = jnp.zeros_like(l_ref)
    acc_ref[...] = jnp.zeros_like(acc_ref)

    def step(j, masked):
        start = pl.multiple_of(j * tq, tq)
        v_j = v_ref[pl.ds(start, tq), :]
        for sub in range(2):
            sl = slice(sub * HEAD_PAD, (sub + 1) * HEAD_PAD)
            k_j = k_ref[pl.ds(start, tq), sl]
            s = lax.dot_general(q_ref[:, sl], k_j, (((1,), (1,)), ((), ())), preferred_element_type=F32)
            if masked:
                r_i = lax.broadcasted_iota(jnp.int32, s.shape, 0)
                c_i = lax.broadcasted_iota(jnp.int32, s.shape, 1)
                s = jnp.where(c_i <= r_i, s, MASK_VALUE)
            m_prev = m_ref[sub]
            m_next = jnp.maximum(m_prev, jnp.max(s, axis=1, keepdims=True))
            alpha = jnp.exp(m_prev - m_next)
            p = jnp.exp(s - jnp.tile(m_next, (1, tq // LANES)))
            l_ref[sub] = alpha * l_ref[sub] + jnp.sum(p, axis=1, keepdims=True)
            m_ref[sub] = m_next
            acc_ref[sub] = alpha * acc_ref[sub] + _dot(p.astype(BF16), v_j)

    def body(j, carry):
        step(j, False)
        return carry

    lax.fori_loop(0, qi, body, 0)
    step(qi, True)
    lane = lax.broadcasted_iota(jnp.int32, (tq, LANES), 1)
    o0 = acc_ref[0] / l_ref[0]
    o1 = acc_ref[1] / l_ref[1]
    o_ref[...] = jnp.where(lane < V_HEAD, o0, o1).astype(BF16)


def _attention(q, k, v, *, batch, seq, tq):
    t = q.shape[0]
    nq = seq // tq
    pairs = MLA_HEADS // 2
    return pl.pallas_call(
        functools.partial(_attn_kernel, tq=tq),
        grid=(batch, pairs, nq),
        in_specs=[pl.BlockSpec((tq, 2 * HEAD_PAD), lambda b, hp, qi: (b * nq + qi, hp)),
                  pl.BlockSpec((seq, 2 * HEAD_PAD), lambda b, hp, qi: (b, hp)),
                  pl.BlockSpec((seq, 2 * V_HEAD), lambda b, hp, qi: (b, hp))],
        out_specs=pl.BlockSpec((tq, 2 * V_HEAD), lambda b, hp, qi: (b * nq + qi, hp)),
        out_shape=jax.ShapeDtypeStruct((t, MLA_HEADS * V_HEAD), BF16),
        scratch_shapes=[pltpu.VMEM((2, tq, LANES), F32), pltpu.VMEM((2, tq, LANES), F32),
                        pltpu.VMEM((2, tq, LANES), F32)],
        compiler_params=pltpu.CompilerParams(dimension_semantics=("arbitrary",) * 3, vmem_limit_bytes=VMEM_LIMIT),
        name="attention",
    )(q, k, v)


def _pack_pair(lo, hi):
    lo_b = lax.bitcast_convert_type(lo, jnp.uint32)
    hi_b = lax.bitcast_convert_type(hi, jnp.uint32)
    return (hi_b & jnp.uint32(0xFFFF0000)) | (lo_b >> 16)


def _unpack_pair(p):
    lo = lax.bitcast_convert_type(p << 16, F32)
    hi = lax.bitcast_convert_type(p & jnp.uint32(0xFFFF0000), F32)
    return lo, hi


def _route(logits):
    lane = lax.broadcasted_iota(jnp.int32, logits.shape, 1)
    lane_f = lane.astype(F32)
    big = float(LANES)
    is_grp = (lane >= GROUP_LANE0) & (lane < GROUP_LANE0 + N_GROUPS)
    g_log = jnp.where(is_grp, logits, MASK_VALUE)
    g_max = jnp.max(g_log, axis=1, keepdims=True)
    g_sum = jnp.sum(jnp.exp(g_log - g_max), axis=1, keepdims=True)
    g_p = 1.0 / g_sum
    g_idx = jnp.min(jnp.where(g_log == g_max, lane_f, big), axis=1, keepdims=True) - float(GROUP_LANE0)
    lo = g_idx * float(EXPERTS_PER_GROUP)
    in_grp = (lane_f >= lo) & (lane_f < lo + float(EXPERTS_PER_GROUP))
    e_log = jnp.where(in_grp, logits, MASK_VALUE)
    e_max = jnp.max(e_log, axis=1, keepdims=True)
    e_exp = jnp.exp(e_log - e_max)
    e_prob = e_exp / jnp.sum(e_exp, axis=1, keepdims=True)
    cand = jnp.where(in_grp, e_prob, -1.0)
    p1 = jnp.max(cand, axis=1, keepdims=True)
    i1 = jnp.min(jnp.where(cand == p1, lane_f, big), axis=1, keepdims=True)
    cand2 = jnp.where(lane_f == i1, -1.0, cand)
    p2 = jnp.max(cand2, axis=1, keepdims=True)
    i2 = jnp.min(jnp.where(cand2 == p2, lane_f, big), axis=1, keepdims=True)
    denom = p1 + p2
    return g_p * (p1 / denom), g_p * (p2 / denom), i1, i2


_L_W1, _L_W2 = 0, 1
_M_E1, _M_E2, _M_R1H, _M_R1L, _M_R2H, _M_R2L = range(6)


def _mixer_post_kernel(o_ref, part_ref, ga_ref, x_ref, wa_ref, wo_ref, nf_ref, wr_ref, br_ref,
                       x1_ref, h2p_ref, info_ref, meta_ref, cnt_ref, run_ref, *, tm):
    i = pl.program_id(0)
    half = x_ref.shape[1] // 2

    @pl.when(i == 0)
    def _():
        run_ref[...] = jnp.zeros_like(run_ref)

    ya = _dot(o_ref[...], wa_ref[...])
    merged = ga_ref[...].astype(F32) * ya + part_ref[...].astype(F32)
    x1 = x_ref[...] + _dot(merged.astype(BF16), wo_ref[...])
    x1_ref[...] = x1
    h2 = _rms(x1, nf_ref[...]).astype(BF16)
    h2f = h2.astype(F32)
    h2p_ref[...] = _pack_pair(h2f[:, :half], h2f[:, half:])
    w1, w2, i1, i2 = _route(_dot(h2, wr_ref[...]) + br_ref[...])

    lane_f = lax.broadcasted_iota(jnp.int32, (tm, LANES), 1).astype(F32)
    oh1 = lane_f == i1
    oh2 = lane_f == i2
    both = jnp.where(oh1, 1.0, 0.0) + jnp.where(oh2, 1.0, 0.0)
    r_i = lax.broadcasted_iota(jnp.int32, (tm, tm), 0)
    c_i = lax.broadcasted_iota(jnp.int32, (tm, tm), 1)
    earlier = jnp.where(c_i < r_i, 1.0, 0.0).astype(BF16)
    before = _dot(earlier, both.astype(BF16)) + run_ref[0:1, :]
    rank1 = jnp.sum(jnp.where(oh1, before, 0.0), axis=1, keepdims=True)
    rank2 = jnp.sum(jnp.where(oh2, before, 0.0), axis=1, keepdims=True)
    run_ref[0:1, :] = run_ref[0:1, :] + jnp.sum(both, axis=0, keepdims=True)
    cnt_ref[...] = run_ref[...]

    lane = lax.broadcasted_iota(jnp.int32, (tm, LANES), 1)
    info_ref[...] = jnp.where(lane == _L_W1, w1, jnp.where(lane == _L_W2, w2, 0.0))

    r1_hi = jnp.floor(rank1 * (1.0 / 256.0))
    r2_hi = jnp.floor(rank2 * (1.0 / 256.0))
    cols = (i1, i2, r1_hi, rank1 - 256.0 * r1_hi, r2_hi, rank2 - 256.0 * r2_hi)
    vals = jnp.zeros((tm, LANES), F32)
    for ln, val in enumerate(cols):
        vals = jnp.where(lane == ln, val, vals)
    s_r = lax.broadcasted_iota(jnp.int32, (8, LANES), 0)
    s_c = lax.broadcasted_iota(jnp.int32, (8, LANES), 1)
    sel = jnp.where((s_r == s_c) & (s_r < len(cols)), 1.0, 0.0).astype(BF16)
    meta_ref[0] = lax.dot_general(sel, vals.astype(BF16), (((1,), (1,)), ((), ())),
                                  preferred_element_type=F32)


def _mixer_post(o, part, ga, x2, lw, *, tm):
    t, d = x2.shape
    row = lambda w: pl.BlockSpec((tm, w), lambda i: (i, 0))
    consts = [lw["w_a_out"], lw["w_o"], lw["norm_ffn"], lw["w_router"], lw["b_router"]]
    return pl.pallas_call(
        functools.partial(_mixer_post_kernel, tm=tm),
        grid=(t // tm,),
        in_specs=[row(o.shape[1]), row(d), row(d), row(d)] + [_const_spec(c.shape) for c in consts],
        out_specs=[row(d), row(d // 2), row(LANES), pl.BlockSpec((1, 8, tm), lambda i: (i, 0, 0)),
                   pl.BlockSpec((8, LANES), lambda i: (0, 0))],
        out_shape=[jax.ShapeDtypeStruct((t, d), F32), jax.ShapeDtypeStruct((t, d // 2), jnp.uint32),
                   jax.ShapeDtypeStruct((t, LANES), F32), jax.ShapeDtypeStruct((t // tm, 8, tm), F32),
                   jax.ShapeDtypeStruct((8, LANES), F32)],
        scratch_shapes=[pltpu.VMEM((8, LANES), F32)],
        compiler_params=pltpu.CompilerParams(dimension_semantics=("arbitrary",), vmem_limit_bytes=VMEM_LIMIT),
        name="mixer_post",
    )(o, part, ga, x2, *consts)


def _dispatch_kernel(slots_ref, h2p_hbm, xs_in_hbm, xs_hbm, buf, lsem, rsem, *, td):
    del xs_in_hbm
    i = pl.program_id(0)
    n = pl.num_programs(0)

    def load(j, b):
        return pltpu.make_async_copy(h2p_hbm.at[pl.ds(j * td, td)], buf.at[b], lsem.at[b])

    @pl.when(i == 0)
    def _():
        load(0, 0).start()

    @pl.when(i + 1 < n)
    def _():
        load(i + 1, (i + 1) % 3).start()

    b = i % 3
    par = i % 2
    load(i, b).wait()
    for r in range(td):
        for k in range(2):
            dst = slots_ref[0, 0, k * td + r]
            pltpu.make_async_copy(buf.at[b, pl.ds(r, 1)], xs_hbm.at[pl.ds(dst, 1)], rsem.at[par]).start()

    def wait_rows(p):
        for _ in range(2):
            pltpu.make_async_copy(buf.at[0], xs_hbm.at[pl.ds(0, td)], rsem.at[p]).wait()

    @pl.when(i > 0)
    def _():
        wait_rows(1 - par)

    @pl.when(i == n - 1)
    def _():
        wait_rows(par)


def _dispatch(h2p, slots, n_slots, *, td):
    t, half = h2p.shape
    xs0 = jnp.zeros((n_slots, half), jnp.uint32)
    return pl.pallas_call(
        functools.partial(_dispatch_kernel, td=td),
        grid=(t // td,),
        in_specs=[pl.BlockSpec((1, 1, 2 * td), lambda i: (i, 0, 0), memory_space=pltpu.SMEM),
                  pl.BlockSpec(memory_space=pl.ANY),
                  pl.BlockSpec(memory_space=pl.ANY)],
        out_specs=pl.BlockSpec(memory_space=pl.ANY),
        out_shape=jax.ShapeDtypeStruct(xs0.shape, jnp.uint32),
        scratch_shapes=[pltpu.VMEM((3, td, half), jnp.uint32), pltpu.SemaphoreType.DMA((3,)),
                        pltpu.SemaphoreType.DMA((2,))],
        input_output_aliases={2: 0},
        compiler_params=pltpu.CompilerParams(dimension_semantics=("arbitrary",), vmem_limit_bytes=VMEM_LIMIT,
                                             has_side_effects=True),
        name="dispatch",
    )(slots, h2p, xs0)


def _expert_kernel(te_ref, nu_ref, xs_ref, wgu_ref, wdn_ref, ys_ref):
    i = pl.program_id(0)
    half = wgu_ref.shape[1] // 2

    @pl.when(i < nu_ref[0])
    def _():
        lo, hi = _unpack_pair(xs_ref[...])
        gu = _dot(lo.astype(BF16), wgu_ref[0, :half, :]) + _dot(hi.astype(BF16), wgu_ref[0, half:, :])
        act = (jax.nn.silu(gu[:, :D_EXPERT]) * gu[:, D_EXPERT:]).astype(BF16)
        y = _dot(act, wdn_ref[0]).astype(BF16).astype(F32)
        ys_ref[...] = _pack_pair(y[:, :half], y[:, half:])

    @pl.when(i >= nu_ref[0])
    def _():
        ys_ref[...] = jnp.zeros_like(ys_ref)


def _experts(xs, tile_expert, n_used, lw, *, tme):
    n_slots, half = xs.shape
    d = 2 * half
    grid_spec = pltpu.PrefetchScalarGridSpec(
        num_scalar_prefetch=2,
        grid=(n_slots // tme,),
        in_specs=[pl.BlockSpec((tme, half), lambda i, te, nu: (i, 0)),
                  pl.BlockSpec((1, d, 2 * D_EXPERT), lambda i, te, nu: (te[i], 0, 0)),
                  pl.BlockSpec((1, D_EXPERT, d), lambda i, te, nu: (te[i], 0, 0))],
        out_specs=pl.BlockSpec((tme, half), lambda i, te, nu: (i, 0)),
    )
    return pl.pallas_call(
        _expert_kernel,
        grid_spec=grid_spec,
        out_shape=jax.ShapeDtypeStruct((n_slots, half), jnp.uint32),
        compiler_params=pltpu.CompilerParams(dimension_semantics=("arbitrary",), vmem_limit_bytes=VMEM_LIMIT),
        name="experts",
    )(tile_expert, n_used, xs, lw["w_gate_up"], lw["w_down"])


def _combine_kernel(s_cur_ref, s_nxt_ref, x1_ref, info_ref, ys_hbm, fn_ref, out_ref, ybuf, sem,
                    *, tmc, final_norm):
    i = pl.program_id(0)
    n = pl.num_programs(0)
    half = x1_ref.shape[1] // 2

    def gather(idx_ref, slot):
        for r in range(2 * tmc):
            src = idx_ref[0, 0, r]
            pltpu.make_async_copy(ys_hbm.at[pl.ds(src, 1)], ybuf.at[slot, pl.ds(r, 1)], sem.at[slot]).start()

    @pl.when(i == 0)
    def _():
        gather(s_cur_ref, 0)

    @pl.when(i + 1 < n)
    def _():
        gather(s_nxt_ref, (i + 1) % 2)

    slot = i % 2
    pltpu.make_async_copy(ys_hbm.at[pl.ds(0, 2 * tmc)], ybuf.at[slot], sem.at[slot]).wait()
    lo1, hi1 = _unpack_pair(ybuf[slot, 0:tmc, :])
    lo2, hi2 = _unpack_pair(ybuf[slot, tmc:2 * tmc, :])
    info = info_ref[...]
    w1 = info[:, _L_W1:_L_W1 + 1]
    w2 = info[:, _L_W2:_L_W2 + 1]
    lo = x1_ref[:, :half] + w1 * lo1 + w2 * lo2
    hi = x1_ref[:, half:] + w1 * hi1 + w2 * hi2
    if final_norm:
        ms = (jnp.sum(lo * lo, axis=-1, keepdims=True) + jnp.sum(hi * hi, axis=-1, keepdims=True)) / (2 * half)
        inv = lax.rsqrt(ms + EPS)
        lo = lo * inv * fn_ref[:, :half]
        hi = hi * inv * fn_ref[:, half:]
    out_ref[:, :half] = lo
    out_ref[:, half:] = hi


def _combine(x1, info, ys, slots, fn, *, tmc, final_norm):
    t, d = x1.shape
    n = t // tmc
    return pl.pallas_call(
        functools.partial(_combine_kernel, tmc=tmc, final_norm=final_norm),
        grid=(n,),
        in_specs=[pl.BlockSpec((1, 1, 2 * tmc), lambda i: (i, 0, 0), memory_space=pltpu.SMEM),
                  pl.BlockSpec((1, 1, 2 * tmc), lambda i: (jnp.minimum(i + 1, n - 1), 0, 0),
                               memory_space=pltpu.SMEM),
                  pl.BlockSpec((tmc, d), lambda i: (i, 0)),
                  pl.BlockSpec((tmc, LANES), lambda i: (i, 0)),
                  pl.BlockSpec(memory_space=pl.ANY),
                  pl.BlockSpec((1, d), lambda i: (0, 0))],
        out_specs=pl.BlockSpec((tmc, d), lambda i: (i, 0)),
        out_shape=jax.ShapeDtypeStruct((t, d), F32),
        scratch_shapes=[pltpu.VMEM((2, 2 * tmc, d // 2), jnp.uint32), pltpu.SemaphoreType.DMA((2,))],
        compiler_params=pltpu.CompilerParams(dimension_semantics=("arbitrary",), vmem_limit_bytes=VMEM_LIMIT),
        name="combine",
    )(slots, slots, x1, info, ys, fn)


def _moe(x1, h2p, info, meta, cnt, lw, fn, *, tme, tmc, final_norm):
    t = x1.shape[0]
    n_tiles = (2 * t) // tme + N_EXPERTS
    counts = cnt[0, :N_EXPERTS].astype(jnp.int32)
    padded = ((counts + tme - 1) // tme) * tme
    ends = jnp.cumsum(padded)
    starts = ends - padded
    mi = meta.astype(jnp.int32)
    expert_ids = jnp.arange(N_EXPERTS, dtype=jnp.int32)

    def slot_of(e, r_hi, r_lo):
        start = jnp.sum(jnp.where(e[..., None] == expert_ids, starts, 0), axis=-1)
        return (start + r_hi * 256 + r_lo).reshape(t // tmc, 1, tmc)

    slot1 = slot_of(mi[:, _M_E1], mi[:, _M_R1H], mi[:, _M_R1L])
    slot2 = slot_of(mi[:, _M_E2], mi[:, _M_R2H], mi[:, _M_R2L])
    slots = jnp.concatenate([slot1, slot2], axis=2)
    tile_start = jnp.arange(n_tiles, dtype=jnp.int32) * tme
    tile_expert = jnp.minimum(jnp.sum((ends[None, :] <= tile_start[:, None]).astype(jnp.int32), axis=1),
                              N_EXPERTS - 1)
    n_used = ends[-1:] // tme
    xs = _dispatch(h2p, slots, n_tiles * tme, td=tmc)
    ys = _experts(xs, tile_expert, n_used, lw, tme=tme)
    return _combine(x1, info, ys, slots, fn, tmc=tmc, final_norm=final_norm)


def _pack_layer(l, w_in, norm_mix, q_norm, w_q_up, kv_norm, w_kv_up, sg_norm, w_s, b_s, conv_w,
                w_a_out, w_b_out, w_c_out, w_o, norm_ffn, w_group_router, b_group_router,
                w_expert_router, b_expert_router, w_gate_up, w_down):
    d = w_in.shape[1]
    wi = w_in[l]
    o_kr = Q_RANK + KV_RANK
    kr_slot = jnp.zeros((d, HEAD_PAD), F32).at[:, QK_NOPE:QK_HEAD].set(wi[:, o_kr:o_kr + QK_ROPE])
    w_in_p = jnp.concatenate([wi[:, :o_kr], kr_slot, wi[:, o_kr + QK_ROPE:]], axis=1).astype(BF16)
    wq = w_q_up[l].reshape(Q_RANK, MLA_HEADS, QK_HEAD)
    wq = jnp.pad(wq, ((0, 0), (0, 0), (0, HEAD_PAD - QK_HEAD))).reshape(Q_RANK, MLA_HEADS * HEAD_PAD)
    wkv = w_kv_up[l].reshape(KV_RANK, MLA_HEADS, QK_NOPE + V_HEAD)
    wk = jnp.pad(wkv[:, :, :QK_NOPE], ((0, 0), (0, 0), (0, HEAD_PAD - QK_NOPE)))
    wk = wk.reshape(KV_RANK, MLA_HEADS * HEAD_PAD)
    wv = wkv[:, :, QK_NOPE:].reshape(KV_RANK, MLA_HEADS * V_HEAD)
    w_r = jnp.zeros((d, LANES), F32)
    w_r = w_r.at[:, :N_EXPERTS].set(w_expert_router[l])
    w_r = w_r.at[:, GROUP_LANE0:GROUP_LANE0 + N_GROUPS].set(w_group_router[l])
    b_r = jnp.zeros((1, LANES), F32)
    b_r = b_r.at[0, :N_EXPERTS].set(b_expert_router[l])
    b_r = b_r.at[0, GROUP_LANE0:GROUP_LANE0 + N_GROUPS].set(b_group_router[l])
    return {
        "norm_mix": norm_mix[l][None, :], "w_in": w_in_p,
        "q_norm": q_norm[l][None, :], "w_q": wq.astype(BF16),
        "kv_norm": kv_norm[l][None, :], "w_k": wk.astype(BF16), "w_v": wv.astype(BF16),
        "sg_norm": sg_norm[l][None, :], "w_s": w_s[l],
        "b_s": jnp.broadcast_to(b_s[l][:, :, None], (SG_GROUPS, CHUNK, SG_GROUP_DIM)),
        "conv_w": conv_w[l],
        "w_a_out": w_a_out[l].astype(BF16), "w_b_out": w_b_out[l].astype(BF16),
        "w_c_out": w_c_out[l].astype(BF16), "w_o": w_o[l].astype(BF16),
        "norm_ffn": norm_ffn[l][None, :], "w_router": w_r.astype(BF16), "b_router": b_r,
        "w_gate_up": w_gate_up[l].astype(BF16), "w_down": w_down[l].astype(BF16),
    }


def _rope_tables(positions):
    half = QK_ROPE // 2
    inv = ROPE_BASE ** (-jnp.arange(0, QK_ROPE, 2, dtype=F32) / QK_ROPE)
    ang = positions.astype(F32).reshape(-1)[:, None] * inv
    cos, sin = jnp.cos(ang), jnp.sin(ang)
    t = ang.shape[0]
    ones = jnp.ones((t, QK_NOPE), F32)
    z16 = jnp.zeros((t, half), F32)
    z32 = jnp.zeros((t, HEAD_PAD - QK_HEAD), F32)
    z64 = jnp.zeros((t, QK_NOPE), F32)
    rc = jnp.concatenate([ones, cos, cos, z32], axis=1)
    rm = jnp.concatenate([z64, -sin, z16, z32], axis=1)
    rp = jnp.concatenate([z64, z16, sin, z32], axis=1)
    return rc, rm, rp


def kernel(x, positions, norm_mix, w_in, q_norm, w_q_up, kv_norm, w_kv_up, sg_norm, w_s, b_s, conv_w,
           w_a_out, w_b_out, w_c_out, w_o, norm_ffn, w_group_router, b_group_router, w_expert_router,
           b_expert_router, w_gate_up, w_down, final_norm):
    batch, seq, d = x.shape
    depth = w_in.shape[0]
    tm = min(512, seq)
    tq = min(512, seq)
    tme = 256
    tmc = min(256, seq)
    rc, rm, rp = _rope_tables(positions)
    x2 = x.reshape(batch * seq, d)
    fn = final_norm[None, :]
    for l in range(depth):
        lw = _pack_layer(l, w_in, norm_mix, q_norm, w_q_up, kv_norm, w_kv_up, sg_norm, w_s, b_s, conv_w,
                         w_a_out, w_b_out, w_c_out, w_o, norm_ffn, w_group_router, b_group_router,
                         w_expert_router, b_expert_router, w_gate_up, w_down)
        q, k, v, part, ga = _mixer_pre(x2, lw, rc, rm, rp, seq=seq, tm=tm)
        o = _attention(q, k, v, batch=batch, seq=seq, tq=tq)
        x1, h2p, info, meta, cnt = _mixer_post(o, part, ga, x2, lw, tm=tm)
        x2 = _moe(x1, h2p, info, meta, cnt, lw, fn, tme=tme, tmc=tmc, final_norm=(l == depth - 1))
    return x2.reshape(batch, seq, d)
```

```python
import functools

import jax
import jax.numpy as jnp
from jax import lax
from jax.experimental import pallas as pl
from jax.experimental.pallas import tpu as pltpu

F32 = jnp.float32
BF16 = jnp.bfloat16

EPS = 1e-6
MLA_HEADS = 8
QK_NOPE = 64
QK_ROPE = 32
QK_HEAD = QK_NOPE + QK_ROPE
V_HEAD = 64
Q_RANK = 384
KV_RANK = 256
ROPE_BASE = 10000.0
SG_GROUPS = 4
SG_GROUP_DIM = 128
SG_WIDTH = SG_GROUPS * SG_GROUP_DIM
CHUNK = 128
CONV_WIDTH = 512
CONV_K = 3
N_GROUPS = 4
EXPERTS_PER_GROUP = 8
N_EXPERTS = N_GROUPS * EXPERTS_PER_GROUP
D_EXPERT = 256

LANES = 128
HEAD_PAD = 128
MASK_VALUE = -1e30
LOG2_E = 1.4426950408889634
VMEM_LIMIT = 56 * 1024 * 1024

_P1 = Q_RANK + KV_RANK + HEAD_PAD
_P2 = _P1 + 2 * SG_WIDTH
_P3 = _P2 + 3 * CONV_WIDTH
GROUP_LANE0 = N_EXPERTS


def _dot(a, b):
    return jnp.dot(a, b, preferred_element_type=F32)


def _rms(xf, g):
    return xf * lax.rsqrt(jnp.mean(xf * xf, axis=-1, keepdims=True) + EPS) * g


def _gelu(x):
    return 0.5 * x * (1.0 + jnp.tanh(0.7978845608028654 * (x + 0.044715 * (x * x * x))))


def _rope(t, rc, rm, rp):
    return t * rc + pltpu.roll(t, HEAD_PAD - QK_ROPE // 2, 1) * rm + pltpu.roll(t, QK_ROPE // 2, 1) * rp


def _const_spec(shape):
    nd = len(shape)
    return pl.BlockSpec(shape, lambda *_: (0,) * nd, pipeline_mode=pl.Buffered(1))


def _mixer_pre_kernel(x_ref, nm_ref, win_ref, qn_ref, wq_ref, kvn_ref, wk_ref, wv_ref, sgn_ref,
                      ws_ref, bs_ref, cw_ref, wb_ref, wc_ref, rc_ref, rm_ref, rp_ref,
                      qt_ref, k_ref, vt_ref, part_ref, ga_ref,
                      halo_ref, ybin_ref, *, tiles_per_seq, tm):
    i = pl.program_id(0)
    d_model = x_ref.shape[1]
    h = _rms(x_ref[...], nm_ref[...]).astype(BF16)

    p1 = _dot(h, win_ref[:, 0:_P1])
    rc, rm, rp = rc_ref[...], rm_ref[...], rp_ref[...]
    qn = _rms(p1[:, :Q_RANK], qn_ref[...]).astype(BF16)
    qf = _dot(qn, wq_ref[...])
    scale = QK_HEAD ** -0.5 * LOG2_E
    for hd in range(MLA_HEADS):
        sl = slice(hd * HEAD_PAD, (hd + 1) * HEAD_PAD)
        qt_ref[0, 0, sl, :] = (_rope(qf[:, sl], rc, rm, rp) * scale).T.astype(BF16)
    kvn = _rms(p1[:, Q_RANK:Q_RANK + KV_RANK], kvn_ref[...]).astype(BF16)
    kr = _rope(p1[:, Q_RANK + KV_RANK:_P1], rc, rm, rp)
    kf = _dot(kvn, wk_ref[...])
    for hd in range(MLA_HEADS):
        sl = slice(hd * HEAD_PAD, (hd + 1) * HEAD_PAD)
        k_ref[:, sl] = (kf[:, sl] + kr).astype(BF16)
    vt_ref[0, 0] = _dot(kvn, wv_ref[...]).T.astype(BF16)

    p2 = _dot(h, win_ref[:, _P1:_P2])
    u = _gelu(p2[:, :SG_WIDTH])
    vb = _rms(_gelu(p2[:, SG_WIDTH:]), sgn_ref[...]).astype(BF16)
    r_i = lax.broadcasted_iota(jnp.int32, (CHUNK, CHUNK), 0)
    c_i = lax.broadcasted_iota(jnp.int32, (CHUNK, CHUNK), 1)
    for g in range(SG_GROUPS):
        wg = jnp.where(c_i <= r_i, ws_ref[g], 0.0).astype(BF16)
        gs = slice(g * SG_GROUP_DIM, (g + 1) * SG_GROUP_DIM)
        for c in range(tm // CHUNK):
            cs = slice(c * CHUNK, (c + 1) * CHUNK)
            mixed = _dot(wg, vb[cs, gs]) + bs_ref[g]
            ybin_ref[cs, gs] = (u[cs, gs] * mixed).astype(BF16)

    @pl.when(i % tiles_per_seq == 0)
    def _():
        halo_ref[...] = jnp.zeros_like(halo_ref)

    p3 = _dot(h, win_ref[:, _P2:_P3])
    z = p3[:, CONV_WIDTH:2 * CONV_WIDTH] * p3[:, 2 * CONV_WIDTH:]
    row = lax.broadcasted_iota(jnp.int32, z.shape, 0)
    halo = halo_ref[...]
    last1 = halo[7:8, :]
    last2 = halo[6:7, :]
    z1 = jnp.where(row == 0, last1, pltpu.roll(z, 1, 0))
    z2 = jnp.where(row == 0, last2, jnp.where(row == 1, last1, pltpu.roll(z, 2, 0)))
    y = cw_ref[0:1, :] * z2 + cw_ref[1:2, :] * z1 + cw_ref[2:3, :] * z
    yc_in = (p3[:, :CONV_WIDTH] * y).astype(BF16)
    halo_ref[...] = z[tm - 8:tm, :]

    p4 = _dot(h, win_ref[:, _P3:_P3 + 3 * d_model])
    yb = _dot(ybin_ref[...], wb_ref[...])
    yc = _dot(yc_in, wc_ref[...])
    part = (jax.nn.sigmoid(p4[:, d_model:2 * d_model]) * yb
            + jax.nn.sigmoid(p4[:, 2 * d_model:]) * yc)
    part_ref[...] = part.astype(BF16)
    ga_ref[...] = jax.nn.sigmoid(p4[:, :d_model]).astype(BF16)


def _mixer_pre(x2, lw, rc, rm, rp, *, seq, tm):
    t, d = x2.shape
    row = lambda w: pl.BlockSpec((tm, w), lambda i: (i, 0))
    consts = [lw["norm_mix"], lw["w_in"], lw["q_norm"], lw["w_q"], lw["kv_norm"], lw["w_k"], lw["w_v"],
              lw["sg_norm"], lw["w_s"], lw["b_s"], lw["conv_w"], lw["w_b_out"], lw["w_c_out"]]
    tps = seq // tm
    kern = functools.partial(_mixer_pre_kernel, tiles_per_seq=tps, tm=tm)
    slab = lambda w: pl.BlockSpec((1, 1, w, tm), lambda i: (i // tps, i % tps, 0, 0))
    return pl.pallas_call(
        kern,
        grid=(t // tm,),
        in_specs=[row(d)] + [_const_spec(c.shape) for c in consts] + [row(HEAD_PAD)] * 3,
        out_specs=[slab(MLA_HEADS * HEAD_PAD), row(MLA_HEADS * HEAD_PAD), slab(MLA_HEADS * V_HEAD), row(d), row(d)],
        out_shape=[jax.ShapeDtypeStruct((t // seq, tps, MLA_HEADS * HEAD_PAD, tm), BF16),
                   jax.ShapeDtypeStruct((t, MLA_HEADS * HEAD_PAD), BF16),
                   jax.ShapeDtypeStruct((t // seq, tps, MLA_HEADS * V_HEAD, tm), BF16),
                   jax.ShapeDtypeStruct((t, d), BF16),
                   jax.ShapeDtypeStruct((t, d), BF16)],
        scratch_shapes=[pltpu.VMEM((8, CONV_WIDTH), F32), pltpu.VMEM((tm, SG_WIDTH), BF16)],
        compiler_params=pltpu.CompilerParams(dimension_semantics=("arbitrary",), vmem_limit_bytes=VMEM_LIMIT),
        name="mixer_pre",
    )(x2, *consts, rc, rm, rp)


ACC_ROWS = V_HEAD + 16


def _attn_kernel(qt_ref, k_ref, vt_ref, o_ref, m_ref, acc_ref, *, tq):
    qi = pl.program_id(2)
    m_ref[...] = jnp.full_like(m_ref, MASK_VALUE)
    acc_ref[...] = jnp.zeros_like(acc_ref)

    def scores(sub, j):
        start = pl.multiple_of(j * tq, tq)
        k_j = k_ref[pl.ds(start, tq), sub * HEAD_PAD:(sub + 1) * HEAD_PAD]
        return _dot(k_j, qt_ref[0, 0, sub * HEAD_PAD:(sub + 1) * HEAD_PAD, :])

    def update(sub, j, st, masked):
        if masked:
            kv_i = lax.broadcasted_iota(jnp.int32, st.shape, 0)
            q_i = lax.broadcasted_iota(jnp.int32, st.shape, 1)
            st = jnp.where(kv_i <= q_i, st, MASK_VALUE)
        m_prev = m_ref[sub]
        m_next = jnp.maximum(m_prev, jnp.max(st, axis=0, keepdims=True))
        alpha = jnp.exp2(m_prev - m_next)
        pt = jnp.exp2((st - m_next[0:1, :]).astype(BF16))
        m_ref[sub] = m_next
        vt_j = jnp.concatenate([vt_ref[0, j, sub * V_HEAD:(sub + 1) * V_HEAD, :],
                                jnp.ones((ACC_ROWS - V_HEAD, tq), BF16)], axis=0)
        acc_ref[sub] = alpha[0:1, :] * acc_ref[sub] + _dot(vt_j, pt)

    def tiles(js, last_masked):
        work = [(sub, j, last_masked and j is js[-1]) for j in js for sub in range(2)]
        st = scores(*work[0][:2])
        for n, (sub, j, masked) in enumerate(work):
            st_next = scores(*work[n + 1][:2]) if n + 1 < len(work) else None
            update(sub, j, st, masked)
            st = st_next

    def body(p, carry):
        tiles((2 * p, 2 * p + 1), False)
        return carry

    lax.fori_loop(0, qi // 2, body, 0)

    @pl.when(qi % 2 == 1)
    def _():
        tiles((qi - 1, qi), True)

    @pl.when(qi % 2 == 0)
    def _():
        tiles((qi,), True)

    ot = jnp.concatenate([acc_ref[sub, 0:V_HEAD, :] / acc_ref[sub, V_HEAD:V_HEAD + 1, :] for sub in range(2)],
                         axis=0)
    o_ref[...] = ot.T.astype(BF16)


def _attention(qt, k, vt, *, batch, seq, tq):
    t = k.shape[0]
    nq = seq // tq
    pairs = MLA_HEADS // 2
    return pl.pallas_call(
        functools.partial(_attn_kernel, tq=tq),
        grid=(batch, pairs, nq),
        in_specs=[pl.BlockSpec((1, 1, 2 * HEAD_PAD, tq), lambda b, hp, qi: (b, qi, hp, 0)),
                  pl.BlockSpec((seq, 2 * HEAD_PAD), lambda b, hp, qi: (b, hp)),
                  pl.BlockSpec((1, nq, 2 * V_HEAD, tq), lambda b, hp, qi: (b, 0, hp, 0))],
        out_specs=pl.BlockSpec((tq, 2 * V_HEAD), lambda b, hp, qi: (b * nq + qi, hp)),
        out_shape=jax.ShapeDtypeStruct((t, MLA_HEADS * V_HEAD), BF16),
        scratch_shapes=[pltpu.VMEM((2, 8, tq), F32), pltpu.VMEM((2, ACC_ROWS, tq), F32)],
        compiler_params=pltpu.CompilerParams(dimension_semantics=("arbitrary",) * 3, vmem_limit_bytes=VMEM_LIMIT),
        name="attention",
    )(qt, k, vt)


def _pack_pair(lo, hi):
    lo_b = lax.bitcast_convert_type(lo, jnp.uint32)
    hi_b = lax.bitcast_convert_type(hi, jnp.uint32)
    return (hi_b & jnp.uint32(0xFFFF0000)) | (lo_b >> 16)


def _unpack_pair(p):
    lo = lax.bitcast_convert_type(p << 16, F32)
    hi = lax.bitcast_convert_type(p & jnp.uint32(0xFFFF0000), F32)
    return lo, hi


def _route(logits):
    lane = lax.broadcasted_iota(jnp.int32, logits.shape, 1)
    lane_f = lane.astype(F32)
    big = float(LANES)
    is_grp = (lane >= GROUP_LANE0) & (lane < GROUP_LANE0 + N_GROUPS)
    g_log = jnp.where(is_grp, logits, MASK_VALUE)
    g_max = jnp.max(g_log, axis=1, keepdims=True)
    g_sum = jnp.sum(jnp.exp(g_log - g_max), axis=1, keepdims=True)
    g_p = 1.0 / g_sum
    g_idx = jnp.min(jnp.where(g_log == g_max, lane_f, big), axis=1, keepdims=True) - float(GROUP_LANE0)
    lo = g_idx * float(EXPERTS_PER_GROUP)
    in_grp = (lane_f >= lo) & (lane_f < lo + float(EXPERTS_PER_GROUP))
    e_log = jnp.where(in_grp, logits, MASK_VALUE)
    e_max = jnp.max(e_log, axis=1, keepdims=True)
    e_exp = jnp.exp(e_log - e_max)
    e_prob = e_exp / jnp.sum(e_exp, axis=1, keepdims=True)
    cand = jnp.where(in_grp, e_prob, -1.0)
    p1 = jnp.max(cand, axis=1, keepdims=True)
    i1 = jnp.min(jnp.where(cand == p1, lane_f, big), axis=1, keepdims=True)
    cand2 = jnp.where(lane_f == i1, -1.0, cand)
    p2 = jnp.max(cand2, axis=1, keepdims=True)
    i2 = jnp.min(jnp.where(cand2 == p2, lane_f, big), axis=1, keepdims=True)
    denom = p1 + p2
    return g_p * (p1 / denom), g_p * (p2 / denom), i1, i2


_L_W1, _L_W2 = 0, 1
_M_E1, _M_E2, _M_R1H, _M_R1L, _M_R2H, _M_R2L = range(6)


def _mixer_post_kernel(o_ref, part_ref, ga_ref, x_ref, wa_ref, wo_ref, nf_ref, wr_ref, br_ref,
                       x1_ref, h2p_ref, info_ref, meta_ref, cnt_ref, run_ref, *, tm):
    i = pl.program_id(0)
    half = x_ref.shape[1] // 2

    @pl.when(i == 0)
    def _():
        run_ref[...] = jnp.zeros_like(run_ref)

    ya = _dot(o_ref[...], wa_ref[...])
    merged = ga_ref[...].astype(F32) * ya + part_ref[...].astype(F32)
    x1 = x_ref[...] + _dot(merged.astype(BF16), wo_ref[...])
    x1_ref[...] = x1
    h2 = _rms(x1, nf_ref[...]).astype(BF16)
    h2f = h2.astype(F32)
    h2p_ref[...] = _pack_pair(h2f[:, :half], h2f[:, half:])
    w1, w2, i1, i2 = _route(_dot(h2, wr_ref[...]) + br_ref[...])

    lane_f = lax.broadcasted_iota(jnp.int32, (tm, LANES), 1).astype(F32)
    oh1 = lane_f == i1
    oh2 = lane_f == i2
    both = jnp.where(oh1, 1.0, 0.0) + jnp.where(oh2, 1.0, 0.0)
    r_i = lax.broadcasted_iota(jnp.int32, (tm, tm), 0)
    c_i = lax.broadcasted_iota(jnp.int32, (tm, tm), 1)
    earlier = jnp.where(c_i < r_i, 1.0, 0.0).astype(BF16)
    before = _dot(earlier, both.astype(BF16)) + run_ref[0:1, :]
    rank1 = jnp.sum(jnp.where(oh1, before, 0.0), axis=1, keepdims=True)
    rank2 = jnp.sum(jnp.where(oh2, before, 0.0), axis=1, keepdims=True)
    run_ref[0:1, :] = run_ref[0:1, :] + jnp.sum(both, axis=0, keepdims=True)
    cnt_ref[...] = run_ref[...]

    lane = lax.broadcasted_iota(jnp.int32, (tm, LANES), 1)
    info_ref[...] = jnp.where(lane == _L_W1, w1, jnp.where(lane == _L_W2, w2, 0.0))

    r1_hi = jnp.floor(rank1 * (1.0 / 256.0))
    r2_hi = jnp.floor(rank2 * (1.0 / 256.0))
    cols = (i1, i2, r1_hi, rank1 - 256.0 * r1_hi, r2_hi, rank2 - 256.0 * r2_hi)
    vals = jnp.zeros((tm, LANES), F32)
    for ln, val in enumerate(cols):
        vals = jnp.where(lane == ln, val, vals)
    s_r = lax.broadcasted_iota(jnp.int32, (8, LANES), 0)
    s_c = lax.broadcasted_iota(jnp.int32, (8, LANES), 1)
    sel = jnp.where((s_r == s_c) & (s_r < len(cols)), 1.0, 0.0).astype(BF16)
    meta_ref[0] = lax.dot_general(sel, vals.astype(BF16), (((1,), (1,)), ((), ())),
                                  preferred_element_type=F32)


def _mixer_post(o, part, ga, x2, lw, *, tm):
    t, d = x2.shape
    row = lambda w: pl.BlockSpec((tm, w), lambda i: (i, 0))
    consts = [lw["w_a_out"], lw["w_o"], lw["norm_ffn"], lw["w_router"], lw["b_router"]]
    return pl.pallas_call(
        functools.partial(_mixer_post_kernel, tm=tm),
        grid=(t // tm,),
        in_specs=[row(o.shape[1]), row(d), row(d), row(d)] + [_const_spec(c.shape) for c in consts],
        out_specs=[row(d), row(d // 2), row(LANES), pl.BlockSpec((1, 8, tm), lambda i: (i, 0, 0)),
                   pl.BlockSpec((8, LANES), lambda i: (0, 0))],
        out_shape=[jax.ShapeDtypeStruct((t, d), F32), jax.ShapeDtypeStruct((t, d // 2), jnp.uint32),
                   jax.ShapeDtypeStruct((t, LANES), F32), jax.ShapeDtypeStruct((t // tm, 8, tm), F32),
                   jax.ShapeDtypeStruct((8, LANES), F32)],
        scratch_shapes=[pltpu.VMEM((8, LANES), F32)],
        compiler_params=pltpu.CompilerParams(dimension_semantics=("arbitrary",), vmem_limit_bytes=VMEM_LIMIT),
        name="mixer_post",
    )(o, part, ga, x2, *consts)


def _dispatch_kernel(slots_ref, h2p_hbm, xs_in_hbm, xs_hbm, buf, lsem, rsem, *, td):
    del xs_in_hbm
    i = pl.program_id(0)
    n = pl.num_programs(0)

    def load(j, b):
        return pltpu.make_async_copy(h2p_hbm.at[pl.ds(j * td, td)], buf.at[b], lsem.at[b])

    @pl.when(i == 0)
    def _():
        load(0, 0).start()

    @pl.when(i + 1 < n)
    def _():
        load(i + 1, (i + 1) % 3).start()

    b = i % 3
    par = i % 2
    load(i, b).wait()
    for r in range(td):
        for k in range(2):
            dst = slots_ref[0, 0, k * td + r]
            pltpu.make_async_copy(buf.at[b, pl.ds(r, 1)], xs_hbm.at[pl.ds(dst, 1)], rsem.at[par]).start()

    def wait_rows(p):
        for _ in range(2):
            pltpu.make_async_copy(buf.at[0], xs_hbm.at[pl.ds(0, td)], rsem.at[p]).wait()

    @pl.when(i > 0)
    def _():
        wait_rows(1 - par)

    @pl.when(i == n - 1)
    def _():
        wait_rows(par)


def _dispatch(h2p, slots, n_slots, *, td):
    t, half = h2p.shape
    xs0 = jnp.zeros((n_slots, half), jnp.uint32)
    return pl.pallas_call(
        functools.partial(_dispatch_kernel, td=td),
        grid=(t // td,),
        in_specs=[pl.BlockSpec((1, 1, 2 * td), lambda i: (i, 0, 0), memory_space=pltpu.SMEM),
                  pl.BlockSpec(memory_space=pl.ANY),
                  pl.BlockSpec(memory_space=pl.ANY)],
        out_specs=pl.BlockSpec(memory_space=pl.ANY),
        out_shape=jax.ShapeDtypeStruct(xs0.shape, jnp.uint32),
        scratch_shapes=[pltpu.VMEM((3, td, half), jnp.uint32), pltpu.SemaphoreType.DMA((3,)),
                        pltpu.SemaphoreType.DMA((2,))],
        input_output_aliases={2: 0},
        compiler_params=pltpu.CompilerParams(dimension_semantics=("arbitrary",), vmem_limit_bytes=VMEM_LIMIT,
                                             has_side_effects=True),
        name="dispatch",
    )(slots, h2p, xs0)


def _expert_kernel(te_ref, nu_ref, xs_ref, wgu_ref, wdn_ref, ys_ref):
    i = pl.program_id(0)
    half = wgu_ref.shape[1] // 2

    @pl.when(i < nu_ref[0])
    def _():
        lo, hi = _unpack_pair(xs_ref[...])
        gu = _dot(lo.astype(BF16), wgu_ref[0, :half, :]) + _dot(hi.astype(BF16), wgu_ref[0, half:, :])
        act = (jax.nn.silu(gu[:, :D_EXPERT]) * gu[:, D_EXPERT:]).astype(BF16)
        y = _dot(act, wdn_ref[0]).astype(BF16).astype(F32)
        ys_ref[...] = _pack_pair(y[:, :half], y[:, half:])

    @pl.when(i >= nu_ref[0])
    def _():
        ys_ref[...] = jnp.zeros_like(ys_ref)


def _experts(xs, tile_expert, n_used, lw, *, tme):
    n_slots, half = xs.shape
    d = 2 * half
    grid_spec = pltpu.PrefetchScalarGridSpec(
        num_scalar_prefetch=2,
        grid=(n_slots // tme,),
        in_specs=[pl.BlockSpec((tme, half), lambda i, te, nu: (i, 0)),
                  pl.BlockSpec((1, d, 2 * D_EXPERT), lambda i, te, nu: (te[i], 0, 0)),
                  pl.BlockSpec((1, D_EXPERT, d), lambda i, te, nu: (te[i], 0, 0))],
        out_specs=pl.BlockSpec((tme, half), lambda i, te, nu: (i, 0)),
    )
    return pl.pallas_call(
        _expert_kernel,
        grid_spec=grid_spec,
        out_shape=jax.ShapeDtypeStruct((n_slots, half), jnp.uint32),
        compiler_params=pltpu.CompilerParams(dimension_semantics=("arbitrary",), vmem_limit_bytes=VMEM_LIMIT),
        name="experts",
    )(tile_expert, n_used, xs, lw["w_gate_up"], lw["w_down"])


def _combine_kernel(s_cur_ref, s_nxt_ref, x1_ref, info_ref, ys_hbm, fn_ref, out_ref, ybuf, sem,
                    *, tmc, final_norm):
    i = pl.program_id(0)
    n = pl.num_programs(0)
    half = x1_ref.shape[1] // 2

    def gather(idx_ref, slot):
        for r in range(2 * tmc):
            src = idx_ref[0, 0, r]
            pltpu.make_async_copy(ys_hbm.at[pl.ds(src, 1)], ybuf.at[slot, pl.ds(r, 1)], sem.at[slot]).start()

    @pl.when(i == 0)
    def _():
        gather(s_cur_ref, 0)

    @pl.when(i + 1 < n)
    def _():
        gather(s_nxt_ref, (i + 1) % 2)

    slot = i % 2
    pltpu.make_async_copy(ys_hbm.at[pl.ds(0, 2 * tmc)], ybuf.at[slot], sem.at[slot]).wait()
    lo1, hi1 = _unpack_pair(ybuf[slot, 0:tmc, :])
    lo2, hi2 = _unpack_pair(ybuf[slot, tmc:2 * tmc, :])
    info = info_ref[...]
    w1 = info[:, _L_W1:_L_W1 + 1]
    w2 = info[:, _L_W2:_L_W2 + 1]
    lo = x1_ref[:, :half] + w1 * lo1 + w2 * lo2
    hi = x1_ref[:, half:] + w1 * hi1 + w2 * hi2
    if final_norm:
        ms = (jnp.sum(lo * lo, axis=-1, keepdims=True) + jnp.sum(hi * hi, axis=-1, keepdims=True)) / (2 * half)
        inv = lax.rsqrt(ms + EPS)
        lo = lo * inv * fn_ref[:, :half]
        hi = hi * inv * fn_ref[:, half:]
    out_ref[:, :half] = lo
    out_ref[:, half:] = hi


def _combine(x1, info, ys, slots, fn, *, tmc, final_norm):
    t, d = x1.shape
    n = t // tmc
    return pl.pallas_call(
        functools.partial(_combine_kernel, tmc=tmc, final_norm=final_norm),
        grid=(n,),
        in_specs=[pl.BlockSpec((1, 1, 2 * tmc), lambda i: (i, 0, 0), memory_space=pltpu.SMEM),
                  pl.BlockSpec((1, 1, 2 * tmc), lambda i: (jnp.minimum(i + 1, n - 1), 0, 0),
                               memory_space=pltpu.SMEM),
                  pl.BlockSpec((tmc, d), lambda i: (i, 0)),
                  pl.BlockSpec((tmc, LANES), lambda i: (i, 0)),
                  pl.BlockSpec(memory_space=pl.ANY),
                  pl.BlockSpec((1, d), lambda i: (0, 0))],
        out_specs=pl.BlockSpec((tmc, d), lambda i: (i, 0)),
        out_shape=jax.ShapeDtypeStruct((t, d), F32),
        scratch_shapes=[pltpu.VMEM((2, 2 * tmc, d // 2), jnp.uint32), pltpu.SemaphoreType.DMA((2,))],
        compiler_params=pltpu.CompilerParams(dimension_semantics=("arbitrary",), vmem_limit_bytes=VMEM_LIMIT),
        name="combine",
    )(slots, slots, x1, info, ys, fn)


def _moe(x1, h2p, info, meta, cnt, lw, fn, *, tme, tmc, final_norm):
    t = x1.shape[0]
    n_tiles = (2 * t) // tme + N_EXPERTS
    counts = cnt[0, :N_EXPERTS].astype(jnp.int32)
    padded = ((counts + tme - 1) // tme) * tme
    ends = jnp.cumsum(padded)
    starts = ends - padded
    mi = meta.astype(jnp.int32)
    expert_ids = jnp.arange(N_EXPERTS, dtype=jnp.int32)

    def slot_of(e, r_hi, r_lo):
        start = jnp.sum(jnp.where(e[..., None] == expert_ids, starts, 0), axis=-1)
        return (start + r_hi * 256 + r_lo).reshape(t // tmc, 1, tmc)

    slot1 = slot_of(mi[:, _M_E1], mi[:, _M_R1H], mi[:, _M_R1L])
    slot2 = slot_of(mi[:, _M_E2], mi[:, _M_R2H], mi[:, _M_R2L])
    slots = jnp.concatenate([slot1, slot2], axis=2)
    tile_start = jnp.arange(n_tiles, dtype=jnp.int32) * tme
    tile_expert = jnp.minimum(jnp.sum((ends[None, :] <= tile_start[:, None]).astype(jnp.int32), axis=1),
                              N_EXPERTS - 1)
    n_used = ends[-1:] // tme
    xs = _dispatch(h2p, slots, n_tiles * tme, td=tmc)
    ys = _experts(xs, tile_expert, n_used, lw, tme=tme)
    return _combine(x1, info, ys, slots, fn, tmc=tmc, final_norm=final_norm)


def _pack_layer(l, w_in, norm_mix, q_norm, w_q_up, kv_norm, w_kv_up, sg_norm, w_s, b_s, conv_w,
                w_a_out, w_b_out, w_c_out, w_o, norm_ffn, w_group_router, b_group_router,
                w_expert_router, b_expert_router, w_gate_up, w_down):
    d = w_in.shape[1]
    wi = w_in[l]
    o_kr = Q_RANK + KV_RANK
    kr_slot = jnp.zeros((d, HEAD_PAD), F32).at[:, QK_NOPE:QK_HEAD].set(wi[:, o_kr:o_kr + QK_ROPE])
    w_in_p = jnp.concatenate([wi[:, :o_kr], kr_slot, wi[:, o_kr + QK_ROPE:]], axis=1).astype(BF16)
    wq = w_q_up[l].reshape(Q_RANK, MLA_HEADS, QK_HEAD)
    wq = jnp.pad(wq, ((0, 0), (0, 0), (0, HEAD_PAD - QK_HEAD))).reshape(Q_RANK, MLA_HEADS * HEAD_PAD)
    wkv = w_kv_up[l].reshape(KV_RANK, MLA_HEADS, QK_NOPE + V_HEAD)
    wk = jnp.pad(wkv[:, :, :QK_NOPE], ((0, 0), (0, 0), (0, HEAD_PAD - QK_NOPE)))
    wk = wk.reshape(KV_RANK, MLA_HEADS * HEAD_PAD)
    wv = wkv[:, :, QK_NOPE:].reshape(KV_RANK, MLA_HEADS * V_HEAD)
    w_r = jnp.zeros((d, LANES), F32)
    w_r = w_r.at[:, :N_EXPERTS].set(w_expert_router[l])
    w_r = w_r.at[:, GROUP_LANE0:GROUP_LANE0 + N_GROUPS].set(w_group_router[l])
    b_r = jnp.zeros((1, LANES), F32)
    b_r = b_r.at[0, :N_EXPERTS].set(b_expert_router[l])
    b_r = b_r.at[0, GROUP_LANE0:GROUP_LANE0 + N_GROUPS].set(b_group_router[l])
    return {
        "norm_mix": norm_mix[l][None, :], "w_in": w_in_p,
        "q_norm": q_norm[l][None, :], "w_q": wq.astype(BF16),
        "kv_norm": kv_norm[l][None, :], "w_k": wk.astype(BF16), "w_v": wv.astype(BF16),
        "sg_norm": sg_norm[l][None, :], "w_s": w_s[l],
        "b_s": jnp.broadcast_to(b_s[l][:, :, None], (SG_GROUPS, CHUNK, SG_GROUP_DIM)),
        "conv_w": conv_w[l],
        "w_a_out": w_a_out[l].astype(BF16), "w_b_out": w_b_out[l].astype(BF16),
        "w_c_out": w_c_out[l].astype(BF16), "w_o": w_o[l].astype(BF16),
        "norm_ffn": norm_ffn[l][None, :], "w_router": w_r.astype(BF16), "b_router": b_r,
        "w_gate_up": w_gate_up[l].astype(BF16), "w_down": w_down[l].astype(BF16),
    }


def _rope_tables(positions):
    half = QK_ROPE // 2
    inv = ROPE_BASE ** (-jnp.arange(0, QK_ROPE, 2, dtype=F32) / QK_ROPE)
    ang = positions.astype(F32).reshape(-1)[:, None] * inv
    cos, sin = jnp.cos(ang), jnp.sin(ang)
    t = ang.shape[0]
    ones = jnp.ones((t, QK_NOPE), F32)
    z16 = jnp.zeros((t, half), F32)
    z32 = jnp.zeros((t, HEAD_PAD - QK_HEAD), F32)
    z64 = jnp.zeros((t, QK_NOPE), F32)
    rc = jnp.concatenate([ones, cos, cos, z32], axis=1)
    rm = jnp.concatenate([z64, -sin, z16, z32], axis=1)
    rp = jnp.concatenate([z64, z16, sin, z32], axis=1)
    return rc, rm, rp


def kernel(x, positions, norm_mix, w_in, q_norm, w_q_up, kv_norm, w_kv_up, sg_norm, w_s, b_s, conv_w,
           w_a_out, w_b_out, w_c_out, w_o, norm_ffn, w_group_router, b_group_router, w_expert_router,
           b_expert_router, w_gate_up, w_down, final_norm):
    batch, seq, d = x.shape
    depth = w_in.shape[0]
    tm = min(512, seq)
    tq = tm
    tme = 512
    tmc = min(256, seq)
    rc, rm, rp = _rope_tables(positions)
    x2 = x.reshape(batch * seq, d)
    fn = final_norm[None, :]
    for l in range(depth):
        lw = _pack_layer(l, w_in, norm_mix, q_norm, w_q_up, kv_norm, w_kv_up, sg_norm, w_s, b_s, conv_w,
                         w_a_out, w_b_out, w_c_out, w_o, norm_ffn, w_group_router, b_group_router,
                         w_expert_router, b_expert_router, w_gate_up, w_down)
        q, k, v, part, ga = _mixer_pre(x2, lw, rc, rm, rp, seq=seq, tm=tm)
        o = _attention(q, k, v, batch=batch, seq=seq, tq=tq)
        x1, h2p, info, meta, cnt = _mixer_post(o, part, ga, x2, lw, tm=tm)
        x2 = _moe(x1, h2p, info, meta, cnt, lw, fn, tme=tme, tmc=tmc, final_norm=(l == depth - 1))
    return x2.reshape(batch, seq, d)
```

```python
import functools

import jax
import jax.numpy as jnp
from jax import lax
from jax.experimental import pallas as pl
from jax.experimental.pallas import tpu as pltpu

F32 = jnp.float32
BF16 = jnp.bfloat16

EPS = 1e-6
MLA_HEADS = 8
QK_NOPE = 64
QK_ROPE = 32
QK_HEAD = QK_NOPE + QK_ROPE
V_HEAD = 64
Q_RANK = 384
KV_RANK = 256
ROPE_BASE = 10000.0
SG_GROUPS = 4
SG_GROUP_DIM = 128
SG_WIDTH = SG_GROUPS * SG_GROUP_DIM
CHUNK = 128
CONV_WIDTH = 512
CONV_K = 3
N_GROUPS = 4
EXPERTS_PER_GROUP = 8
N_EXPERTS = N_GROUPS * EXPERTS_PER_GROUP
D_EXPERT = 256

LANES = 128
HEAD_PAD = 128
MASK_VALUE = -1e30
LOG2_E = 1.4426950408889634
VMEM_LIMIT = 56 * 1024 * 1024

_P1 = Q_RANK + KV_RANK + HEAD_PAD
_P2 = _P1 + 2 * SG_WIDTH
_P3 = _P2 + 3 * CONV_WIDTH
GROUP_LANE0 = N_EXPERTS


def _dot(a, b):
    return jnp.dot(a, b, preferred_element_type=F32)


def _rms(xf, g):
    return xf * lax.rsqrt(jnp.mean(xf * xf, axis=-1, keepdims=True) + EPS) * g


def _gelu(x):
    return 0.5 * x * (1.0 + jnp.tanh(0.7978845608028654 * (x + 0.044715 * (x * x * x))))


def _rope(t, rc, rm, rp):
    return t * rc + pltpu.roll(t, HEAD_PAD - QK_ROPE // 2, 1) * rm + pltpu.roll(t, QK_ROPE // 2, 1) * rp


def _const_spec(shape):
    nd = len(shape)
    return pl.BlockSpec(shape, lambda *_: (0,) * nd, pipeline_mode=pl.Buffered(1))


def _mixer_pre_kernel(x_ref, nm_ref, win_ref, qn_ref, wq_ref, kvn_ref, wk_ref, wv_ref, sgn_ref,
                      ws_ref, bs_ref, cw_ref, wb_ref, wc_ref, rc_ref, rm_ref, rp_ref,
                      qt_ref, k_ref, vt_ref, part_ref, ga_ref,
                      halo_ref, ybin_ref, *, tiles_per_seq, tm):
    i = pl.program_id(0)
    d_model = x_ref.shape[1]
    h = _rms(x_ref[...], nm_ref[...]).astype(BF16)

    p1 = _dot(h, win_ref[:, 0:_P1])
    rc, rm, rp = rc_ref[...], rm_ref[...], rp_ref[...]
    qn = _rms(p1[:, :Q_RANK], qn_ref[...]).astype(BF16)
    qf = _dot(qn, wq_ref[...])
    scale = QK_HEAD ** -0.5 * LOG2_E
    for hd in range(MLA_HEADS):
        sl = slice(hd * HEAD_PAD, (hd + 1) * HEAD_PAD)
        qt_ref[0, 0, sl, :] = (_rope(qf[:, sl], rc, rm, rp) * scale).T.astype(BF16)
    kvn = _rms(p1[:, Q_RANK:Q_RANK + KV_RANK], kvn_ref[...]).astype(BF16)
    kr = _rope(p1[:, Q_RANK + KV_RANK:_P1], rc, rm, rp)
    kf = _dot(kvn, wk_ref[...])
    for hd in range(MLA_HEADS):
        sl = slice(hd * HEAD_PAD, (hd + 1) * HEAD_PAD)
        k_ref[:, sl] = (kf[:, sl] + kr).astype(BF16)
    vt_ref[0, 0] = _dot(kvn, wv_ref[...]).T.astype(BF16)

    p2 = _dot(h, win_ref[:, _P1:_P2])
    u = _gelu(p2[:, :SG_WIDTH])
    vb = _rms(_gelu(p2[:, SG_WIDTH:]), sgn_ref[...]).astype(BF16)
    r_i = lax.broadcasted_iota(jnp.int32, (CHUNK, CHUNK), 0)
    c_i = lax.broadcasted_iota(jnp.int32, (CHUNK, CHUNK), 1)
    for g in range(SG_GROUPS):
        wg = jnp.where(c_i <= r_i, ws_ref[g], 0.0).astype(BF16)
        gs = slice(g * SG_GROUP_DIM, (g + 1) * SG_GROUP_DIM)
        for c in range(tm // CHUNK):
            cs = slice(c * CHUNK, (c + 1) * CHUNK)
            mixed = _dot(wg, vb[cs, gs]) + bs_ref[g]
            ybin_ref[cs, gs] = (u[cs, gs] * mixed).astype(BF16)

    @pl.when(i % tiles_per_seq == 0)
    def _():
        halo_ref[...] = jnp.zeros_like(halo_ref)

    p3 = _dot(h, win_ref[:, _P2:_P3])
    z = p3[:, CONV_WIDTH:2 * CONV_WIDTH] * p3[:, 2 * CONV_WIDTH:]
    row = lax.broadcasted_iota(jnp.int32, z.shape, 0)
    halo = halo_ref[...]
    last1 = halo[7:8, :]
    last2 = halo[6:7, :]
    z1 = jnp.where(row == 0, last1, pltpu.roll(z, 1, 0))
    z2 = jnp.where(row == 0, last2, jnp.where(row == 1, last1, pltpu.roll(z, 2, 0)))
    y = cw_ref[0:1, :] * z2 + cw_ref[1:2, :] * z1 + cw_ref[2:3, :] * z
    yc_in = (p3[:, :CONV_WIDTH] * y).astype(BF16)
    halo_ref[...] = z[tm - 8:tm, :]

    p4 = _dot(h, win_ref[:, _P3:_P3 + 3 * d_model])
    yb = _dot(ybin_ref[...], wb_ref[...])
    yc = _dot(yc_in, wc_ref[...])
    part = (jax.nn.sigmoid(p4[:, d_model:2 * d_model]) * yb
            + jax.nn.sigmoid(p4[:, 2 * d_model:]) * yc)
    part_ref[...] = part.astype(BF16)
    ga_ref[...] = jax.nn.sigmoid(p4[:, :d_model]).astype(BF16)


def _mixer_pre(x2, lw, rc, rm, rp, *, seq, tm):
    t, d = x2.shape
    row = lambda w: pl.BlockSpec((tm, w), lambda i: (i, 0))
    consts = [lw["norm_mix"], lw["w_in"], lw["q_norm"], lw["w_q"], lw["kv_norm"], lw["w_k"], lw["w_v"],
              lw["sg_norm"], lw["w_s"], lw["b_s"], lw["conv_w"], lw["w_b_out"], lw["w_c_out"]]
    tps = seq // tm
    kern = functools.partial(_mixer_pre_kernel, tiles_per_seq=tps, tm=tm)
    slab = lambda w: pl.BlockSpec((1, 1, w, tm), lambda i: (i // tps, i % tps, 0, 0))
    return pl.pallas_call(
        kern,
        grid=(t // tm,),
        in_specs=[row(d)] + [_const_spec(c.shape) for c in consts] + [row(HEAD_PAD)] * 3,
        out_specs=[slab(MLA_HEADS * HEAD_PAD), row(MLA_HEADS * HEAD_PAD), slab(MLA_HEADS * V_HEAD), row(d), row(d)],
        out_shape=[jax.ShapeDtypeStruct((t // seq, tps, MLA_HEADS * HEAD_PAD, tm), BF16),
                   jax.ShapeDtypeStruct((t, MLA_HEADS * HEAD_PAD), BF16),
                   jax.ShapeDtypeStruct((t // seq, tps, MLA_HEADS * V_HEAD, tm), BF16),
                   jax.ShapeDtypeStruct((t, d), BF16),
                   jax.ShapeDtypeStruct((t, d), BF16)],
        scratch_shapes=[pltpu.VMEM((8, CONV_WIDTH), F32), pltpu.VMEM((tm, SG_WIDTH), BF16)],
        compiler_params=pltpu.CompilerParams(dimension_semantics=("arbitrary",), vmem_limit_bytes=VMEM_LIMIT),
        name="mixer_pre",
    )(x2, *consts, rc, rm, rp)


ACC_ROWS = V_HEAD + 16


def _attn_kernel(qt_ref, k_ref, vt_ref, o_ref, m_ref, acc_ref, sa_ref, sb_ref, mca_ref, mcb_ref, *, tq):
    qi = pl.program_id(2)
    m_ref[...] = jnp.full_like(m_ref, MASK_VALUE)
    acc_ref[...] = jnp.zeros_like(acc_ref)

    bufs = ((sa_ref, mca_ref), (sb_ref, mcb_ref))

    def score(j, buf, masked):
        s_ref, mc_ref = bufs[buf]
        start = pl.multiple_of(j * tq, tq)
        for sub in range(2):
            k_j = k_ref[pl.ds(start, tq), sub * HEAD_PAD:(sub + 1) * HEAD_PAD]
            st = _dot(k_j, qt_ref[0, 0, sub * HEAD_PAD:(sub + 1) * HEAD_PAD, :])
            if masked:
                kv_i = lax.broadcasted_iota(jnp.int32, st.shape, 0)
                q_i = lax.broadcasted_iota(jnp.int32, st.shape, 1)
                st = jnp.where(kv_i <= q_i, st, MASK_VALUE)
            s_ref[sub] = st
            mc_ref[sub] = jnp.broadcast_to(jnp.max(st, axis=0, keepdims=True), (8, tq))

    def value(j, buf):
        s_ref, mc_ref = bufs[buf]
        for sub in range(2):
            m_prev = m_ref[sub]
            m_next = jnp.maximum(m_prev, mc_ref[sub])
            alpha = jnp.exp2(m_prev - m_next)
            pt = jnp.exp2((s_ref[sub] - m_next[0:1, :]).astype(BF16))
            m_ref[sub] = m_next
            vt_j = jnp.concatenate([vt_ref[0, j, sub * V_HEAD:(sub + 1) * V_HEAD, :],
                                    jnp.ones((ACC_ROWS - V_HEAD, tq), BF16)], axis=0)
            acc_ref[sub] = alpha[0:1, :] * acc_ref[sub] + _dot(vt_j, pt)

    @pl.when(qi == 0)
    def _():
        score(0, 0, True)
        value(0, 0)

    @pl.when(qi > 0)
    def _():
        score(0, 0, False)
        n_pairs = (qi - 1) // 2

        def body(p, carry):
            score(2 * p + 1, 1, False)
            value(2 * p, 0)
            score(2 * p + 2, 0, False)
            value(2 * p + 1, 1)
            return carry

        lax.fori_loop(0, n_pairs, body, 0)
        t0 = 2 * n_pairs

        @pl.when(qi % 2 == 1)
        def _():
            score(qi, 1, True)
            value(t0, 0)
            value(qi, 1)

        @pl.when(qi % 2 == 0)
        def _():
            score(t0 + 1, 1, False)
            value(t0, 0)
            score(qi, 0, True)
            value(t0 + 1, 1)
            value(qi, 0)

    ot = jnp.concatenate([acc_ref[sub, 0:V_HEAD, :] / acc_ref[sub, V_HEAD:V_HEAD + 1, :] for sub in range(2)],
                         axis=0)
    o_ref[...] = ot.T.astype(BF16)


def _attention(qt, k, vt, *, batch, seq, tq):
    t = k.shape[0]
    nq = seq // tq
    pairs = MLA_HEADS // 2
    return pl.pallas_call(
        functools.partial(_attn_kernel, tq=tq),
        grid=(batch, pairs, nq),
        in_specs=[pl.BlockSpec((1, 1, 2 * HEAD_PAD, tq), lambda b, hp, qi: (b, qi, hp, 0)),
                  pl.BlockSpec((seq, 2 * HEAD_PAD), lambda b, hp, qi: (b, hp)),
                  pl.BlockSpec((1, nq, 2 * V_HEAD, tq), lambda b, hp, qi: (b, 0, hp, 0))],
        out_specs=pl.BlockSpec((tq, 2 * V_HEAD), lambda b, hp, qi: (b * nq + qi, hp)),
        out_shape=jax.ShapeDtypeStruct((t, MLA_HEADS * V_HEAD), BF16),
        scratch_shapes=[pltpu.VMEM((2, 8, tq), F32), pltpu.VMEM((2, ACC_ROWS, tq), F32),
                        pltpu.VMEM((2, tq, tq), F32), pltpu.VMEM((2, tq, tq), F32),
                        pltpu.VMEM((2, 8, tq), F32), pltpu.VMEM((2, 8, tq), F32)],
        compiler_params=pltpu.CompilerParams(dimension_semantics=("arbitrary",) * 3, vmem_limit_bytes=VMEM_LIMIT),
        name="attention",
    )(qt, k, vt)


def _pack_pair(lo, hi):
    lo_b = lax.bitcast_convert_type(lo, jnp.uint32)
    hi_b = lax.bitcast_convert_type(hi, jnp.uint32)
    return (hi_b & jnp.uint32(0xFFFF0000)) | (lo_b >> 16)


def _unpack_pair(p):
    lo = lax.bitcast_convert_type(p << 16, F32)
    hi = lax.bitcast_convert_type(p & jnp.uint32(0xFFFF0000), F32)
    return lo, hi


def _route(logits):
    lane = lax.broadcasted_iota(jnp.int32, logits.shape, 1)
    lane_f = lane.astype(F32)
    big = float(LANES)
    is_grp = (lane >= GROUP_LANE0) & (lane < GROUP_LANE0 + N_GROUPS)
    g_log = jnp.where(is_grp, logits, MASK_VALUE)
    g_max = jnp.max(g_log, axis=1, keepdims=True)
    g_sum = jnp.sum(jnp.exp(g_log - g_max), axis=1, keepdims=True)
    g_p = 1.0 / g_sum
    g_idx = jnp.min(jnp.where(g_log == g_max, lane_f, big), axis=1, keepdims=True) - float(GROUP_LANE0)
    lo = g_idx * float(EXPERTS_PER_GROUP)
    in_grp = (lane_f >= lo) & (lane_f < lo + float(EXPERTS_PER_GROUP))
    e_log = jnp.where(in_grp, logits, MASK_VALUE)
    e_max = jnp.max(e_log, axis=1, keepdims=True)
    e_exp = jnp.exp(e_log - e_max)
    e_prob = e_exp / jnp.sum(e_exp, axis=1, keepdims=True)
    cand = jnp.where(in_grp, e_prob, -1.0)
    p1 = jnp.max(cand, axis=1, keepdims=True)
    i1 = jnp.min(jnp.where(cand == p1, lane_f, big), axis=1, keepdims=True)
    cand2 = jnp.where(lane_f == i1, -1.0, cand)
    p2 = jnp.max(cand2, axis=1, keepdims=True)
    i2 = jnp.min(jnp.where(cand2 == p2, lane_f, big), axis=1, keepdims=True)
    denom = p1 + p2
    return g_p * (p1 / denom), g_p * (p2 / denom), i1, i2


_L_W1, _L_W2 = 0, 1
_M_E1, _M_E2, _M_R1H, _M_R1L, _M_R2H, _M_R2L = range(6)


def _mixer_post_kernel(o_ref, part_ref, ga_ref, x_ref, wa_ref, wo_ref, nf_ref, wr_ref, br_ref,
                       x1_ref, h2p_ref, info_ref, meta_ref, cnt_ref, run_ref, *, tm):
    i = pl.program_id(0)
    half = x_ref.shape[1] // 2

    @pl.when(i == 0)
    def _():
        run_ref[...] = jnp.zeros_like(run_ref)

    ya = _dot(o_ref[...], wa_ref[...])
    merged = ga_ref[...].astype(F32) * ya + part_ref[...].astype(F32)
    x1 = x_ref[...] + _dot(merged.astype(BF16), wo_ref[...])
    x1_ref[...] = x1
    h2 = _rms(x1, nf_ref[...]).astype(BF16)
    h2f = h2.astype(F32)
    h2p_ref[...] = _pack_pair(h2f[:, :half], h2f[:, half:])
    w1, w2, i1, i2 = _route(_dot(h2, wr_ref[...]) + br_ref[...])

    lane_f = lax.broadcasted_iota(jnp.int32, (tm, LANES), 1).astype(F32)
    oh1 = lane_f == i1
    oh2 = lane_f == i2
    both = jnp.where(oh1, 1.0, 0.0) + jnp.where(oh2, 1.0, 0.0)
    r_i = lax.broadcasted_iota(jnp.int32, (tm, tm), 0)
    c_i = lax.broadcasted_iota(jnp.int32, (tm, tm), 1)
    earlier = jnp.where(c_i < r_i, 1.0, 0.0).astype(BF16)
    before = _dot(earlier, both.astype(BF16)) + run_ref[0:1, :]
    rank1 = jnp.sum(jnp.where(oh1, before, 0.0), axis=1, keepdims=True)
    rank2 = jnp.sum(jnp.where(oh2, before, 0.0), axis=1, keepdims=True)
    run_ref[0:1, :] = run_ref[0:1, :] + jnp.sum(both, axis=0, keepdims=True)
    cnt_ref[...] = run_ref[...]

    lane = lax.broadcasted_iota(jnp.int32, (tm, LANES), 1)
    info_ref[...] = jnp.where(lane == _L_W1, w1, jnp.where(lane == _L_W2, w2, 0.0))

    r1_hi = jnp.floor(rank1 * (1.0 / 256.0))
    r2_hi = jnp.floor(rank2 * (1.0 / 256.0))
    cols = (i1, i2, r1_hi, rank1 - 256.0 * r1_hi, r2_hi, rank2 - 256.0 * r2_hi)
    vals = jnp.zeros((tm, LANES), F32)
    for ln, val in enumerate(cols):
        vals = jnp.where(lane == ln, val, vals)
    s_r = lax.broadcasted_iota(jnp.int32, (8, LANES), 0)
    s_c = lax.broadcasted_iota(jnp.int32, (8, LANES), 1)
    sel = jnp.where((s_r == s_c) & (s_r < len(cols)), 1.0, 0.0).astype(BF16)
    meta_ref[0] = lax.dot_general(sel, vals.astype(BF16), (((1,), (1,)), ((), ())),
                                  preferred_element_type=F32)


def _mixer_post(o, part, ga, x2, lw, *, tm):
    t, d = x2.shape
    row = lambda w: pl.BlockSpec((tm, w), lambda i: (i, 0))
    consts = [lw["w_a_out"], lw["w_o"], lw["norm_ffn"], lw["w_router"], lw["b_router"]]
    return pl.pallas_call(
        functools.partial(_mixer_post_kernel, tm=tm),
        grid=(t // tm,),
        in_specs=[row(o.shape[1]), row(d), row(d), row(d)] + [_const_spec(c.shape) for c in consts],
        out_specs=[row(d), row(d // 2), row(LANES), pl.BlockSpec((1, 8, tm), lambda i: (i, 0, 0)),
                   pl.BlockSpec((8, LANES), lambda i: (0, 0))],
        out_shape=[jax.ShapeDtypeStruct((t, d), F32), jax.ShapeDtypeStruct((t, d // 2), jnp.uint32),
                   jax.ShapeDtypeStruct((t, LANES), F32), jax.ShapeDtypeStruct((t // tm, 8, tm), F32),
                   jax.ShapeDtypeStruct((8, LANES), F32)],
        scratch_shapes=[pltpu.VMEM((8, LANES), F32)],
        compiler_params=pltpu.CompilerParams(dimension_semantics=("arbitrary",), vmem_limit_bytes=VMEM_LIMIT),
        name="mixer_post",
    )(o, part, ga, x2, *consts)


def _dispatch_kernel(slots_ref, h2p_hbm, xs_in_hbm, xs_hbm, buf, lsem, rsem, *, td):
    del xs_in_hbm
    i = pl.program_id(0)
    n = pl.num_programs(0)

    def load(j, b):
        return pltpu.make_async_copy(h2p_hbm.at[pl.ds(j * td, td)], buf.at[b], lsem.at[b])

    @pl.when(i == 0)
    def _():
        load(0, 0).start()

    @pl.when(i + 1 < n)
    def _():
        load(i + 1, (i + 1) % 3).start()

    b = i % 3
    par = i % 2
    load(i, b).wait()
    for r in range(td):
        for k in range(2):
            dst = slots_ref[0, 0, k * td + r]
            pltpu.make_async_copy(buf.at[b, pl.ds(r, 1)], xs_hbm.at[pl.ds(dst, 1)], rsem.at[par]).start()

    def wait_rows(p):
        for _ in range(2):
            pltpu.make_async_copy(buf.at[0], xs_hbm.at[pl.ds(0, td)], rsem.at[p]).wait()

    @pl.when(i > 0)
    def _():
        wait_rows(1 - par)

    @pl.when(i == n - 1)
    def _():
        wait_rows(par)


def _dispatch(h2p, slots, n_slots, *, td):
    t, half = h2p.shape
    xs0 = jnp.zeros((n_slots, half), jnp.uint32)
    return pl.pallas_call(
        functools.partial(_dispatch_kernel, td=td),
        grid=(t // td,),
        in_specs=[pl.BlockSpec((1, 1, 2 * td), lambda i: (i, 0, 0), memory_space=pltpu.SMEM),
                  pl.BlockSpec(memory_space=pl.ANY),
                  pl.BlockSpec(memory_space=pl.ANY)],
        out_specs=pl.BlockSpec(memory_space=pl.ANY),
        out_shape=jax.ShapeDtypeStruct(xs0.shape, jnp.uint32),
        scratch_shapes=[pltpu.VMEM((3, td, half), jnp.uint32), pltpu.SemaphoreType.DMA((3,)),
                        pltpu.SemaphoreType.DMA((2,))],
        input_output_aliases={2: 0},
        compiler_params=pltpu.CompilerParams(dimension_semantics=("arbitrary",), vmem_limit_bytes=VMEM_LIMIT,
                                             has_side_effects=True),
        name="dispatch",
    )(slots, h2p, xs0)


def _expert_kernel(te_ref, nu_ref, xs_ref, wgu_ref, wdn_ref, ys_ref):
    i = pl.program_id(0)
    half = wgu_ref.shape[1] // 2

    @pl.when(i < nu_ref[0])
    def _():
        lo, hi = _unpack_pair(xs_ref[...])
        gu = _dot(lo.astype(BF16), wgu_ref[0, :half, :]) + _dot(hi.astype(BF16), wgu_ref[0, half:, :])
        act = (jax.nn.silu(gu[:, :D_EXPERT]) * gu[:, D_EXPERT:]).astype(BF16)
        y = _dot(act, wdn_ref[0]).astype(BF16).astype(F32)
        ys_ref[...] = _pack_pair(y[:, :half], y[:, half:])

    @pl.when(i >= nu_ref[0])
    def _():
        ys_ref[...] = jnp.zeros_like(ys_ref)


def _experts(xs, tile_expert, n_used, lw, *, tme):
    n_slots, half = xs.shape
    d = 2 * half
    grid_spec = pltpu.PrefetchScalarGridSpec(
        num_scalar_prefetch=2,
        grid=(n_slots // tme,),
        in_specs=[pl.BlockSpec((tme, half), lambda i, te, nu: (i, 0)),
                  pl.BlockSpec((1, d, 2 * D_EXPERT), lambda i, te, nu: (te[i], 0, 0)),
                  pl.BlockSpec((1, D_EXPERT, d), lambda i, te, nu: (te[i], 0, 0))],
        out_specs=pl.BlockSpec((tme, half), lambda i, te, nu: (i, 0)),
    )
    return pl.pallas_call(
        _expert_kernel,
        grid_spec=grid_spec,
        out_shape=jax.ShapeDtypeStruct((n_slots, half), jnp.uint32),
        compiler_params=pltpu.CompilerParams(dimension_semantics=("arbitrary",), vmem_limit_bytes=VMEM_LIMIT),
        name="experts",
    )(tile_expert, n_used, xs, lw["w_gate_up"], lw["w_down"])


def _combine_kernel(s_cur_ref, s_nxt_ref, x1_ref, info_ref, ys_hbm, fn_ref, out_ref, ybuf, sem,
                    *, tmc, final_norm):
    i = pl.program_id(0)
    n = pl.num_programs(0)
    half = x1_ref.shape[1] // 2

    def gather(idx_ref, slot):
        for r in range(2 * tmc):
            src = idx_ref[0, 0, r]
            pltpu.make_async_copy(ys_hbm.at[pl.ds(src, 1)], ybuf.at[slot, pl.ds(r, 1)], sem.at[slot]).start()

    @pl.when(i == 0)
    def _():
        gather(s_cur_ref, 0)

    @pl.when(i + 1 < n)
    def _():
        gather(s_nxt_ref, (i + 1) % 2)

    slot = i % 2
    pltpu.make_async_copy(ys_hbm.at[pl.ds(0, 2 * tmc)], ybuf.at[slot], sem.at[slot]).wait()
    lo1, hi1 = _unpack_pair(ybuf[slot, 0:tmc, :])
    lo2, hi2 = _unpack_pair(ybuf[slot, tmc:2 * tmc, :])
    info = info_ref[...]
    w1 = info[:, _L_W1:_L_W1 + 1]
    w2 = info[:, _L_W2:_L_W2 + 1]
    lo = x1_ref[:, :half] + w1 * lo1 + w2 * lo2
    hi = x1_ref[:, half:] + w1 * hi1 + w2 * hi2
    if final_norm:
        ms = (jnp.sum(lo * lo, axis=-1, keepdims=True) + jnp.sum(hi * hi, axis=-1, keepdims=True)) / (2 * half)
        inv = lax.rsqrt(ms + EPS)
        lo = lo * inv * fn_ref[:, :half]
        hi = hi * inv * fn_ref[:, half:]
    out_ref[:, :half] = lo
    out_ref[:, half:] = hi


def _combine(x1, info, ys, slots, fn, *, tmc, final_norm):
    t, d = x1.shape
    n = t // tmc
    return pl.pallas_call(
        functools.partial(_combine_kernel, tmc=tmc, final_norm=final_norm),
        grid=(n,),
        in_specs=[pl.BlockSpec((1, 1, 2 * tmc), lambda i: (i, 0, 0), memory_space=pltpu.SMEM),
                  pl.BlockSpec((1, 1, 2 * tmc), lambda i: (jnp.minimum(i + 1, n - 1), 0, 0),
                               memory_space=pltpu.SMEM),
                  pl.BlockSpec((tmc, d), lambda i: (i, 0)),
                  pl.BlockSpec((tmc, LANES), lambda i: (i, 0)),
                  pl.BlockSpec(memory_space=pl.ANY),
                  pl.BlockSpec((1, d), lambda i: (0, 0))],
        out_specs=pl.BlockSpec((tmc, d), lambda i: (i, 0)),
        out_shape=jax.ShapeDtypeStruct((t, d), F32),
        scratch_shapes=[pltpu.VMEM((2, 2 * tmc, d // 2), jnp.uint32), pltpu.SemaphoreType.DMA((2,))],
        compiler_params=pltpu.CompilerParams(dimension_semantics=("arbitrary",), vmem_limit_bytes=VMEM_LIMIT),
        name="combine",
    )(slots, slots, x1, info, ys, fn)


def _moe(x1, h2p, info, meta, cnt, lw, fn, *, tme, tmc, final_norm):
    t = x1.shape[0]
    n_tiles = (2 * t) // tme + N_EXPERTS
    counts = cnt[0, :N_EXPERTS].astype(jnp.int32)
    padded = ((counts + tme - 1) // tme) * tme
    ends = jnp.cumsum(padded)
    starts = ends - padded
    mi = meta.astype(jnp.int32)
    expert_ids = jnp.arange(N_EXPERTS, dtype=jnp.int32)

    def slot_of(e, r_hi, r_lo):
        start = jnp.sum(jnp.where(e[..., None] == expert_ids, starts, 0), axis=-1)
        return (start + r_hi * 256 + r_lo).reshape(t // tmc, 1, tmc)

    slot1 = slot_of(mi[:, _M_E1], mi[:, _M_R1H], mi[:, _M_R1L])
    slot2 = slot_of(mi[:, _M_E2], mi[:, _M_R2H], mi[:, _M_R2L])
    slots = jnp.concatenate([slot1, slot2], axis=2)
    tile_start = jnp.arange(n_tiles, dtype=jnp.int32) * tme
    tile_expert = jnp.minimum(jnp.sum((ends[None, :] <= tile_start[:, None]).astype(jnp.int32), axis=1),
                              N_EXPERTS - 1)
    n_used = ends[-1:] // tme
    xs = _dispatch(h2p, slots, n_tiles * tme, td=tmc)
    ys = _experts(xs, tile_expert, n_used, lw, tme=tme)
    return _combine(x1, info, ys, slots, fn, tmc=tmc, final_norm=final_norm)


def _pack_layer(l, w_in, norm_mix, q_norm, w_q_up, kv_norm, w_kv_up, sg_norm, w_s, b_s, conv_w,
                w_a_out, w_b_out, w_c_out, w_o, norm_ffn, w_group_router, b_group_router,
                w_expert_router, b_expert_router, w_gate_up, w_down):
    d = w_in.shape[1]
    wi = w_in[l]
    o_kr = Q_RANK + KV_RANK
    kr_slot = jnp.zeros((d, HEAD_PAD), F32).at[:, QK_NOPE:QK_HEAD].set(wi[:, o_kr:o_kr + QK_ROPE])
    w_in_p = jnp.concatenate([wi[:, :o_kr], kr_slot, wi[:, o_kr + QK_ROPE:]], axis=1).astype(BF16)
    wq = w_q_up[l].reshape(Q_RANK, MLA_HEADS, QK_HEAD)
    wq = jnp.pad(wq, ((0, 0), (0, 0), (0, HEAD_PAD - QK_HEAD))).reshape(Q_RANK, MLA_HEADS * HEAD_PAD)
    wkv = w_kv_up[l].reshape(KV_RANK, MLA_HEADS, QK_NOPE + V_HEAD)
    wk = jnp.pad(wkv[:, :, :QK_NOPE], ((0, 0), (0, 0), (0, HEAD_PAD - QK_NOPE)))
    wk = wk.reshape(KV_RANK, MLA_HEADS * HEAD_PAD)
    wv = wkv[:, :, QK_NOPE:].reshape(KV_RANK, MLA_HEADS * V_HEAD)
    w_r = jnp.zeros((d, LANES), F32)
    w_r = w_r.at[:, :N_EXPERTS].set(w_expert_router[l])
    w_r = w_r.at[:, GROUP_LANE0:GROUP_LANE0 + N_GROUPS].set(w_group_router[l])
    b_r = jnp.zeros((1, LANES), F32)
    b_r = b_r.at[0, :N_EXPERTS].set(b_expert_router[l])
    b_r = b_r.at[0, GROUP_LANE0:GROUP_LANE0 + N_GROUPS].set(b_group_router[l])
    return {
        "norm_mix": norm_mix[l][None, :], "w_in": w_in_p,
        "q_norm": q_norm[l][None, :], "w_q": wq.astype(BF16),
        "kv_norm": kv_norm[l][None, :], "w_k": wk.astype(BF16), "w_v": wv.astype(BF16),
        "sg_norm": sg_norm[l][None, :], "w_s": w_s[l],
        "b_s": jnp.broadcast_to(b_s[l][:, :, None], (SG_GROUPS, CHUNK, SG_GROUP_DIM)),
        "conv_w": conv_w[l],
        "w_a_out": w_a_out[l].astype(BF16), "w_b_out": w_b_out[l].astype(BF16),
        "w_c_out": w_c_out[l].astype(BF16), "w_o": w_o[l].astype(BF16),
        "norm_ffn": norm_ffn[l][None, :], "w_router": w_r.astype(BF16), "b_router": b_r,
        "w_gate_up": w_gate_up[l].astype(BF16), "w_down": w_down[l].astype(BF16),
    }


def _rope_tables(positions):
    half = QK_ROPE // 2
    inv = ROPE_BASE ** (-jnp.arange(0, QK_ROPE, 2, dtype=F32) / QK_ROPE)
    ang = positions.astype(F32).reshape(-1)[:, None] * inv
    cos, sin = jnp.cos(ang), jnp.sin(ang)
    t = ang.shape[0]
    ones = jnp.ones((t, QK_NOPE), F32)
    z16 = jnp.zeros((t, half), F32)
    z32 = jnp.zeros((t, HEAD_PAD - QK_HEAD), F32)
    z64 = jnp.zeros((t, QK_NOPE), F32)
    rc = jnp.concatenate([ones, cos, cos, z32], axis=1)
    rm = jnp.concatenate([z64, -sin, z16, z32], axis=1)
    rp = jnp.concatenate([z64, z16, sin, z32], axis=1)
    return rc, rm, rp


def kernel(x, positions, norm_mix, w_in, q_norm, w_q_up, kv_norm, w_kv_up, sg_norm, w_s, b_s, conv_w,
           w_a_out, w_b_out, w_c_out, w_o, norm_ffn, w_group_router, b_group_router, w_expert_router,
           b_expert_router, w_gate_up, w_down, final_norm):
    batch, seq, d = x.shape
    depth = w_in.shape[0]
    tm = min(512, seq)
    tq = tm
    tme = 512
    tmc = min(256, seq)
    rc, rm, rp = _rope_tables(positions)
    x2 = x.reshape(batch * seq, d)
    fn = final_norm[None, :]
    for l in range(depth):
        lw = _pack_layer(l, w_in, norm_mix, q_norm, w_q_up, kv_norm, w_kv_up, sg_norm, w_s, b_s, conv_w,
                         w_a_out, w_b_out, w_c_out, w_o, norm_ffn, w_group_router, b_group_router,
                         w_expert_router, b_expert_router, w_gate_up, w_down)
        q, k, v, part, ga = _mixer_pre(x2, lw, rc, rm, rp, seq=seq, tm=tm)
        o = _attention(q, k, v, batch=batch, seq=seq, tq=tq)
        x1, h2p, info, meta, cnt = _mixer_post(o, part, ga, x2, lw, tm=tm)
        x2 = _moe(x1, h2p, info, meta, cnt, lw, fn, tme=tme, tmc=tmc, final_norm=(l == depth - 1))
    return x2.reshape(batch, seq, d)
```

```python
import functools

import jax
import jax.numpy as jnp
from jax import lax
from jax.experimental import pallas as pl
from jax.experimental.pallas import tpu as pltpu

F32 = jnp.float32
BF16 = jnp.bfloat16

EPS = 1e-6
MLA_HEADS = 8
QK_NOPE = 64
QK_ROPE = 32
QK_HEAD = QK_NOPE + QK_ROPE
V_HEAD = 64
Q_RANK = 384
KV_RANK = 256
ROPE_BASE = 10000.0
SG_GROUPS = 4
SG_GROUP_DIM = 128
SG_WIDTH = SG_GROUPS * SG_GROUP_DIM
CHUNK = 128
CONV_WIDTH = 512
CONV_K = 3
N_GROUPS = 4
EXPERTS_PER_GROUP = 8
N_EXPERTS = N_GROUPS * EXPERTS_PER_GROUP
D_EXPERT = 256

LANES = 128
HEAD_PAD = 128
MASK_VALUE = -1e30
LOG2_E = 1.4426950408889634
VMEM_LIMIT = 56 * 1024 * 1024

_P1 = Q_RANK + KV_RANK + HEAD_PAD
_P2 = _P1 + 2 * SG_WIDTH
_P3 = _P2 + 3 * CONV_WIDTH
GROUP_LANE0 = N_EXPERTS


def _dot(a, b):
    return jnp.dot(a, b, preferred_element_type=F32)


def _rms(xf, g):
    return xf * lax.rsqrt(jnp.mean(xf * xf, axis=-1, keepdims=True) + EPS) * g


def _gelu(x):
    return 0.5 * x * (1.0 + jnp.tanh(0.7978845608028654 * (x + 0.044715 * (x * x * x))))


def _rope_t(tt, cos, sin):
    lo, mid = QK_NOPE, QK_NOPE + QK_ROPE // 2
    x1, x2 = tt[lo:mid], tt[mid:QK_HEAD]
    return jnp.concatenate([tt[:lo], x1 * cos - x2 * sin, x2 * cos + x1 * sin, tt[QK_HEAD:]], axis=0)


def _const_spec(shape):
    nd = len(shape)
    return pl.BlockSpec(shape, lambda *_: (0,) * nd, pipeline_mode=pl.Buffered(1))


def _mixer_pre_kernel(x_ref, nm_ref, win_ref, qn_ref, wq_ref, kvn_ref, wk_ref, wv_ref, sgn_ref,
                      ws_ref, bs_ref, cw_ref, wb_ref, wc_ref, cos_ref, sin_ref,
                      qt_ref, k_ref, vt_ref, part_ref, ga_ref,
                      halo_ref, ybin_ref, *, tiles_per_seq, tm):
    i = pl.program_id(0)
    d_model = x_ref.shape[1]
    h = _rms(x_ref[...], nm_ref[...]).astype(BF16)

    p1 = _dot(h, win_ref[:, 0:_P1])
    cos, sin = cos_ref[...], sin_ref[...]
    qn = _rms(p1[:, :Q_RANK], qn_ref[...]).astype(BF16)
    qf = _dot(qn, wq_ref[...])
    scale = QK_HEAD ** -0.5 * LOG2_E
    for hd in range(MLA_HEADS):
        sl = slice(hd * HEAD_PAD, (hd + 1) * HEAD_PAD)
        qt_ref[0, 0, sl, :] = (_rope_t(qf[:, sl].T, cos, sin) * scale).astype(BF16)
    kvn = _rms(p1[:, Q_RANK:Q_RANK + KV_RANK], kvn_ref[...]).astype(BF16)
    kr = _rope_t(p1[:, Q_RANK + KV_RANK:_P1].T, cos, sin).T
    kf = _dot(kvn, wk_ref[...])
    for hd in range(MLA_HEADS):
        sl = slice(hd * HEAD_PAD, (hd + 1) * HEAD_PAD)
        k_ref[:, sl] = (kf[:, sl] + kr).astype(BF16)
    vt_ref[0, 0] = _dot(kvn, wv_ref[...]).T.astype(BF16)

    p2 = _dot(h, win_ref[:, _P1:_P2])
    u = _gelu(p2[:, :SG_WIDTH])
    vb = _rms(_gelu(p2[:, SG_WIDTH:]), sgn_ref[...]).astype(BF16)
    r_i = lax.broadcasted_iota(jnp.int32, (CHUNK, CHUNK), 0)
    c_i = lax.broadcasted_iota(jnp.int32, (CHUNK, CHUNK), 1)
    for g in range(SG_GROUPS):
        wg = jnp.where(c_i <= r_i, ws_ref[g], 0.0).astype(BF16)
        gs = slice(g * SG_GROUP_DIM, (g + 1) * SG_GROUP_DIM)
        for c in range(tm // CHUNK):
            cs = slice(c * CHUNK, (c + 1) * CHUNK)
            mixed = _dot(wg, vb[cs, gs]) + bs_ref[g]
            ybin_ref[cs, gs] = (u[cs, gs] * mixed).astype(BF16)

    @pl.when(i % tiles_per_seq == 0)
    def _():
        halo_ref[...] = jnp.zeros_like(halo_ref)

    p3 = _dot(h, win_ref[:, _P2:_P3])
    z = p3[:, CONV_WIDTH:2 * CONV_WIDTH] * p3[:, 2 * CONV_WIDTH:]
    row = lax.broadcasted_iota(jnp.int32, z.shape, 0)
    halo = halo_ref[...]
    last1 = halo[7:8, :]
    last2 = halo[6:7, :]
    z1 = jnp.where(row == 0, last1, pltpu.roll(z, 1, 0))
    z2 = jnp.where(row == 0, last2, jnp.where(row == 1, last1, pltpu.roll(z, 2, 0)))
    y = cw_ref[0:1, :] * z2 + cw_ref[1:2, :] * z1 + cw_ref[2:3, :] * z
    yc_in = (p3[:, :CONV_WIDTH] * y).astype(BF16)
    halo_ref[...] = z[tm - 8:tm, :]

    p4 = _dot(h, win_ref[:, _P3:_P3 + 3 * d_model])
    yb = _dot(ybin_ref[...], wb_ref[...])
    yc = _dot(yc_in, wc_ref[...])
    part = (jax.nn.sigmoid(p4[:, d_model:2 * d_model]) * yb
            + jax.nn.sigmoid(p4[:, 2 * d_model:]) * yc)
    part_ref[...] = part.astype(BF16)
    ga_ref[...] = jax.nn.sigmoid(p4[:, :d_model]).astype(BF16)


def _mixer_pre(x2, lw, cos_t, sin_t, *, seq, tm):
    t, d = x2.shape
    row = lambda w: pl.BlockSpec((tm, w), lambda i: (i, 0))
    consts = [lw["norm_mix"], lw["w_in"], lw["q_norm"], lw["w_q"], lw["kv_norm"], lw["w_k"], lw["w_v"],
              lw["sg_norm"], lw["w_s"], lw["b_s"], lw["conv_w"], lw["w_b_out"], lw["w_c_out"]]
    tps = seq // tm
    kern = functools.partial(_mixer_pre_kernel, tiles_per_seq=tps, tm=tm)
    slab = lambda w: pl.BlockSpec((1, 1, w, tm), lambda i: (i // tps, i % tps, 0, 0))
    return pl.pallas_call(
        kern,
        grid=(t // tm,),
        in_specs=[row(d)] + [_const_spec(c.shape) for c in consts]
        + [pl.BlockSpec((QK_ROPE // 2, tm), lambda i: (0, i))] * 2,
        out_specs=[slab(MLA_HEADS * HEAD_PAD), row(MLA_HEADS * HEAD_PAD), slab(MLA_HEADS * V_HEAD), row(d), row(d)],
        out_shape=[jax.ShapeDtypeStruct((t // seq, tps, MLA_HEADS * HEAD_PAD, tm), BF16),
                   jax.ShapeDtypeStruct((t, MLA_HEADS * HEAD_PAD), BF16),
                   jax.ShapeDtypeStruct((t // seq, tps, MLA_HEADS * V_HEAD, tm), BF16),
                   jax.ShapeDtypeStruct((t, d), BF16),
                   jax.ShapeDtypeStruct((t, d), BF16)],
        scratch_shapes=[pltpu.VMEM((8, CONV_WIDTH), F32), pltpu.VMEM((tm, SG_WIDTH), BF16)],
        compiler_params=pltpu.CompilerParams(dimension_semantics=("arbitrary",), vmem_limit_bytes=VMEM_LIMIT),
        name="mixer_pre",
    )(x2, *consts, cos_t, sin_t)


ACC_ROWS = V_HEAD + 16


def _attn_kernel(qt_ref, k_ref, vt_ref, o_ref, m_ref, acc_ref, sa_ref, sb_ref, mca_ref, mcb_ref, *, tq, nq):
    bufs = ((sa_ref, mca_ref), (sb_ref, mcb_ref))

    def score(qi, j, buf, masked):
        s_ref, mc_ref = bufs[buf]
        start = pl.multiple_of(j * tq, tq)
        for sub in range(2):
            k_j = k_ref[pl.ds(start, tq), sub * HEAD_PAD:(sub + 1) * HEAD_PAD]
            st = _dot(k_j, qt_ref[0, qi, sub * HEAD_PAD:(sub + 1) * HEAD_PAD, :])
            if masked:
                kv_i = lax.broadcasted_iota(jnp.int32, st.shape, 0)
                q_i = lax.broadcasted_iota(jnp.int32, st.shape, 1)
                st = jnp.where(kv_i <= q_i, st, MASK_VALUE)
            s_ref[sub] = st
            mc_ref[sub] = jnp.broadcast_to(jnp.max(st, axis=0, keepdims=True), (8, tq))

    def value(j, buf):
        s_ref, mc_ref = bufs[buf]
        for sub in range(2):
            m_prev = m_ref[sub]
            m_next = jnp.maximum(m_prev, mc_ref[sub])
            alpha = jnp.exp2(m_prev - m_next)
            pt = jnp.exp2((s_ref[sub] - m_next[0:1, :]).astype(BF16))
            m_ref[sub] = m_next
            vt_j = jnp.concatenate([vt_ref[0, j, sub * V_HEAD:(sub + 1) * V_HEAD, :],
                                    jnp.ones((ACC_ROWS - V_HEAD, tq), BF16)], axis=0)
            acc_ref[sub] = alpha[0:1, :] * acc_ref[sub] + _dot(vt_j, pt)

    def q_tile(qi, carry):
        m_ref[...] = jnp.full_like(m_ref, MASK_VALUE)
        acc_ref[...] = jnp.zeros_like(acc_ref)

        @pl.when(qi == 0)
        def _():
            score(qi, 0, 0, True)
            value(0, 0)

        @pl.when(qi > 0)
        def _():
            score(qi, 0, 0, False)
            n_pairs = (qi - 1) // 2

            def body(p, c):
                score(qi, 2 * p + 1, 1, False)
                value(2 * p, 0)
                score(qi, 2 * p + 2, 0, False)
                value(2 * p + 1, 1)
                return c

            lax.fori_loop(0, n_pairs, body, 0)
            t0 = 2 * n_pairs

            @pl.when(qi % 2 == 1)
            def _():
                score(qi, qi, 1, True)
                value(t0, 0)
                value(qi, 1)

            @pl.when(qi % 2 == 0)
            def _():
                score(qi, t0 + 1, 1, False)
                value(t0, 0)
                score(qi, qi, 0, True)
                value(t0 + 1, 1)
                value(qi, 0)

        ot = jnp.concatenate([acc_ref[sub, 0:V_HEAD, :] / acc_ref[sub, V_HEAD:V_HEAD + 1, :]
                              for sub in range(2)], axis=0)
        o_ref[pl.ds(pl.multiple_of(qi * tq, tq), tq), :] = ot.T.astype(BF16)
        return carry

    lax.fori_loop(0, nq, q_tile, 0)


def _attention(qt, k, vt, *, batch, seq, tq):
    t = k.shape[0]
    nq = seq // tq
    pairs = MLA_HEADS // 2
    return pl.pallas_call(
        functools.partial(_attn_kernel, tq=tq, nq=nq),
        grid=(batch, pairs),
        in_specs=[pl.BlockSpec((1, nq, 2 * HEAD_PAD, tq), lambda b, hp: (b, 0, hp, 0)),
                  pl.BlockSpec((seq, 2 * HEAD_PAD), lambda b, hp: (b, hp)),
                  pl.BlockSpec((1, nq, 2 * V_HEAD, tq), lambda b, hp: (b, 0, hp, 0))],
        out_specs=pl.BlockSpec((seq, 2 * V_HEAD), lambda b, hp: (b, hp)),
        out_shape=jax.ShapeDtypeStruct((t, MLA_HEADS * V_HEAD), BF16),
        scratch_shapes=[pltpu.VMEM((2, 8, tq), F32), pltpu.VMEM((2, ACC_ROWS, tq), F32),
                        pltpu.VMEM((2, tq, tq), F32), pltpu.VMEM((2, tq, tq), F32),
                        pltpu.VMEM((2, 8, tq), F32), pltpu.VMEM((2, 8, tq), F32)],
        compiler_params=pltpu.CompilerParams(dimension_semantics=("arbitrary",) * 2, vmem_limit_bytes=VMEM_LIMIT),
        name="attention",
    )(qt, k, vt)


def _pack_pair(lo, hi):
    lo_b = lax.bitcast_convert_type(lo, jnp.uint32)
    hi_b = lax.bitcast_convert_type(hi, jnp.uint32)
    return (hi_b & jnp.uint32(0xFFFF0000)) | (lo_b >> 16)


def _unpack_pair(p):
    lo = lax.bitcast_convert_type(p << 16, F32)
    hi = lax.bitcast_convert_type(p & jnp.uint32(0xFFFF0000), F32)
    return lo, hi


def _route(logits):
    lane = lax.broadcasted_iota(jnp.int32, logits.shape, 1)
    lane_f = lane.astype(F32)
    big = float(LANES)
    is_grp = (lane >= GROUP_LANE0) & (lane < GROUP_LANE0 + N_GROUPS)
    g_log = jnp.where(is_grp, logits, MASK_VALUE)
    g_max = jnp.max(g_log, axis=1, keepdims=True)
    g_sum = jnp.sum(jnp.exp(g_log - g_max), axis=1, keepdims=True)
    g_p = 1.0 / g_sum
    g_idx = jnp.min(jnp.where(g_log == g_max, lane_f, big), axis=1, keepdims=True) - float(GROUP_LANE0)
    lo = g_idx * float(EXPERTS_PER_GROUP)
    in_grp = (lane_f >= lo) & (lane_f < lo + float(EXPERTS_PER_GROUP))
    e_log = jnp.where(in_grp, logits, MASK_VALUE)
    e_max = jnp.max(e_log, axis=1, keepdims=True)
    e_exp = jnp.exp(e_log - e_max)
    e_prob = e_exp / jnp.sum(e_exp, axis=1, keepdims=True)
    cand = jnp.where(in_grp, e_prob, -1.0)
    p1 = jnp.max(cand, axis=1, keepdims=True)
    i1 = jnp.min(jnp.where(cand == p1, lane_f, big), axis=1, keepdims=True)
    cand2 = jnp.where(lane_f == i1, -1.0, cand)
    p2 = jnp.max(cand2, axis=1, keepdims=True)
    i2 = jnp.min(jnp.where(cand2 == p2, lane_f, big), axis=1, keepdims=True)
    denom = p1 + p2
    return g_p * (p1 / denom), g_p * (p2 / denom), i1, i2


_L_W1, _L_W2 = 0, 1
_M_E1, _M_E2, _M_R1H, _M_R1L, _M_R2H, _M_R2L = range(6)


def _mixer_post_kernel(o_ref, part_ref, ga_ref, x_ref, wa_ref, wo_ref, nf_ref, wr_ref, br_ref,
                       x1_ref, h2p_ref, info_ref, meta_ref, cnt_ref, run_ref, *, tm):
    i = pl.program_id(0)
    half = x_ref.shape[1] // 2

    @pl.when(i == 0)
    def _():
        run_ref[...] = jnp.zeros_like(run_ref)

    ya = _dot(o_ref[...], wa_ref[...])
    merged = ga_ref[...].astype(F32) * ya + part_ref[...].astype(F32)
    x1 = x_ref[...] + _dot(merged.astype(BF16), wo_ref[...])
    x1_ref[...] = x1
    h2 = _rms(x1, nf_ref[...]).astype(BF16)
    h2f = h2.astype(F32)
    h2p_ref[...] = _pack_pair(h2f[:, :half], h2f[:, half:])
    w1, w2, i1, i2 = _route(_dot(h2, wr_ref[...]) + br_ref[...])

    lane_f = lax.broadcasted_iota(jnp.int32, (tm, LANES), 1).astype(F32)
    oh1 = lane_f == i1
    oh2 = lane_f == i2
    both = jnp.where(oh1, 1.0, 0.0) + jnp.where(oh2, 1.0, 0.0)
    r_i = lax.broadcasted_iota(jnp.int32, (tm, tm), 0)
    c_i = lax.broadcasted_iota(jnp.int32, (tm, tm), 1)
    earlier = jnp.where(c_i < r_i, 1.0, 0.0).astype(BF16)
    before = _dot(earlier, both.astype(BF16)) + run_ref[0:1, :]
    rank1 = jnp.sum(jnp.where(oh1, before, 0.0), axis=1, keepdims=True)
    rank2 = jnp.sum(jnp.where(oh2, before, 0.0), axis=1, keepdims=True)
    run_ref[0:1, :] = run_ref[0:1, :] + jnp.sum(both, axis=0, keepdims=True)
    cnt_ref[...] = run_ref[...]

    lane = lax.broadcasted_iota(jnp.int32, (tm, LANES), 1)
    info_ref[...] = jnp.where(lane == _L_W1, w1, jnp.where(lane == _L_W2, w2, 0.0))

    r1_hi = jnp.floor(rank1 * (1.0 / 256.0))
    r2_hi = jnp.floor(rank2 * (1.0 / 256.0))
    cols = (i1, i2, r1_hi, rank1 - 256.0 * r1_hi, r2_hi, rank2 - 256.0 * r2_hi)
    vals = jnp.zeros((tm, LANES), F32)
    for ln, val in enumerate(cols):
        vals = jnp.where(lane == ln, val, vals)
    s_r = lax.broadcasted_iota(jnp.int32, (8, LANES), 0)
    s_c = lax.broadcasted_iota(jnp.int32, (8, LANES), 1)
    sel = jnp.where((s_r == s_c) & (s_r < len(cols)), 1.0, 0.0).astype(BF16)
    meta_ref[0] = lax.dot_general(sel, vals.astype(BF16), (((1,), (1,)), ((), ())),
                                  preferred_element_type=F32)


def _mixer_post(o, part, ga, x2, lw, *, tm):
    t, d = x2.shape
    row = lambda w: pl.BlockSpec((tm, w), lambda i: (i, 0))
    consts = [lw["w_a_out"], lw["w_o"], lw["norm_ffn"], lw["w_router"], lw["b_router"]]
    return pl.pallas_call(
        functools.partial(_mixer_post_kernel, tm=tm),
        grid=(t // tm,),
        in_specs=[row(o.shape[1]), row(d), row(d), row(d)] + [_const_spec(c.shape) for c in consts],
        out_specs=[row(d), row(d // 2), row(LANES), pl.BlockSpec((1, 8, tm), lambda i: (i, 0, 0)),
                   pl.BlockSpec((8, LANES), lambda i: (0, 0))],
        out_shape=[jax.ShapeDtypeStruct((t, d), F32), jax.ShapeDtypeStruct((t, d // 2), jnp.uint32),
                   jax.ShapeDtypeStruct((t, LANES), F32), jax.ShapeDtypeStruct((t // tm, 8, tm), F32),
                   jax.ShapeDtypeStruct((8, LANES), F32)],
        scratch_shapes=[pltpu.VMEM((8, LANES), F32)],
        compiler_params=pltpu.CompilerParams(dimension_semantics=("arbitrary",), vmem_limit_bytes=VMEM_LIMIT),
        name="mixer_post",
    )(o, part, ga, x2, *consts)


def _dispatch_kernel(slots_ref, h2p_hbm, xs_in_hbm, xs_hbm, buf, lsem, rsem, *, td):
    del xs_in_hbm
    i = pl.program_id(0)
    n = pl.num_programs(0)

    def load(j, b):
        return pltpu.make_async_copy(h2p_hbm.at[pl.ds(j * td, td)], buf.at[b], lsem.at[b])

    @pl.when(i == 0)
    def _():
        load(0, 0).start()

    @pl.when(i + 1 < n)
    def _():
        load(i + 1, (i + 1) % 3).start()

    b = i % 3
    par = i % 2
    load(i, b).wait()
    for r in range(td):
        for k in range(2):
            dst = slots_ref[0, 0, k * td + r]
            pltpu.make_async_copy(buf.at[b, pl.ds(r, 1)], xs_hbm.at[pl.ds(dst, 1)], rsem.at[par]).start()

    def wait_rows(p):
        for _ in range(2):
            pltpu.make_async_copy(buf.at[0], xs_hbm.at[pl.ds(0, td)], rsem.at[p]).wait()

    @pl.when(i > 0)
    def _():
        wait_rows(1 - par)

    @pl.when(i == n - 1)
    def _():
        wait_rows(par)


def _dispatch(h2p, slots, n_slots, *, td):
    t, half = h2p.shape
    xs0 = jnp.zeros((n_slots, half), jnp.uint32)
    return pl.pallas_call(
        functools.partial(_dispatch_kernel, td=td),
        grid=(t // td,),
        in_specs=[pl.BlockSpec((1, 1, 2 * td), lambda i: (i, 0, 0), memory_space=pltpu.SMEM),
                  pl.BlockSpec(memory_space=pl.ANY),
                  pl.BlockSpec(memory_space=pl.ANY)],
        out_specs=pl.BlockSpec(memory_space=pl.ANY),
        out_shape=jax.ShapeDtypeStruct(xs0.shape, jnp.uint32),
        scratch_shapes=[pltpu.VMEM((3, td, half), jnp.uint32), pltpu.SemaphoreType.DMA((3,)),
                        pltpu.SemaphoreType.DMA((2,))],
        input_output_aliases={2: 0},
        compiler_params=pltpu.CompilerParams(dimension_semantics=("arbitrary",), vmem_limit_bytes=VMEM_LIMIT,
                                             has_side_effects=True),
        name="dispatch",
    )(slots, h2p, xs0)


def _expert_kernel(te_ref, nu_ref, xs_ref, wgu_ref, wdn_ref, ys_ref):
    i = pl.program_id(0)
    half = wgu_ref.shape[1] // 2

    @pl.when(i < nu_ref[0])
    def _():
        lo, hi = _unpack_pair(xs_ref[...])
        gu = _dot(lo.astype(BF16), wgu_ref[0, :half, :]) + _dot(hi.astype(BF16), wgu_ref[0, half:, :])
        act = (jax.nn.silu(gu[:, :D_EXPERT]) * gu[:, D_EXPERT:]).astype(BF16)
        y = _dot(act, wdn_ref[0]).astype(BF16).astype(F32)
        ys_ref[...] = _pack_pair(y[:, :half], y[:, half:])

    @pl.when(i >= nu_ref[0])
    def _():
        ys_ref[...] = jnp.zeros_like(ys_ref)


def _experts(xs, tile_expert, n_used, lw, *, tme):
    n_slots, half = xs.shape
    d = 2 * half
    grid_spec = pltpu.PrefetchScalarGridSpec(
        num_scalar_prefetch=2,
        grid=(n_slots // tme,),
        in_specs=[pl.BlockSpec((tme, half), lambda i, te, nu: (i, 0)),
                  pl.BlockSpec((1, d, 2 * D_EXPERT), lambda i, te, nu: (te[i], 0, 0)),
                  pl.BlockSpec((1, D_EXPERT, d), lambda i, te, nu: (te[i], 0, 0))],
        out_specs=pl.BlockSpec((tme, half), lambda i, te, nu: (i, 0)),
    )
    return pl.pallas_call(
        _expert_kernel,
        grid_spec=grid_spec,
        out_shape=jax.ShapeDtypeStruct((n_slots, half), jnp.uint32),
        compiler_params=pltpu.CompilerParams(dimension_semantics=("arbitrary",), vmem_limit_bytes=VMEM_LIMIT),
        name="experts",
    )(tile_expert, n_used, xs, lw["w_gate_up"], lw["w_down"])


def _combine_kernel(s_cur_ref, s_nxt_ref, x1_ref, info_ref, ys_hbm, fn_ref, out_ref, ybuf, sem,
                    *, tmc, final_norm):
    i = pl.program_id(0)
    n = pl.num_programs(0)
    half = x1_ref.shape[1] // 2

    def gather(idx_ref, slot):
        for r in range(2 * tmc):
            src = idx_ref[0, 0, r]
            pltpu.make_async_copy(ys_hbm.at[pl.ds(src, 1)], ybuf.at[slot, pl.ds(r, 1)], sem.at[slot]).start()

    @pl.when(i == 0)
    def _():
        gather(s_cur_ref, 0)

    @pl.when(i + 1 < n)
    def _():
        gather(s_nxt_ref, (i + 1) % 2)

    slot = i % 2
    pltpu.make_async_copy(ys_hbm.at[pl.ds(0, 2 * tmc)], ybuf.at[slot], sem.at[slot]).wait()
    lo1, hi1 = _unpack_pair(ybuf[slot, 0:tmc, :])
    lo2, hi2 = _unpack_pair(ybuf[slot, tmc:2 * tmc, :])
    info = info_ref[...]
    w1 = info[:, _L_W1:_L_W1 + 1]
    w2 = info[:, _L_W2:_L_W2 + 1]
    lo = x1_ref[:, :half] + w1 * lo1 + w2 * lo2
    hi = x1_ref[:, half:] + w1 * hi1 + w2 * hi2
    if final_norm:
        ms = (jnp.sum(lo * lo, axis=-1, keepdims=True) + jnp.sum(hi * hi, axis=-1, keepdims=True)) / (2 * half)
        inv = lax.rsqrt(ms + EPS)
        lo = lo * inv * fn_ref[:, :half]
        hi = hi * inv * fn_ref[:, half:]
    out_ref[:, :half] = lo
    out_ref[:, half:] = hi


def _combine(x1, info, ys, slots, fn, *, tmc, final_norm):
    t, d = x1.shape
    n = t // tmc
    return pl.pallas_call(
        functools.partial(_combine_kernel, tmc=tmc, final_norm=final_norm),
        grid=(n,),
        in_specs=[pl.BlockSpec((1, 1, 2 * tmc), lambda i: (i, 0, 0), memory_space=pltpu.SMEM),
                  pl.BlockSpec((1, 1, 2 * tmc), lambda i: (jnp.minimum(i + 1, n - 1), 0, 0),
                               memory_space=pltpu.SMEM),
                  pl.BlockSpec((tmc, d), lambda i: (i, 0)),
                  pl.BlockSpec((tmc, LANES), lambda i: (i, 0)),
                  pl.BlockSpec(memory_space=pl.ANY),
                  pl.BlockSpec((1, d), lambda i: (0, 0))],
        out_specs=pl.BlockSpec((tmc, d), lambda i: (i, 0)),
        out_shape=jax.ShapeDtypeStruct((t, d), F32),
        scratch_shapes=[pltpu.VMEM((2, 2 * tmc, d // 2), jnp.uint32), pltpu.SemaphoreType.DMA((2,))],
        compiler_params=pltpu.CompilerParams(dimension_semantics=("arbitrary",), vmem_limit_bytes=VMEM_LIMIT),
        name="combine",
    )(slots, slots, x1, info, ys, fn)


def _moe(x1, h2p, info, meta, cnt, lw, fn, *, tme, tmc, final_norm):
    t = x1.shape[0]
    n_tiles = (2 * t) // tme + N_EXPERTS
    counts = cnt[0, :N_EXPERTS].astype(jnp.int32)
    padded = ((counts + tme - 1) // tme) * tme
    ends = jnp.cumsum(padded)
    starts = ends - padded
    mi = meta.astype(jnp.int32)
    expert_ids = jnp.arange(N_EXPERTS, dtype=jnp.int32)

    def slot_of(e, r_hi, r_lo):
        start = jnp.sum(jnp.where(e[..., None] == expert_ids, starts, 0), axis=-1)
        return (start + r_hi * 256 + r_lo).reshape(t // tmc, 1, tmc)

    slot1 = slot_of(mi[:, _M_E1], mi[:, _M_R1H], mi[:, _M_R1L])
    slot2 = slot_of(mi[:, _M_E2], mi[:, _M_R2H], mi[:, _M_R2L])
    slots = jnp.concatenate([slot1, slot2], axis=2)
    tile_start = jnp.arange(n_tiles, dtype=jnp.int32) * tme
    tile_expert = jnp.minimum(jnp.sum((ends[None, :] <= tile_start[:, None]).astype(jnp.int32), axis=1),
                              N_EXPERTS - 1)
    n_used = ends[-1:] // tme
    xs = _dispatch(h2p, slots, n_tiles * tme, td=tmc)
    ys = _experts(xs, tile_expert, n_used, lw, tme=tme)
    return _combine(x1, info, ys, slots, fn, tmc=tmc, final_norm=final_norm)


def _pack_layer(l, w_in, norm_mix, q_norm, w_q_up, kv_norm, w_kv_up, sg_norm, w_s, b_s, conv_w,
                w_a_out, w_b_out, w_c_out, w_o, norm_ffn, w_group_router, b_group_router,
                w_expert_router, b_expert_router, w_gate_up, w_down):
    d = w_in.shape[1]
    wi = w_in[l]
    o_kr = Q_RANK + KV_RANK
    kr_slot = jnp.zeros((d, HEAD_PAD), F32).at[:, QK_NOPE:QK_HEAD].set(wi[:, o_kr:o_kr + QK_ROPE])
    w_in_p = jnp.concatenate([wi[:, :o_kr], kr_slot, wi[:, o_kr + QK_ROPE:]], axis=1).astype(BF16)
    wq = w_q_up[l].reshape(Q_RANK, MLA_HEADS, QK_HEAD)
    wq = jnp.pad(wq, ((0, 0), (0, 0), (0, HEAD_PAD - QK_HEAD))).reshape(Q_RANK, MLA_HEADS * HEAD_PAD)
    wkv = w_kv_up[l].reshape(KV_RANK, MLA_HEADS, QK_NOPE + V_HEAD)
    wk = jnp.pad(wkv[:, :, :QK_NOPE], ((0, 0), (0, 0), (0, HEAD_PAD - QK_NOPE)))
    wk = wk.reshape(KV_RANK, MLA_HEADS * HEAD_PAD)
    wv = wkv[:, :, QK_NOPE:].reshape(KV_RANK, MLA_HEADS * V_HEAD)
    w_r = jnp.zeros((d, LANES), F32)
    w_r = w_r.at[:, :N_EXPERTS].set(w_expert_router[l])
    w_r = w_r.at[:, GROUP_LANE0:GROUP_LANE0 + N_GROUPS].set(w_group_router[l])
    b_r = jnp.zeros((1, LANES), F32)
    b_r = b_r.at[0, :N_EXPERTS].set(b_expert_router[l])
    b_r = b_r.at[0, GROUP_LANE0:GROUP_LANE0 + N_GROUPS].set(b_group_router[l])
    return {
        "norm_mix": norm_mix[l][None, :], "w_in": w_in_p,
        "q_norm": q_norm[l][None, :], "w_q": wq.astype(BF16),
        "kv_norm": kv_norm[l][None, :], "w_k": wk.astype(BF16), "w_v": wv.astype(BF16),
        "sg_norm": sg_norm[l][None, :], "w_s": w_s[l],
        "b_s": jnp.broadcast_to(b_s[l][:, :, None], (SG_GROUPS, CHUNK, SG_GROUP_DIM)),
        "conv_w": conv_w[l],
        "w_a_out": w_a_out[l].astype(BF16), "w_b_out": w_b_out[l].astype(BF16),
        "w_c_out": w_c_out[l].astype(BF16), "w_o": w_o[l].astype(BF16),
        "norm_ffn": norm_ffn[l][None, :], "w_router": w_r.astype(BF16), "b_router": b_r,
        "w_gate_up": w_gate_up[l].astype(BF16), "w_down": w_down[l].astype(BF16),
    }


def _rope_tables(positions):
    inv = ROPE_BASE ** (-jnp.arange(0, QK_ROPE, 2, dtype=F32) / QK_ROPE)
    ang = inv[:, None] * positions.astype(F32).reshape(-1)[None, :]
    return jnp.cos(ang), jnp.sin(ang)


def kernel(x, positions, norm_mix, w_in, q_norm, w_q_up, kv_norm, w_kv_up, sg_norm, w_s, b_s, conv_w,
           w_a_out, w_b_out, w_c_out, w_o, norm_ffn, w_group_router, b_group_router, w_expert_router,
           b_expert_router, w_gate_up, w_down, final_norm):
    batch, seq, d = x.shape
    depth = w_in.shape[0]
    tm = min(512, seq)
    tq = tm
    tme = 512
    tmc = min(256, seq)
    cos_t, sin_t = _rope_tables(positions)
    x2 = x.reshape(batch * seq, d)
    fn = final_norm[None, :]
    for l in range(depth):
        lw = _pack_layer(l, w_in, norm_mix, q_norm, w_q_up, kv_norm, w_kv_up, sg_norm, w_s, b_s, conv_w,
                         w_a_out, w_b_out, w_c_out, w_o, norm_ffn, w_group_router, b_group_router,
                         w_expert_router, b_expert_router, w_gate_up, w_down)
        q, k, v, part, ga = _mixer_pre(x2, lw, cos_t, sin_t, seq=seq, tm=tm)
        o = _attention(q, k, v, batch=batch, seq=seq, tq=tq)
        x1, h2p, info, meta, cnt = _mixer_post(o, part, ga, x2, lw, tm=tm)
        x2 = _moe(x1, h2p, info, meta, cnt, lw, fn, tme=tme, tmc=tmc, final_norm=(l == depth - 1))
    return x2.reshape(batch, seq, d)
```

```python
import functools

import jax
import jax.numpy as jnp
from jax import lax
from jax.experimental import pallas as pl
from jax.experimental.pallas import tpu as pltpu

F32 = jnp.float32
BF16 = jnp.bfloat16

EPS = 1e-6
MLA_HEADS = 8
QK_NOPE = 64
QK_ROPE = 32
QK_HEAD = QK_NOPE + QK_ROPE
V_HEAD = 64
Q_RANK = 384
KV_RANK = 256
ROPE_BASE = 10000.0
SG_GROUPS = 4
SG_GROUP_DIM = 128
SG_WIDTH = SG_GROUPS * SG_GROUP_DIM
CHUNK = 128
CONV_WIDTH = 512
CONV_K = 3
N_GROUPS = 4
EXPERTS_PER_GROUP = 8
N_EXPERTS = N_GROUPS * EXPERTS_PER_GROUP
D_EXPERT = 256

LANES = 128
HEAD_PAD = 128
MASK_VALUE = -1e30
LOG2_E = 1.4426950408889634
VMEM_LIMIT = 56 * 1024 * 1024

_P1 = Q_RANK + KV_RANK + HEAD_PAD
_P2 = _P1 + 2 * SG_WIDTH
_P3 = _P2 + 3 * CONV_WIDTH
GROUP_LANE0 = N_EXPERTS


def _dot(a, b):
    return jnp.dot(a, b, preferred_element_type=F32)


def _rms(xf, g):
    return xf * lax.rsqrt(jnp.mean(xf * xf, axis=-1, keepdims=True) + EPS) * g


def _gelu(x):
    return 0.5 * x * (1.0 + jnp.tanh(0.7978845608028654 * (x + 0.044715 * (x * x * x))))


def _rope_t(tt, cos, sin):
    lo, mid = QK_NOPE, QK_NOPE + QK_ROPE // 2
    x1, x2 = tt[lo:mid], tt[mid:QK_HEAD]
    return jnp.concatenate([tt[:lo], x1 * cos - x2 * sin, x2 * cos + x1 * sin, tt[QK_HEAD:]], axis=0)


def _const_spec(shape):
    nd = len(shape)
    return pl.BlockSpec(shape, lambda *_: (0,) * nd, pipeline_mode=pl.Buffered(1))


def _mixer_pre_kernel(x_ref, nm_ref, win_ref, qn_ref, wq_ref, kvn_ref, wk_ref, wv_ref, sgn_ref,
                      ws_ref, bs_ref, cw_ref, wb_ref, wc_ref, cos_ref, sin_ref,
                      qt_ref, k_ref, vt_ref, part_ref, ga_ref,
                      halo_ref, ybin_ref, *, tiles_per_seq, tm):
    i = pl.program_id(0)
    d_model = x_ref.shape[1]
    h = _rms(x_ref[...], nm_ref[...]).astype(BF16)

    p1 = _dot(h, win_ref[:, 0:_P1])
    cos, sin = cos_ref[...], sin_ref[...]
    qn = _rms(p1[:, :Q_RANK], qn_ref[...]).astype(BF16)
    qf = _dot(qn, wq_ref[...])
    scale = QK_HEAD ** -0.5 * LOG2_E
    for hd in range(MLA_HEADS):
        sl = slice(hd * HEAD_PAD, (hd + 1) * HEAD_PAD)
        qt_ref[0, 0, sl, :] = (_rope_t(qf[:, sl].T, cos, sin) * scale).astype(BF16)
    kvn = _rms(p1[:, Q_RANK:Q_RANK + KV_RANK], kvn_ref[...]).astype(BF16)
    kr = _rope_t(p1[:, Q_RANK + KV_RANK:_P1].T, cos, sin).T
    kf = _dot(kvn, wk_ref[...])
    for hd in range(MLA_HEADS):
        sl = slice(hd * HEAD_PAD, (hd + 1) * HEAD_PAD)
        k_ref[:, sl] = (kf[:, sl] + kr).astype(BF16)
    vt_ref[0, 0] = _dot(kvn, wv_ref[...]).T.astype(BF16)

    p2 = _dot(h, win_ref[:, _P1:_P2])
    u = _gelu(p2[:, :SG_WIDTH])
    vb = _rms(_gelu(p2[:, SG_WIDTH:]), sgn_ref[...]).astype(BF16)
    r_i = lax.broadcasted_iota(jnp.int32, (CHUNK, CHUNK), 0)
    c_i = lax.broadcasted_iota(jnp.int32, (CHUNK, CHUNK), 1)
    for g in range(SG_GROUPS):
        wg = jnp.where(c_i <= r_i, ws_ref[g], 0.0).astype(BF16)
        gs = slice(g * SG_GROUP_DIM, (g + 1) * SG_GROUP_DIM)
        for c in range(tm // CHUNK):
            cs = slice(c * CHUNK, (c + 1) * CHUNK)
            mixed = _dot(wg, vb[cs, gs]) + bs_ref[g]
            ybin_ref[cs, gs] = (u[cs, gs] * mixed).astype(BF16)

    @pl.when(i % tiles_per_seq == 0)
    def _():
        halo_ref[...] = jnp.zeros_like(halo_ref)

    p3 = _dot(h, win_ref[:, _P2:_P3])
    z = p3[:, CONV_WIDTH:2 * CONV_WIDTH] * p3[:, 2 * CONV_WIDTH:]
    row = lax.broadcasted_iota(jnp.int32, z.shape, 0)
    halo = halo_ref[...]
    last1 = halo[7:8, :]
    last2 = halo[6:7, :]
    z1 = jnp.where(row == 0, last1, pltpu.roll(z, 1, 0))
    z2 = jnp.where(row == 0, last2, jnp.where(row == 1, last1, pltpu.roll(z, 2, 0)))
    y = cw_ref[0:1, :] * z2 + cw_ref[1:2, :] * z1 + cw_ref[2:3, :] * z
    yc_in = (p3[:, :CONV_WIDTH] * y).astype(BF16)
    halo_ref[...] = z[tm - 8:tm, :]

    p4 = _dot(h, win_ref[:, _P3:_P3 + 3 * d_model])
    yb = _dot(ybin_ref[...], wb_ref[...])
    yc = _dot(yc_in, wc_ref[...])
    part = (jax.nn.sigmoid(p4[:, d_model:2 * d_model]) * yb
            + jax.nn.sigmoid(p4[:, 2 * d_model:]) * yc)
    part_ref[...] = part.astype(BF16)
    ga_ref[...] = jax.nn.sigmoid(p4[:, :d_model]).astype(BF16)


def _mixer_pre(x2, lw, cos_t, sin_t, *, seq, tm):
    t, d = x2.shape
    row = lambda w: pl.BlockSpec((tm, w), lambda i: (i, 0))
    consts = [lw["norm_mix"], lw["w_in"], lw["q_norm"], lw["w_q"], lw["kv_norm"], lw["w_k"], lw["w_v"],
              lw["sg_norm"], lw["w_s"], lw["b_s"], lw["conv_w"], lw["w_b_out"], lw["w_c_out"]]
    tps = seq // tm
    kern = functools.partial(_mixer_pre_kernel, tiles_per_seq=tps, tm=tm)
    slab = lambda w: pl.BlockSpec((1, 1, w, tm), lambda i: (i // tps, i % tps, 0, 0))
    return pl.pallas_call(
        kern,
        grid=(t // tm,),
        in_specs=[row(d)] + [_const_spec(c.shape) for c in consts]
        + [pl.BlockSpec((QK_ROPE // 2, tm), lambda i: (0, i))] * 2,
        out_specs=[slab(MLA_HEADS * HEAD_PAD), row(MLA_HEADS * HEAD_PAD), slab(MLA_HEADS * V_HEAD), row(d), row(d)],
        out_shape=[jax.ShapeDtypeStruct((t // seq, tps, MLA_HEADS * HEAD_PAD, tm), BF16),
                   jax.ShapeDtypeStruct((t, MLA_HEADS * HEAD_PAD), BF16),
                   jax.ShapeDtypeStruct((t // seq, tps, MLA_HEADS * V_HEAD, tm), BF16),
                   jax.ShapeDtypeStruct((t, d), BF16),
                   jax.ShapeDtypeStruct((t, d), BF16)],
        scratch_shapes=[pltpu.VMEM((8, CONV_WIDTH), F32), pltpu.VMEM((tm, SG_WIDTH), BF16)],
        compiler_params=pltpu.CompilerParams(dimension_semantics=("arbitrary",), vmem_limit_bytes=VMEM_LIMIT),
        name="mixer_pre",
    )(x2, *consts, cos_t, sin_t)


ACC_ROWS = V_HEAD + 16


def _attn_kernel(qt_ref, k_ref, vt_ref, o_ref, m_ref, acc_ref, sa_ref, sb_ref, mca_ref, mcb_ref, *, tq, nq):
    bufs = ((sa_ref, mca_ref), (sb_ref, mcb_ref))

    def score(qi, j, buf, masked):
        s_ref, mc_ref = bufs[buf]
        start = pl.multiple_of(j * tq, tq)
        for sub in range(2):
            k_j = k_ref[pl.ds(start, tq), sub * HEAD_PAD:(sub + 1) * HEAD_PAD]
            st = _dot(k_j, qt_ref[0, qi, sub * HEAD_PAD:(sub + 1) * HEAD_PAD, :])
            if masked:
                kv_i = lax.broadcasted_iota(jnp.int32, st.shape, 0)
                q_i = lax.broadcasted_iota(jnp.int32, st.shape, 1)
                st = jnp.where(kv_i <= q_i, st, MASK_VALUE)
            s_ref[sub] = st
            mc_ref[sub] = jnp.broadcast_to(jnp.max(st, axis=0, keepdims=True), (8, tq))

    def value(j, buf):
        s_ref, mc_ref = bufs[buf]
        for sub in range(2):
            m_prev = m_ref[sub]
            m_next = jnp.maximum(m_prev, mc_ref[sub])
            alpha = jnp.exp2(m_prev - m_next)
            pt = jnp.exp2((s_ref[sub] - m_next[0:1, :]).astype(BF16))
            m_ref[sub] = m_next
            vt_j = jnp.concatenate([vt_ref[0, j, sub * V_HEAD:(sub + 1) * V_HEAD, :],
                                    jnp.ones((ACC_ROWS - V_HEAD, tq), BF16)], axis=0)
            acc_ref[sub] = alpha[0:1, :] * acc_ref[sub] + _dot(vt_j, pt)

    def q_tile(qi, carry):
        m_ref[...] = jnp.full_like(m_ref, MASK_VALUE)
        acc_ref[...] = jnp.zeros_like(acc_ref)

        @pl.when(qi == 0)
        def _():
            score(qi, 0, 0, True)
            value(0, 0)

        @pl.when(qi > 0)
        def _():
            score(qi, 0, 0, False)
            n_pairs = (qi - 1) // 2

            def body(p, c):
                score(qi, 2 * p + 1, 1, False)
                value(2 * p, 0)
                score(qi, 2 * p + 2, 0, False)
                value(2 * p + 1, 1)
                return c

            lax.fori_loop(0, n_pairs, body, 0)
            t0 = 2 * n_pairs

            @pl.when(qi % 2 == 1)
            def _():
                score(qi, qi, 1, True)
                value(t0, 0)
                value(qi, 1)

            @pl.when(qi % 2 == 0)
            def _():
                score(qi, t0 + 1, 1, False)
                value(t0, 0)
                score(qi, qi, 0, True)
                value(t0 + 1, 1)
                value(qi, 0)

        ot = jnp.concatenate([acc_ref[sub, 0:V_HEAD, :] / acc_ref[sub, V_HEAD:V_HEAD + 1, :]
                              for sub in range(2)], axis=0)
        o_ref[pl.ds(pl.multiple_of(qi * tq, tq), tq), :] = ot.T.astype(BF16)
        return carry

    lax.fori_loop(0, nq, q_tile, 0)


def _attention(qt, k, vt, *, batch, seq, tq):
    t = k.shape[0]
    nq = seq // tq
    pairs = MLA_HEADS // 2
    return pl.pallas_call(
        functools.partial(_attn_kernel, tq=tq, nq=nq),
        grid=(batch, pairs),
        in_specs=[pl.BlockSpec((1, nq, 2 * HEAD_PAD, tq), lambda b, hp: (b, 0, hp, 0)),
                  pl.BlockSpec((seq, 2 * HEAD_PAD), lambda b, hp: (b, hp)),
                  pl.BlockSpec((1, nq, 2 * V_HEAD, tq), lambda b, hp: (b, 0, hp, 0))],
        out_specs=pl.BlockSpec((seq, 2 * V_HEAD), lambda b, hp: (b, hp)),
        out_shape=jax.ShapeDtypeStruct((t, MLA_HEADS * V_HEAD), BF16),
        scratch_shapes=[pltpu.VMEM((2, 8, tq), F32), pltpu.VMEM((2, ACC_ROWS, tq), F32),
                        pltpu.VMEM((2, tq, tq), F32), pltpu.VMEM((2, tq, tq), F32),
                        pltpu.VMEM((2, 8, tq), F32), pltpu.VMEM((2, 8, tq), F32)],
        compiler_params=pltpu.CompilerParams(dimension_semantics=("arbitrary",) * 2, vmem_limit_bytes=VMEM_LIMIT),
        name="attention",
    )(qt, k, vt)


def _pack_pair(lo, hi):
    lo_b = lax.bitcast_convert_type(lo, jnp.uint32)
    hi_b = lax.bitcast_convert_type(hi, jnp.uint32)
    return (hi_b & jnp.uint32(0xFFFF0000)) | (lo_b >> 16)


def _unpack_pair(p):
    lo = lax.bitcast_convert_type(p << 16, F32)
    hi = lax.bitcast_convert_type(p & jnp.uint32(0xFFFF0000), F32)
    return lo, hi


def _route(logits):
    lane = lax.broadcasted_iota(jnp.int32, logits.shape, 1)
    lane_f = lane.astype(F32)
    big = float(LANES)
    is_grp = (lane >= GROUP_LANE0) & (lane < GROUP_LANE0 + N_GROUPS)
    g_log = jnp.where(is_grp, logits, MASK_VALUE)
    g_max = jnp.max(g_log, axis=1, keepdims=True)
    g_sum = jnp.sum(jnp.exp(g_log - g_max), axis=1, keepdims=True)
    g_p = 1.0 / g_sum
    g_idx = jnp.min(jnp.where(g_log == g_max, lane_f, big), axis=1, keepdims=True) - float(GROUP_LANE0)
    lo = g_idx * float(EXPERTS_PER_GROUP)
    in_grp = (lane_f >= lo) & (lane_f < lo + float(EXPERTS_PER_GROUP))
    e_log = jnp.where(in_grp, logits, MASK_VALUE)
    e_max = jnp.max(e_log, axis=1, keepdims=True)
    e_exp = jnp.exp(e_log - e_max)
    e_prob = e_exp / jnp.sum(e_exp, axis=1, keepdims=True)
    cand = jnp.where(in_grp, e_prob, -1.0)
    p1 = jnp.max(cand, axis=1, keepdims=True)
    i1 = jnp.min(jnp.where(cand == p1, lane_f, big), axis=1, keepdims=True)
    cand2 = jnp.where(lane_f == i1, -1.0, cand)
    p2 = jnp.max(cand2, axis=1, keepdims=True)
    i2 = jnp.min(jnp.where(cand2 == p2, lane_f, big), axis=1, keepdims=True)
    denom = p1 + p2
    return g_p * (p1 / denom), g_p * (p2 / denom), i1, i2


_L_W1, _L_W2 = 0, 1
_M_E1, _M_E2, _M_R1H, _M_R1L, _M_R2H, _M_R2L = range(6)


def _mixer_post_kernel(o_ref, part_ref, ga_ref, x_ref, wa_ref, wo_ref, nf_ref, wr_ref, br_ref,
                       x1_ref, h2p_ref, info_ref, meta_ref, cnt_ref, run_ref, *, tm):
    i = pl.program_id(0)
    half = x_ref.shape[1] // 2

    @pl.when(i == 0)
    def _():
        run_ref[...] = jnp.zeros_like(run_ref)

    ya = _dot(o_ref[...], wa_ref[...])
    merged = ga_ref[...].astype(F32) * ya + part_ref[...].astype(F32)
    x1 = x_ref[...] + _dot(merged.astype(BF16), wo_ref[...])
    x1_ref[...] = x1
    h2 = _rms(x1, nf_ref[...]).astype(BF16)
    h2f = h2.astype(F32)
    h2p_ref[...] = _pack_pair(h2f[:, :half], h2f[:, half:])
    w1, w2, i1, i2 = _route(_dot(h2, wr_ref[...]) + br_ref[...])

    lane_f = lax.broadcasted_iota(jnp.int32, (tm, LANES), 1).astype(F32)
    oh1 = lane_f == i1
    oh2 = lane_f == i2
    both = jnp.where(oh1, 1.0, 0.0) + jnp.where(oh2, 1.0, 0.0)
    r_i = lax.broadcasted_iota(jnp.int32, (tm, tm), 0)
    c_i = lax.broadcasted_iota(jnp.int32, (tm, tm), 1)
    earlier = jnp.where(c_i < r_i, 1.0, 0.0).astype(BF16)
    before = _dot(earlier, both.astype(BF16)) + run_ref[0:1, :]
    rank1 = jnp.sum(jnp.where(oh1, before, 0.0), axis=1, keepdims=True)
    rank2 = jnp.sum(jnp.where(oh2, before, 0.0), axis=1, keepdims=True)
    run_ref[0:1, :] = run_ref[0:1, :] + jnp.sum(both, axis=0, keepdims=True)
    cnt_ref[...] = run_ref[...]

    lane = lax.broadcasted_iota(jnp.int32, (tm, LANES), 1)
    info_ref[...] = jnp.where(lane == _L_W1, w1, jnp.where(lane == _L_W2, w2, 0.0))

    r1_hi = jnp.floor(rank1 * (1.0 / 256.0))
    r2_hi = jnp.floor(rank2 * (1.0 / 256.0))
    cols = (i1, i2, r1_hi, rank1 - 256.0 * r1_hi, r2_hi, rank2 - 256.0 * r2_hi)
    vals = jnp.zeros((tm, LANES), F32)
    for ln, val in enumerate(cols):
        vals = jnp.where(lane == ln, val, vals)
    s_r = lax.broadcasted_iota(jnp.int32, (8, LANES), 0)
    s_c = lax.broadcasted_iota(jnp.int32, (8, LANES), 1)
    sel = jnp.where((s_r == s_c) & (s_r < len(cols)), 1.0, 0.0).astype(BF16)
    meta_ref[0] = lax.dot_general(sel, vals.astype(BF16), (((1,), (1,)), ((), ())),
                                  preferred_element_type=F32)


def _mixer_post(o, part, ga, x2, lw, *, tm):
    t, d = x2.shape
    row = lambda w: pl.BlockSpec((tm, w), lambda i: (i, 0))
    consts = [lw["w_a_out"], lw["w_o"], lw["norm_ffn"], lw["w_router"], lw["b_router"]]
    return pl.pallas_call(
        functools.partial(_mixer_post_kernel, tm=tm),
        grid=(t // tm,),
        in_specs=[row(o.shape[1]), row(d), row(d), row(d)] + [_const_spec(c.shape) for c in consts],
        out_specs=[row(d), row(d // 2), row(LANES), pl.BlockSpec((1, 8, tm), lambda i: (i, 0, 0)),
                   pl.BlockSpec((8, LANES), lambda i: (0, 0))],
        out_shape=[jax.ShapeDtypeStruct((t, d), F32), jax.ShapeDtypeStruct((t, d // 2), jnp.uint32),
                   jax.ShapeDtypeStruct((t, LANES), F32), jax.ShapeDtypeStruct((t // tm, 8, tm), F32),
                   jax.ShapeDtypeStruct((8, LANES), F32)],
        scratch_shapes=[pltpu.VMEM((8, LANES), F32)],
        compiler_params=pltpu.CompilerParams(dimension_semantics=("arbitrary",), vmem_limit_bytes=VMEM_LIMIT),
        name="mixer_post",
    )(o, part, ga, x2, *consts)


DISPATCH_RING = 4


def _dispatch_kernel(slots_ref, h2p_hbm, xs_in_hbm, xs_hbm, buf, lsem, rsem, *, td):
    del xs_in_hbm
    i = pl.program_id(0)
    n = pl.num_programs(0)

    def load(tile, b):
        return pltpu.make_async_copy(h2p_hbm.at[pl.ds(tile * td, td)], buf.at[b], lsem.at[b])

    def wait_rows(b):
        for _ in range(2):
            pltpu.make_async_copy(buf.at[b], xs_hbm.at[pl.ds(0, td)], rsem.at[b]).wait()

    @pl.when(i == 0)
    def _():
        load(0, 0).start()
        load(1, 1).start()

    for k in range(DISPATCH_RING):
        tile = DISPATCH_RING * i + k
        load(tile, k).wait()
        for r in range(td):
            for a in range(2):
                dst = slots_ref[0, 0, (2 * k + a) * td + r]
                pltpu.make_async_copy(buf.at[k, pl.ds(r, 1)], xs_hbm.at[pl.ds(dst, 1)], rsem.at[k]).start()
        if k > 0:
            wait_rows(k - 1)
        else:
            @pl.when(i > 0)
            def _():
                wait_rows(DISPATCH_RING - 1)
        if k + 2 < DISPATCH_RING:
            load(tile + 2, k + 2).start()
        else:
            @pl.when(i + 1 < n)
            def _():
                load(tile + 2, (k + 2) % DISPATCH_RING).start()

    @pl.when(i == n - 1)
    def _():
        wait_rows(DISPATCH_RING - 1)


def _dispatch(h2p, slots, xs_init, *, td):
    t, half = h2p.shape
    steps = t // (DISPATCH_RING * td)
    return pl.pallas_call(
        functools.partial(_dispatch_kernel, td=td),
        grid=(steps,),
        in_specs=[pl.BlockSpec((1, 1, DISPATCH_RING * 2 * td), lambda i: (i, 0, 0), memory_space=pltpu.SMEM),
                  pl.BlockSpec(memory_space=pl.ANY),
                  pl.BlockSpec(memory_space=pl.ANY)],
        out_specs=pl.BlockSpec(memory_space=pl.ANY),
        out_shape=jax.ShapeDtypeStruct(xs_init.shape, jnp.uint32),
        scratch_shapes=[pltpu.VMEM((DISPATCH_RING, td, half), jnp.uint32),
                        pltpu.SemaphoreType.DMA((DISPATCH_RING,)), pltpu.SemaphoreType.DMA((DISPATCH_RING,))],
        input_output_aliases={2: 0},
        compiler_params=pltpu.CompilerParams(dimension_semantics=("arbitrary",), vmem_limit_bytes=VMEM_LIMIT,
                                             has_side_effects=True),
        name="dispatch",
    )(slots.reshape(steps, 1, DISPATCH_RING * 2 * td), h2p, xs_init)


def _expert_kernel(te_ref, nu_ref, xs_ref, wgu_ref, wdn_ref, ys_ref):
    i = pl.program_id(0)
    half = wgu_ref.shape[1] // 2

    @pl.when(i < nu_ref[0])
    def _():
        lo, hi = _unpack_pair(xs_ref[...])
        gu = _dot(lo.astype(BF16), wgu_ref[0, :half, :]) + _dot(hi.astype(BF16), wgu_ref[0, half:, :])
        act = (jax.nn.silu(gu[:, :D_EXPERT]) * gu[:, D_EXPERT:]).astype(BF16)
        y = _dot(act, wdn_ref[0]).astype(BF16).astype(F32)
        ys_ref[...] = _pack_pair(y[:, :half], y[:, half:])

    @pl.when(i >= nu_ref[0])
    def _():
        ys_ref[...] = jnp.zeros_like(ys_ref)


def _experts(xs, tile_expert, n_used, lw, *, tme):
    n_slots, half = xs.shape
    d = 2 * half
    grid_spec = pltpu.PrefetchScalarGridSpec(
        num_scalar_prefetch=2,
        grid=(n_slots // tme,),
        in_specs=[pl.BlockSpec((tme, half), lambda i, te, nu: (i, 0)),
                  pl.BlockSpec((1, d, 2 * D_EXPERT), lambda i, te, nu: (te[i], 0, 0)),
                  pl.BlockSpec((1, D_EXPERT, d), lambda i, te, nu: (te[i], 0, 0))],
        out_specs=pl.BlockSpec((tme, half), lambda i, te, nu: (i, 0)),
    )
    return pl.pallas_call(
        _expert_kernel,
        grid_spec=grid_spec,
        out_shape=jax.ShapeDtypeStruct((n_slots, half), jnp.uint32),
        compiler_params=pltpu.CompilerParams(dimension_semantics=("arbitrary",), vmem_limit_bytes=VMEM_LIMIT),
        name="experts",
    )(tile_expert, n_used, xs, lw["w_gate_up"], lw["w_down"])


def _combine_kernel(s0_ref, s1_ref, s2_ref, x1_ref, info_ref, ys_hbm, fn_ref, out_ref, ya, yb, sem, *, tmc):
    i = pl.program_id(0)
    n = pl.num_programs(0)
    half = x1_ref.shape[1] // 2

    def gather(idx_ref, buf, k):
        for r in range(2 * tmc):
            src = idx_ref[0, 0, r]
            pltpu.make_async_copy(ys_hbm.at[pl.ds(src, 1)], buf.at[pl.ds(r, 1)], sem.at[k]).start()

    def finish(buf, k, rows):
        pltpu.make_async_copy(ys_hbm.at[pl.ds(0, 2 * tmc)], buf, sem.at[k]).wait()
        lo1, hi1 = _unpack_pair(buf[0:tmc, :])
        lo2, hi2 = _unpack_pair(buf[tmc:2 * tmc, :])
        info = info_ref[rows, :]
        w1 = info[:, _L_W1:_L_W1 + 1]
        w2 = info[:, _L_W2:_L_W2 + 1]
        lo = x1_ref[rows, :half] + w1 * lo1 + w2 * lo2
        hi = x1_ref[rows, half:] + w1 * hi1 + w2 * hi2
        if fn_ref is not None:
            ms = (jnp.sum(lo * lo, axis=-1, keepdims=True) + jnp.sum(hi * hi, axis=-1, keepdims=True)) / (2 * half)
            inv = lax.rsqrt(ms + EPS)
            lo = lo * inv * fn_ref[:, :half]
            hi = hi * inv * fn_ref[:, half:]
        out_ref[rows, :half] = lo
        out_ref[rows, half:] = hi

    @pl.when(i == 0)
    def _():
        gather(s0_ref, ya, 0)

    gather(s1_ref, yb, 1)
    finish(ya, 0, slice(0, tmc))

    @pl.when(i + 1 < n)
    def _():
        gather(s2_ref, ya, 0)

    finish(yb, 1, slice(tmc, 2 * tmc))


def _combine_plain_kernel(s0_ref, s1_ref, s2_ref, x1_ref, info_ref, ys_hbm, out_ref, ya, yb, sem, *, tmc):
    _combine_kernel(s0_ref, s1_ref, s2_ref, x1_ref, info_ref, ys_hbm, None, out_ref, ya, yb, sem, tmc=tmc)


def _combine(x1, info, ys, slots, fn, *, tmc, final_norm):
    t, d = x1.shape
    n_tiles = t // tmc
    n = n_tiles // 2
    smem = lambda f: pl.BlockSpec((1, 1, 2 * tmc), f, memory_space=pltpu.SMEM)
    in_specs = [smem(lambda i: (2 * i, 0, 0)), smem(lambda i: (2 * i + 1, 0, 0)),
                smem(lambda i: (jnp.minimum(2 * i + 2, n_tiles - 1), 0, 0)),
                pl.BlockSpec((2 * tmc, d), lambda i: (i, 0)),
                pl.BlockSpec((2 * tmc, LANES), lambda i: (i, 0)),
                pl.BlockSpec(memory_space=pl.ANY)]
    args = [slots, slots, slots, x1, info, ys]
    if final_norm:
        in_specs.append(pl.BlockSpec((1, d), lambda i: (0, 0)))
        args.append(fn)
    return pl.pallas_call(
        functools.partial(_combine_kernel if final_norm else _combine_plain_kernel, tmc=tmc),
        grid=(n,),
        in_specs=in_specs,
        out_specs=pl.BlockSpec((2 * tmc, d), lambda i: (i, 0)),
        out_shape=jax.ShapeDtypeStruct((t, d), F32),
        scratch_shapes=[pltpu.VMEM((2 * tmc, d // 2), jnp.uint32), pltpu.VMEM((2 * tmc, d // 2), jnp.uint32),
                        pltpu.SemaphoreType.DMA((2,))],
        compiler_params=pltpu.CompilerParams(dimension_semantics=("arbitrary",), vmem_limit_bytes=VMEM_LIMIT),
        name="combine",
    )(*args)


def _moe(x1, h2p, info, meta, cnt, lw, fn, xs_init, *, tme, tmc, final_norm):
    t = x1.shape[0]
    n_tiles = (2 * t) // tme + N_EXPERTS
    counts = cnt[0, :N_EXPERTS].astype(jnp.int32)
    padded = ((counts + tme - 1) // tme) * tme
    ends = jnp.cumsum(padded)
    starts = ends - padded
    mi = meta.astype(jnp.int32)
    expert_ids = jnp.arange(N_EXPERTS, dtype=jnp.int32)

    def slot_of(e, r_hi, r_lo):
        start = jnp.sum(jnp.where(e[..., None] == expert_ids, starts, 0), axis=-1)
        return (start + r_hi * 256 + r_lo).reshape(t // tmc, 1, tmc)

    slot1 = slot_of(mi[:, _M_E1], mi[:, _M_R1H], mi[:, _M_R1L])
    slot2 = slot_of(mi[:, _M_E2], mi[:, _M_R2H], mi[:, _M_R2L])
    slots = jnp.concatenate([slot1, slot2], axis=2)
    tile_start = jnp.arange(n_tiles, dtype=jnp.int32) * tme
    tile_expert = jnp.minimum(jnp.sum((ends[None, :] <= tile_start[:, None]).astype(jnp.int32), axis=1),
                              N_EXPERTS - 1)
    n_used = ends[-1:] // tme
    xs = _dispatch(h2p, slots, xs_init, td=tmc)
    ys = _experts(xs, tile_expert, n_used, lw, tme=tme)
    return _combine(x1, info, ys, slots, fn, tmc=tmc, final_norm=final_norm), xs


def _pack_layer(l, w_in, norm_mix, q_norm, w_q_up, kv_norm, w_kv_up, sg_norm, w_s, b_s, conv_w,
                w_a_out, w_b_out, w_c_out, w_o, norm_ffn, w_group_router, b_group_router,
                w_expert_router, b_expert_router, w_gate_up, w_down):
    d = w_in.shape[1]
    wi = w_in[l]
    o_kr = Q_RANK + KV_RANK
    kr_slot = jnp.zeros((d, HEAD_PAD), F32).at[:, QK_NOPE:QK_HEAD].set(wi[:, o_kr:o_kr + QK_ROPE])
    w_in_p = jnp.concatenate([wi[:, :o_kr], kr_slot, wi[:, o_kr + QK_ROPE:]], axis=1).astype(BF16)
    wq = w_q_up[l].reshape(Q_RANK, MLA_HEADS, QK_HEAD)
    wq = jnp.pad(wq, ((0, 0), (0, 0), (0, HEAD_PAD - QK_HEAD))).reshape(Q_RANK, MLA_HEADS * HEAD_PAD)
    wkv = w_kv_up[l].reshape(KV_RANK, MLA_HEADS, QK_NOPE + V_HEAD)
    wk = jnp.pad(wkv[:, :, :QK_NOPE], ((0, 0), (0, 0), (0, HEAD_PAD - QK_NOPE)))
    wk = wk.reshape(KV_RANK, MLA_HEADS * HEAD_PAD)
    wv = wkv[:, :, QK_NOPE:].reshape(KV_RANK, MLA_HEADS * V_HEAD)
    w_r = jnp.zeros((d, LANES), F32)
    w_r = w_r.at[:, :N_EXPERTS].set(w_expert_router[l])
    w_r = w_r.at[:, GROUP_LANE0:GROUP_LANE0 + N_GROUPS].set(w_group_router[l])
    b_r = jnp.zeros((1, LANES), F32)
    b_r = b_r.at[0, :N_EXPERTS].set(b_expert_router[l])
    b_r = b_r.at[0, GROUP_LANE0:GROUP_LANE0 + N_GROUPS].set(b_group_router[l])
    return {
        "norm_mix": norm_mix[l][None, :], "w_in": w_in_p,
        "q_norm": q_norm[l][None, :], "w_q": wq.astype(BF16),
        "kv_norm": kv_norm[l][None, :], "w_k": wk.astype(BF16), "w_v": wv.astype(BF16),
        "sg_norm": sg_norm[l][None, :], "w_s": w_s[l],
        "b_s": jnp.broadcast_to(b_s[l][:, :, None], (SG_GROUPS, CHUNK, SG_GROUP_DIM)),
        "conv_w": conv_w[l],
        "w_a_out": w_a_out[l].astype(BF16), "w_b_out": w_b_out[l].astype(BF16),
        "w_c_out": w_c_out[l].astype(BF16), "w_o": w_o[l].astype(BF16),
        "norm_ffn": norm_ffn[l][None, :], "w_router": w_r.astype(BF16), "b_router": b_r,
        "w_gate_up": w_gate_up[l].astype(BF16), "w_down": w_down[l].astype(BF16),
    }


def _rope_tables(positions):
    inv = ROPE_BASE ** (-jnp.arange(0, QK_ROPE, 2, dtype=F32) / QK_ROPE)
    ang = inv[:, None] * positions.astype(F32).reshape(-1)[None, :]
    return jnp.cos(ang), jnp.sin(ang)


def kernel(x, positions, norm_mix, w_in, q_norm, w_q_up, kv_norm, w_kv_up, sg_norm, w_s, b_s, conv_w,
           w_a_out, w_b_out, w_c_out, w_o, norm_ffn, w_group_router, b_group_router, w_expert_router,
           b_expert_router, w_gate_up, w_down, final_norm):
    batch, seq, d = x.shape
    depth = w_in.shape[0]
    tm = min(512, seq)
    tq = tm
    tme = 512
    tmc = min(256, seq)
    cos_t, sin_t = _rope_tables(positions)
    x2 = x.reshape(batch * seq, d)
    fn = final_norm[None, :]
    xs = jnp.zeros((2 * batch * seq + N_EXPERTS * tme, d // 2), jnp.uint32)
    for l in range(depth):
        lw = _pack_layer(l, w_in, norm_mix, q_norm, w_q_up, kv_norm, w_kv_up, sg_norm, w_s, b_s, conv_w,
                         w_a_out, w_b_out, w_c_out, w_o, norm_ffn, w_group_router, b_group_router,
                         w_expert_router, b_expert_router, w_gate_up, w_down)
        q, k, v, part, ga = _mixer_pre(x2, lw, cos_t, sin_t, seq=seq, tm=tm)
        o = _attention(q, k, v, batch=batch, seq=seq, tq=tq)
        x1, h2p, info, meta, cnt = _mixer_post(o, part, ga, x2, lw, tm=tm)
        x2, xs = _moe(x1, h2p, info, meta, cnt, lw, fn, xs, tme=tme, tmc=tmc, final_norm=(l == depth - 1))
    return x2.reshape(batch, seq, d)
```

```python
import functools

import jax
import jax.numpy as jnp
from jax import lax
from jax.experimental import pallas as pl
from jax.experimental.pallas import tpu as pltpu

F32 = jnp.float32
BF16 = jnp.bfloat16

EPS = 1e-6
MLA_HEADS = 8
QK_NOPE = 64
QK_ROPE = 32
QK_HEAD = QK_NOPE + QK_ROPE
V_HEAD = 64
Q_RANK = 384
KV_RANK = 256
ROPE_BASE = 10000.0
SG_GROUPS = 4
SG_GROUP_DIM = 128
SG_WIDTH = SG_GROUPS * SG_GROUP_DIM
CHUNK = 128
CONV_WIDTH = 512
CONV_K = 3
N_GROUPS = 4
EXPERTS_PER_GROUP = 8
N_EXPERTS = N_GROUPS * EXPERTS_PER_GROUP
D_EXPERT = 256

LANES = 128
HEAD_PAD = 128
MASK_VALUE = -1e30
LOG2_E = 1.4426950408889634
VMEM_LIMIT = 56 * 1024 * 1024

_P1 = Q_RANK + KV_RANK + HEAD_PAD
_P2 = _P1 + 2 * SG_WIDTH
_P3 = _P2 + 3 * CONV_WIDTH
GROUP_LANE0 = N_EXPERTS


def _dot(a, b):
    return jnp.dot(a, b, preferred_element_type=F32)


def _rms(xf, g):
    return xf * lax.rsqrt(jnp.mean(xf * xf, axis=-1, keepdims=True) + EPS) * g


def _gelu(x):
    return 0.5 * x * (1.0 + jnp.tanh(0.7978845608028654 * (x + 0.044715 * (x * x * x))))


def _rope_t(tt, cos, sin):
    lo, mid = QK_NOPE, QK_NOPE + QK_ROPE // 2
    x1, x2 = tt[lo:mid], tt[mid:QK_HEAD]
    return jnp.concatenate([tt[:lo], x1 * cos - x2 * sin, x2 * cos + x1 * sin, tt[QK_HEAD:]], axis=0)


def _const_spec(shape):
    nd = len(shape)
    return pl.BlockSpec(shape, lambda *_: (0,) * nd, pipeline_mode=pl.Buffered(1))


def _mixer_pre_kernel(x_ref, nm_ref, win_ref, qn_ref, wq_ref, kvn_ref, wk_ref, wv_ref, sgn_ref,
                      ws_ref, bs_ref, cw_ref, wb_ref, wc_ref, cos_ref, sin_ref,
                      qt_ref, k_ref, vt_ref, part_ref, ga_ref,
                      halo_ref, ybin_ref, *, tiles_per_seq, tm):
    i = pl.program_id(0)
    d_model = x_ref.shape[1]
    h = _rms(x_ref[...], nm_ref[...]).astype(BF16)

    p1 = _dot(h, win_ref[:, 0:_P1])
    cos, sin = cos_ref[...], sin_ref[...]
    qn = _rms(p1[:, :Q_RANK], qn_ref[...]).astype(BF16)
    qf = _dot(qn, wq_ref[...])
    scale = QK_HEAD ** -0.5 * LOG2_E
    for hd in range(MLA_HEADS):
        sl = slice(hd * HEAD_PAD, (hd + 1) * HEAD_PAD)
        qt_ref[0, 0, sl, :] = (_rope_t(qf[:, sl].T, cos, sin) * scale).astype(BF16)
    kvn = _rms(p1[:, Q_RANK:Q_RANK + KV_RANK], kvn_ref[...]).astype(BF16)
    kr = _rope_t(p1[:, Q_RANK + KV_RANK:_P1].T, cos, sin).T
    kf = _dot(kvn, wk_ref[...])
    for hd in range(MLA_HEADS):
        sl = slice(hd * HEAD_PAD, (hd + 1) * HEAD_PAD)
        k_ref[:, sl] = (kf[:, sl] + kr).astype(BF16)
    vt_ref[0, 0] = _dot(kvn, wv_ref[...]).T.astype(BF16)

    p2 = _dot(h, win_ref[:, _P1:_P2])
    u = _gelu(p2[:, :SG_WIDTH])
    vb = _rms(_gelu(p2[:, SG_WIDTH:]), sgn_ref[...]).astype(BF16)
    r_i = lax.broadcasted_iota(jnp.int32, (CHUNK, CHUNK), 0)
    c_i = lax.broadcasted_iota(jnp.int32, (CHUNK, CHUNK), 1)
    for g in range(SG_GROUPS):
        wg = jnp.where(c_i <= r_i, ws_ref[g], 0.0).astype(BF16)
        gs = slice(g * SG_GROUP_DIM, (g + 1) * SG_GROUP_DIM)
        for c in range(tm // CHUNK):
            cs = slice(c * CHUNK, (c + 1) * CHUNK)
            mixed = _dot(wg, vb[cs, gs]) + bs_ref[g]
            ybin_ref[cs, gs] = (u[cs, gs] * mixed).astype(BF16)

    @pl.when(i % tiles_per_seq == 0)
    def _():
        halo_ref[...] = jnp.zeros_like(halo_ref)

    p3 = _dot(h, win_ref[:, _P2:_P3])
    z = p3[:, CONV_WIDTH:2 * CONV_WIDTH] * p3[:, 2 * CONV_WIDTH:]
    row = lax.broadcasted_iota(jnp.int32, z.shape, 0)
    halo = halo_ref[...]
    last1 = halo[7:8, :]
    last2 = halo[6:7, :]
    z1 = jnp.where(row == 0, last1, pltpu.roll(z, 1, 0))
    z2 = jnp.where(row == 0, last2, jnp.where(row == 1, last1, pltpu.roll(z, 2, 0)))
    y = cw_ref[0:1, :] * z2 + cw_ref[1:2, :] * z1 + cw_ref[2:3, :] * z
    yc_in = (p3[:, :CONV_WIDTH] * y).astype(BF16)
    halo_ref[...] = z[tm - 8:tm, :]

    p4 = _dot(h, win_ref[:, _P3:_P3 + 3 * d_model])
    yb = _dot(ybin_ref[...], wb_ref[...])
    yc = _dot(yc_in, wc_ref[...])
    part = (jax.nn.sigmoid(p4[:, d_model:2 * d_model]) * yb
            + jax.nn.sigmoid(p4[:, 2 * d_model:]) * yc)
    part_ref[...] = part.astype(BF16)
    ga_ref[...] = jax.nn.sigmoid(p4[:, :d_model]).astype(BF16)


def _mixer_pre(x2, lw, cos_t, sin_t, *, seq, tm):
    t, d = x2.shape
    row = lambda w: pl.BlockSpec((tm, w), lambda i: (i, 0))
    consts = [lw["norm_mix"], lw["w_in"], lw["q_norm"], lw["w_q"], lw["kv_norm"], lw["w_k"], lw["w_v"],
              lw["sg_norm"], lw["w_s"], lw["b_s"], lw["conv_w"], lw["w_b_out"], lw["w_c_out"]]
    tps = seq // tm
    kern = functools.partial(_mixer_pre_kernel, tiles_per_seq=tps, tm=tm)
    slab = lambda w: pl.BlockSpec((1, 1, w, tm), lambda i: (i // tps, i % tps, 0, 0))
    return pl.pallas_call(
        kern,
        grid=(t // tm,),
        in_specs=[row(d)] + [_const_spec(c.shape) for c in consts]
        + [pl.BlockSpec((QK_ROPE // 2, tm), lambda i: (0, i))] * 2,
        out_specs=[slab(MLA_HEADS * HEAD_PAD), row(MLA_HEADS * HEAD_PAD), slab(MLA_HEADS * V_HEAD), row(d), row(d)],
        out_shape=[jax.ShapeDtypeStruct((t // seq, tps, MLA_HEADS * HEAD_PAD, tm), BF16),
                   jax.ShapeDtypeStruct((t, MLA_HEADS * HEAD_PAD), BF16),
                   jax.ShapeDtypeStruct((t // seq, tps, MLA_HEADS * V_HEAD, tm), BF16),
                   jax.ShapeDtypeStruct((t, d), BF16),
                   jax.ShapeDtypeStruct((t, d), BF16)],
        scratch_shapes=[pltpu.VMEM((8, CONV_WIDTH), F32), pltpu.VMEM((tm, SG_WIDTH), BF16)],
        compiler_params=pltpu.CompilerParams(dimension_semantics=("arbitrary",), vmem_limit_bytes=VMEM_LIMIT),
        name="mixer_pre",
    )(x2, *consts, cos_t, sin_t)


ACC_ROWS = V_HEAD + 16


def _attn_kernel(qt_ref, k_ref, vt_ref, o_ref, m_ref, acc_ref, sa_ref, sb_ref, mca_ref, mcb_ref, *, tq, nq):
    bufs = ((sa_ref, mca_ref), (sb_ref, mcb_ref))

    def score(qi, j, buf, masked):
        s_ref, mc_ref = bufs[buf]
        start = pl.multiple_of(j * tq, tq)
        for sub in range(2):
            k_j = k_ref[pl.ds(start, tq), sub * HEAD_PAD:(sub + 1) * HEAD_PAD]
            st = _dot(k_j, qt_ref[0, qi, sub * HEAD_PAD:(sub + 1) * HEAD_PAD, :])
            if masked:
                kv_i = lax.broadcasted_iota(jnp.int32, st.shape, 0)
                q_i = lax.broadcasted_iota(jnp.int32, st.shape, 1)
                st = jnp.where(kv_i <= q_i, st, MASK_VALUE)
            s_ref[sub] = st
            mc_ref[sub] = jnp.broadcast_to(jnp.max(st, axis=0, keepdims=True), (8, tq))

    def value(j, buf):
        s_ref, mc_ref = bufs[buf]
        for sub in range(2):
            m_prev = m_ref[sub]
            m_next = jnp.maximum(m_prev, mc_ref[sub])
            alpha = jnp.exp2(m_prev - m_next)
            pt = jnp.exp2((s_ref[sub] - m_next[0:1, :]).astype(BF16))
            m_ref[sub] = m_next
            vt_j = jnp.concatenate([vt_ref[0, j, sub * V_HEAD:(sub + 1) * V_HEAD, :],
                                    jnp.ones((ACC_ROWS - V_HEAD, tq), BF16)], axis=0)
            acc_ref[sub] = alpha[0:1, :] * acc_ref[sub] + _dot(vt_j, pt)

    def q_tile(qi, carry):
        m_ref[...] = jnp.full_like(m_ref, MASK_VALUE)
        acc_ref[...] = jnp.zeros_like(acc_ref)

        @pl.when(qi == 0)
        def _():
            score(qi, 0, 0, True)
            value(0, 0)

        @pl.when(qi > 0)
        def _():
            score(qi, 0, 0, False)
            n_pairs = (qi - 1) // 2

            def body(p, c):
                score(qi, 2 * p + 1, 1, False)
                value(2 * p, 0)
                score(qi, 2 * p + 2, 0, False)
                value(2 * p + 1, 1)
                return c

            lax.fori_loop(0, n_pairs, body, 0)
            t0 = 2 * n_pairs

            @pl.when(qi % 2 == 1)
            def _():
                score(qi, qi, 1, True)
                value(t0, 0)
                value(qi, 1)

            @pl.when(qi % 2 == 0)
            def _():
                score(qi, t0 + 1, 1, False)
                value(t0, 0)
                score(qi, qi, 0, True)
                value(t0 + 1, 1)
                value(qi, 0)

        ot = jnp.concatenate([acc_ref[sub, 0:V_HEAD, :] / acc_ref[sub, V_HEAD:V_HEAD + 1, :]
                              for sub in range(2)], axis=0)
        o_ref[pl.ds(pl.multiple_of(qi * tq, tq), tq), :] = ot.T.astype(BF16)
        return carry

    lax.fori_loop(0, nq, q_tile, 0)


def _attention(qt, k, vt, *, batch, seq, tq):
    t = k.shape[0]
    nq = seq // tq
    pairs = MLA_HEADS // 2
    return pl.pallas_call(
        functools.partial(_attn_kernel, tq=tq, nq=nq),
        grid=(batch, pairs),
        in_specs=[pl.BlockSpec((1, nq, 2 * HEAD_PAD, tq), lambda b, hp: (b, 0, hp, 0)),
                  pl.BlockSpec((seq, 2 * HEAD_PAD), lambda b, hp: (b, hp)),
                  pl.BlockSpec((1, nq, 2 * V_HEAD, tq), lambda b, hp: (b, 0, hp, 0))],
        out_specs=pl.BlockSpec((seq, 2 * V_HEAD), lambda b, hp: (b, hp)),
        out_shape=jax.ShapeDtypeStruct((t, MLA_HEADS * V_HEAD), BF16),
        scratch_shapes=[pltpu.VMEM((2, 8, tq), F32), pltpu.VMEM((2, ACC_ROWS, tq), F32),
                        pltpu.VMEM((2, tq, tq), F32), pltpu.VMEM((2, tq, tq), F32),
                        pltpu.VMEM((2, 8, tq), F32), pltpu.VMEM((2, 8, tq), F32)],
        compiler_params=pltpu.CompilerParams(dimension_semantics=("arbitrary",) * 2, vmem_limit_bytes=VMEM_LIMIT),
        name="attention",
    )(qt, k, vt)


def _pack_pair(lo, hi):
    lo_b = lax.bitcast_convert_type(lo, jnp.uint32)
    hi_b = lax.bitcast_convert_type(hi, jnp.uint32)
    return (hi_b & jnp.uint32(0xFFFF0000)) | (lo_b >> 16)


def _unpack_pair(p):
    lo = lax.bitcast_convert_type(p << 16, F32)
    hi = lax.bitcast_convert_type(p & jnp.uint32(0xFFFF0000), F32)
    return lo, hi


def _route(logits):
    lane = lax.broadcasted_iota(jnp.int32, logits.shape, 1)
    lane_f = lane.astype(F32)
    big = float(LANES)
    is_grp = (lane >= GROUP_LANE0) & (lane < GROUP_LANE0 + N_GROUPS)
    g_log = jnp.where(is_grp, logits, MASK_VALUE)
    g_max = jnp.max(g_log, axis=1, keepdims=True)
    g_sum = jnp.sum(jnp.exp(g_log - g_max), axis=1, keepdims=True)
    g_p = 1.0 / g_sum
    g_idx = jnp.min(jnp.where(g_log == g_max, lane_f, big), axis=1, keepdims=True) - float(GROUP_LANE0)
    lo = g_idx * float(EXPERTS_PER_GROUP)
    in_grp = (lane_f >= lo) & (lane_f < lo + float(EXPERTS_PER_GROUP))
    e_log = jnp.where(in_grp, logits, MASK_VALUE)
    e_max = jnp.max(e_log, axis=1, keepdims=True)
    e_exp = jnp.exp(e_log - e_max)
    e_prob = e_exp / jnp.sum(e_exp, axis=1, keepdims=True)
    cand = jnp.where(in_grp, e_prob, -1.0)
    p1 = jnp.max(cand, axis=1, keepdims=True)
    i1 = jnp.min(jnp.where(cand == p1, lane_f, big), axis=1, keepdims=True)
    cand2 = jnp.where(lane_f == i1, -1.0, cand)
    p2 = jnp.max(cand2, axis=1, keepdims=True)
    i2 = jnp.min(jnp.where(cand2 == p2, lane_f, big), axis=1, keepdims=True)
    denom = p1 + p2
    return g_p * (p1 / denom), g_p * (p2 / denom), i1, i2


_L_W1, _L_W2 = 0, 1
_M_E1, _M_E2, _M_R1H, _M_R1L, _M_R2H, _M_R2L = range(6)


def _mixer_post_kernel(o_ref, part_ref, ga_ref, x_ref, wa_ref, wo_ref, nf_ref, wr_ref, br_ref,
                       x1_ref, h2p_ref, info_ref, meta_ref, cnt_ref, run_ref, *, tm):
    i = pl.program_id(0)
    half = x_ref.shape[1] // 2

    @pl.when(i == 0)
    def _():
        run_ref[...] = jnp.zeros_like(run_ref)

    ya = _dot(o_ref[...], wa_ref[...])
    merged = ga_ref[...].astype(F32) * ya + part_ref[...].astype(F32)
    x1 = x_ref[...] + _dot(merged.astype(BF16), wo_ref[...])
    x1_ref[...] = x1
    h2 = _rms(x1, nf_ref[...]).astype(BF16)
    h2f = h2.astype(F32)
    h2p_ref[...] = _pack_pair(h2f[:, :half], h2f[:, half:])
    w1, w2, i1, i2 = _route(_dot(h2, wr_ref[...]) + br_ref[...])

    lane_f = lax.broadcasted_iota(jnp.int32, (tm, LANES), 1).astype(F32)
    oh1 = lane_f == i1
    oh2 = lane_f == i2
    both = jnp.where(oh1, 1.0, 0.0) + jnp.where(oh2, 1.0, 0.0)
    r_i = lax.broadcasted_iota(jnp.int32, (tm, tm), 0)
    c_i = lax.broadcasted_iota(jnp.int32, (tm, tm), 1)
    earlier = jnp.where(c_i < r_i, 1.0, 0.0).astype(BF16)
    before = _dot(earlier, both.astype(BF16)) + run_ref[0:1, :]
    rank1 = jnp.sum(jnp.where(oh1, before, 0.0), axis=1, keepdims=True)
    rank2 = jnp.sum(jnp.where(oh2, before, 0.0), axis=1, keepdims=True)
    run_ref[0:1, :] = run_ref[0:1, :] + jnp.sum(both, axis=0, keepdims=True)
    cnt_ref[...] = run_ref[...]

    lane = lax.broadcasted_iota(jnp.int32, (tm, LANES), 1)
    info_ref[...] = jnp.where(lane == _L_W1, w1, jnp.where(lane == _L_W2, w2, 0.0))

    r1_hi = jnp.floor(rank1 * (1.0 / 256.0))
    r2_hi = jnp.floor(rank2 * (1.0 / 256.0))
    cols = (i1, i2, r1_hi, rank1 - 256.0 * r1_hi, r2_hi, rank2 - 256.0 * r2_hi)
    vals = jnp.zeros((tm, LANES), F32)
    for ln, val in enumerate(cols):
        vals = jnp.where(lane == ln, val, vals)
    s_r = lax.broadcasted_iota(jnp.int32, (8, LANES), 0)
    s_c = lax.broadcasted_iota(jnp.int32, (8, LANES), 1)
    sel = jnp.where((s_r == s_c) & (s_r < len(cols)), 1.0, 0.0).astype(BF16)
    meta_ref[0] = lax.dot_general(sel, vals.astype(BF16), (((1,), (1,)), ((), ())),
                                  preferred_element_type=F32)


def _mixer_post(o, part, ga, x2, lw, *, tm):
    t, d = x2.shape
    row = lambda w: pl.BlockSpec((tm, w), lambda i: (i, 0))
    consts = [lw["w_a_out"], lw["w_o"], lw["norm_ffn"], lw["w_router"], lw["b_router"]]
    return pl.pallas_call(
        functools.partial(_mixer_post_kernel, tm=tm),
        grid=(t // tm,),
        in_specs=[row(o.shape[1]), row(d), row(d), row(d)] + [_const_spec(c.shape) for c in consts],
        out_specs=[row(d), row(d // 2), row(LANES), pl.BlockSpec((1, 8, tm), lambda i: (i, 0, 0)),
                   pl.BlockSpec((8, LANES), lambda i: (0, 0))],
        out_shape=[jax.ShapeDtypeStruct((t, d), F32), jax.ShapeDtypeStruct((t, d // 2), jnp.uint32),
                   jax.ShapeDtypeStruct((t, LANES), F32), jax.ShapeDtypeStruct((t // tm, 8, tm), F32),
                   jax.ShapeDtypeStruct((8, LANES), F32)],
        scratch_shapes=[pltpu.VMEM((8, LANES), F32)],
        compiler_params=pltpu.CompilerParams(dimension_semantics=("arbitrary",), vmem_limit_bytes=VMEM_LIMIT),
        name="mixer_post",
    )(o, part, ga, x2, *consts)


DISPATCH_RING = 4


def _dispatch_kernel(slots_ref, h2p_hbm, xs_in_hbm, xs_hbm, buf, lsem, rsem, *, td):
    del xs_in_hbm
    i = pl.program_id(0)
    n = pl.num_programs(0)

    def load(tile, b):
        return pltpu.make_async_copy(h2p_hbm.at[pl.ds(tile * td, td)], buf.at[b], lsem.at[b])

    def wait_rows(b):
        for _ in range(2):
            pltpu.make_async_copy(buf.at[b], xs_hbm.at[pl.ds(0, td)], rsem.at[b]).wait()

    @pl.when(i == 0)
    def _():
        load(0, 0).start()
        load(1, 1).start()

    for k in range(DISPATCH_RING):
        tile = DISPATCH_RING * i + k
        load(tile, k).wait()
        for r in range(td):
            for a in range(2):
                dst = slots_ref[0, 0, (2 * k + a) * td + r]
                pltpu.make_async_copy(buf.at[k, pl.ds(r, 1)], xs_hbm.at[pl.ds(dst, 1)],
                                      rsem.at[k]).start(priority=a)
        if k > 0:
            wait_rows(k - 1)
        else:
            @pl.when(i > 0)
            def _():
                wait_rows(DISPATCH_RING - 1)
        if k + 2 < DISPATCH_RING:
            load(tile + 2, k + 2).start()
        else:
            @pl.when(i + 1 < n)
            def _():
                load(tile + 2, (k + 2) % DISPATCH_RING).start()

    @pl.when(i == n - 1)
    def _():
        wait_rows(DISPATCH_RING - 1)


def _dispatch(h2p, slots, xs_init, *, td):
    t, half = h2p.shape
    steps = t // (DISPATCH_RING * td)
    return pl.pallas_call(
        functools.partial(_dispatch_kernel, td=td),
        grid=(steps,),
        in_specs=[pl.BlockSpec((1, 1, DISPATCH_RING * 2 * td), lambda i: (i, 0, 0), memory_space=pltpu.SMEM),
                  pl.BlockSpec(memory_space=pl.ANY),
                  pl.BlockSpec(memory_space=pl.ANY)],
        out_specs=pl.BlockSpec(memory_space=pl.ANY),
        out_shape=jax.ShapeDtypeStruct(xs_init.shape, jnp.uint32),
        scratch_shapes=[pltpu.VMEM((DISPATCH_RING, td, half), jnp.uint32),
                        pltpu.SemaphoreType.DMA((DISPATCH_RING,)), pltpu.SemaphoreType.DMA((DISPATCH_RING,))],
        input_output_aliases={2: 0},
        compiler_params=pltpu.CompilerParams(dimension_semantics=("arbitrary",), vmem_limit_bytes=VMEM_LIMIT,
                                             has_side_effects=True),
        name="dispatch",
    )(slots.reshape(steps, 1, DISPATCH_RING * 2 * td), h2p, xs_init)


def _expert_kernel(te_ref, nu_ref, xs_ref, wgu_ref, wdn_ref, ys_ref):
    i = pl.program_id(0)
    half = wgu_ref.shape[1] // 2

    @pl.when(i < nu_ref[0])
    def _():
        lo, hi = _unpack_pair(xs_ref[...])
        gu = _dot(lo.astype(BF16), wgu_ref[0, :half, :]) + _dot(hi.astype(BF16), wgu_ref[0, half:, :])
        act = (jax.nn.silu(gu[:, :D_EXPERT]) * gu[:, D_EXPERT:]).astype(BF16)
        y = _dot(act, wdn_ref[0]).astype(BF16).astype(F32)
        ys_ref[...] = _pack_pair(y[:, :half], y[:, half:])

    @pl.when(i >= nu_ref[0])
    def _():
        ys_ref[...] = jnp.zeros_like(ys_ref)


def _experts(xs, tile_expert, n_used, lw, *, tme):
    n_slots, half = xs.shape
    d = 2 * half
    grid_spec = pltpu.PrefetchScalarGridSpec(
        num_scalar_prefetch=2,
        grid=(n_slots // tme,),
        in_specs=[pl.BlockSpec((tme, half), lambda i, te, nu: (i, 0)),
                  pl.BlockSpec((1, d, 2 * D_EXPERT), lambda i, te, nu: (te[i], 0, 0)),
                  pl.BlockSpec((1, D_EXPERT, d), lambda i, te, nu: (te[i], 0, 0))],
        out_specs=pl.BlockSpec((tme, half), lambda i, te, nu: (i, 0)),
    )
    return pl.pallas_call(
        _expert_kernel,
        grid_spec=grid_spec,
        out_shape=jax.ShapeDtypeStruct((n_slots, half), jnp.uint32),
        compiler_params=pltpu.CompilerParams(dimension_semantics=("arbitrary",), vmem_limit_bytes=VMEM_LIMIT),
        name="experts",
    )(tile_expert, n_used, xs, lw["w_gate_up"], lw["w_down"])


def _combine_kernel(s0_ref, s1_ref, s2_ref, x1_ref, info_ref, ys_hbm, fn_ref, out_ref, ya, yb, sem, *, tmc):
    i = pl.program_id(0)
    n = pl.num_programs(0)
    half = x1_ref.shape[1] // 2

    def gather(idx_ref, buf, k):
        for r in range(2 * tmc):
            src = idx_ref[0, 0, r]
            pltpu.make_async_copy(ys_hbm.at[pl.ds(src, 1)], buf.at[pl.ds(r, 1)],
                                  sem.at[k]).start(priority=r % 2)

    def finish(buf, k, rows):
        pltpu.make_async_copy(ys_hbm.at[pl.ds(0, 2 * tmc)], buf, sem.at[k]).wait()
        lo1, hi1 = _unpack_pair(buf[0:tmc, :])
        lo2, hi2 = _unpack_pair(buf[tmc:2 * tmc, :])
        info = info_ref[rows, :]
        w1 = info[:, _L_W1:_L_W1 + 1]
        w2 = info[:, _L_W2:_L_W2 + 1]
        lo = x1_ref[rows, :half] + w1 * lo1 + w2 * lo2
        hi = x1_ref[rows, half:] + w1 * hi1 + w2 * hi2
        if fn_ref is not None:
            ms = (jnp.sum(lo * lo, axis=-1, keepdims=True) + jnp.sum(hi * hi, axis=-1, keepdims=True)) / (2 * half)
            inv = lax.rsqrt(ms + EPS)
            lo = lo * inv * fn_ref[:, :half]
            hi = hi * inv * fn_ref[:, half:]
        out_ref[rows, :half] = lo
        out_ref[rows, half:] = hi

    @pl.when(i == 0)
    def _():
        gather(s0_ref, ya, 0)

    gather(s1_ref, yb, 1)
    finish(ya, 0, slice(0, tmc))

    @pl.when(i + 1 < n)
    def _():
        gather(s2_ref, ya, 0)

    finish(yb, 1, slice(tmc, 2 * tmc))


def _combine_plain_kernel(s0_ref, s1_ref, s2_ref, x1_ref, info_ref, ys_hbm, out_ref, ya, yb, sem, *, tmc):
    _combine_kernel(s0_ref, s1_ref, s2_ref, x1_ref, info_ref, ys_hbm, None, out_ref, ya, yb, sem, tmc=tmc)


def _combine(x1, info, ys, slots, fn, *, tmc, final_norm):
    t, d = x1.shape
    n_tiles = t // tmc
    n = n_tiles // 2
    smem = lambda f: pl.BlockSpec((1, 1, 2 * tmc), f, memory_space=pltpu.SMEM)
    in_specs = [smem(lambda i: (2 * i, 0, 0)), smem(lambda i: (2 * i + 1, 0, 0)),
                smem(lambda i: (jnp.minimum(2 * i + 2, n_tiles - 1), 0, 0)),
                pl.BlockSpec((2 * tmc, d), lambda i: (i, 0)),
                pl.BlockSpec((2 * tmc, LANES), lambda i: (i, 0)),
                pl.BlockSpec(memory_space=pl.ANY)]
    args = [slots, slots, slots, x1, info, ys]
    if final_norm:
        in_specs.append(pl.BlockSpec((1, d), lambda i: (0, 0)))
        args.append(fn)
    return pl.pallas_call(
        functools.partial(_combine_kernel if final_norm else _combine_plain_kernel, tmc=tmc),
        grid=(n,),
        in_specs=in_specs,
        out_specs=pl.BlockSpec((2 * tmc, d), lambda i: (i, 0)),
        out_shape=jax.ShapeDtypeStruct((t, d), F32),
        scratch_shapes=[pltpu.VMEM((2 * tmc, d // 2), jnp.uint32), pltpu.VMEM((2 * tmc, d // 2), jnp.uint32),
                        pltpu.SemaphoreType.DMA((2,))],
        compiler_params=pltpu.CompilerParams(dimension_semantics=("arbitrary",), vmem_limit_bytes=VMEM_LIMIT),
        name="combine",
    )(*args)


def _moe(x1, h2p, info, meta, cnt, lw, fn, xs_init, *, tme, tmc, final_norm):
    t = x1.shape[0]
    n_tiles = (2 * t) // tme + N_EXPERTS
    counts = cnt[0, :N_EXPERTS].astype(jnp.int32)
    padded = ((counts + tme - 1) // tme) * tme
    ends = jnp.cumsum(padded)
    starts = ends - padded
    mi = meta.astype(jnp.int32)
    expert_ids = jnp.arange(N_EXPERTS, dtype=jnp.int32)

    def slot_of(e, r_hi, r_lo):
        start = jnp.sum(jnp.where(e[..., None] == expert_ids, starts, 0), axis=-1)
        return (start + r_hi * 256 + r_lo).reshape(t // tmc, 1, tmc)

    slot1 = slot_of(mi[:, _M_E1], mi[:, _M_R1H], mi[:, _M_R1L])
    slot2 = slot_of(mi[:, _M_E2], mi[:, _M_R2H], mi[:, _M_R2L])
    slots = jnp.concatenate([slot1, slot2], axis=2)
    tile_start = jnp.arange(n_tiles, dtype=jnp.int32) * tme
    tile_expert = jnp.minimum(jnp.sum((ends[None, :] <= tile_start[:, None]).astype(jnp.int32), axis=1),
                              N_EXPERTS - 1)
    n_used = ends[-1:] // tme
    xs = _dispatch(h2p, slots, xs_init, td=tmc)
    ys = _experts(xs, tile_expert, n_used, lw, tme=tme)
    return _combine(x1, info, ys, slots, fn, tmc=tmc, final_norm=final_norm), xs


def _pack_layer(l, w_in, norm_mix, q_norm, w_q_up, kv_norm, w_kv_up, sg_norm, w_s, b_s, conv_w,
                w_a_out, w_b_out, w_c_out, w_o, norm_ffn, w_group_router, b_group_router,
                w_expert_router, b_expert_router, w_gate_up, w_down):
    d = w_in.shape[1]
    wi = w_in[l]
    o_kr = Q_RANK + KV_RANK
    kr_slot = jnp.zeros((d, HEAD_PAD), F32).at[:, QK_NOPE:QK_HEAD].set(wi[:, o_kr:o_kr + QK_ROPE])
    w_in_p = jnp.concatenate([wi[:, :o_kr], kr_slot, wi[:, o_kr + QK_ROPE:]], axis=1).astype(BF16)
    wq = w_q_up[l].reshape(Q_RANK, MLA_HEADS, QK_HEAD)
    wq = jnp.pad(wq, ((0, 0), (0, 0), (0, HEAD_PAD - QK_HEAD))).reshape(Q_RANK, MLA_HEADS * HEAD_PAD)
    wkv = w_kv_up[l].reshape(KV_RANK, MLA_HEADS, QK_NOPE + V_HEAD)
    wk = jnp.pad(wkv[:, :, :QK_NOPE], ((0, 0), (0, 0), (0, HEAD_PAD - QK_NOPE)))
    wk = wk.reshape(KV_RANK, MLA_HEADS * HEAD_PAD)
    wv = wkv[:, :, QK_NOPE:].reshape(KV_RANK, MLA_HEADS * V_HEAD)
    w_r = jnp.zeros((d, LANES), F32)
    w_r = w_r.at[:, :N_EXPERTS].set(w_expert_router[l])
    w_r = w_r.at[:, GROUP_LANE0:GROUP_LANE0 + N_GROUPS].set(w_group_router[l])
    b_r = jnp.zeros((1, LANES), F32)
    b_r = b_r.at[0, :N_EXPERTS].set(b_expert_router[l])
    b_r = b_r.at[0, GROUP_LANE0:GROUP_LANE0 + N_GROUPS].set(b_group_router[l])
    return {
        "norm_mix": norm_mix[l][None, :], "w_in": w_in_p,
        "q_norm": q_norm[l][None, :], "w_q": wq.astype(BF16),
        "kv_norm": kv_norm[l][None, :], "w_k": wk.astype(BF16), "w_v": wv.astype(BF16),
        "sg_norm": sg_norm[l][None, :], "w_s": w_s[l],
        "b_s": jnp.broadcast_to(b_s[l][:, :, None], (SG_GROUPS, CHUNK, SG_GROUP_DIM)),
        "conv_w": conv_w[l],
        "w_a_out": w_a_out[l].astype(BF16), "w_b_out": w_b_out[l].astype(BF16),
        "w_c_out": w_c_out[l].astype(BF16), "w_o": w_o[l].astype(BF16),
        "norm_ffn": norm_ffn[l][None, :], "w_router": w_r.astype(BF16), "b_router": b_r,
        "w_gate_up": w_gate_up[l].astype(BF16), "w_down": w_down[l].astype(BF16),
    }


def _rope_tables(positions):
    inv = ROPE_BASE ** (-jnp.arange(0, QK_ROPE, 2, dtype=F32) / QK_ROPE)
    ang = inv[:, None] * positions.astype(F32).reshape(-1)[None, :]
    return jnp.cos(ang), jnp.sin(ang)


def kernel(x, positions, norm_mix, w_in, q_norm, w_q_up, kv_norm, w_kv_up, sg_norm, w_s, b_s, conv_w,
           w_a_out, w_b_out, w_c_out, w_o, norm_ffn, w_group_router, b_group_router, w_expert_router,
           b_expert_router, w_gate_up, w_down, final_norm):
    batch, seq, d = x.shape
    depth = w_in.shape[0]
    tm = min(512, seq)
    tq = tm
    tme = 512
    tmc = min(256, seq)
    cos_t, sin_t = _rope_tables(positions)
    x2 = x.reshape(batch * seq, d)
    fn = final_norm[None, :]
    xs = jnp.zeros((2 * batch * seq + N_EXPERTS * tme, d // 2), jnp.uint32)
    for l in range(depth):
        lw = _pack_layer(l, w_in, norm_mix, q_norm, w_q_up, kv_norm, w_kv_up, sg_norm, w_s, b_s, conv_w,
                         w_a_out, w_b_out, w_c_out, w_o, norm_ffn, w_group_router, b_group_router,
                         w_expert_router, b_expert_router, w_gate_up, w_down)
        q, k, v, part, ga = _mixer_pre(x2, lw, cos_t, sin_t, seq=seq, tm=tm)
        o = _attention(q, k, v, batch=batch, seq=seq, tq=tq)
        x1, h2p, info, meta, cnt = _mixer_post(o, part, ga, x2, lw, tm=tm)
        x2, xs = _moe(x1, h2p, info, meta, cnt, lw, fn, xs, tme=tme, tmc=tmc, final_norm=(l == depth - 1))
    return x2.reshape(batch, seq, d)
```

```python
import functools

import jax
import jax.numpy as jnp
from jax import lax
from jax.experimental import pallas as pl
from jax.experimental.pallas import tpu as pltpu

F32 = jnp.float32
BF16 = jnp.bfloat16

EPS = 1e-6
MLA_HEADS = 8
QK_NOPE = 64
QK_ROPE = 32
QK_HEAD = QK_NOPE + QK_ROPE
V_HEAD = 64
Q_RANK = 384
KV_RANK = 256
ROPE_BASE = 10000.0
SG_GROUPS = 4
SG_GROUP_DIM = 128
SG_WIDTH = SG_GROUPS * SG_GROUP_DIM
CHUNK = 128
CONV_WIDTH = 512
CONV_K = 3
N_GROUPS = 4
EXPERTS_PER_GROUP = 8
N_EXPERTS = N_GROUPS * EXPERTS_PER_GROUP
D_EXPERT = 256

LANES = 128
HEAD_PAD = 128
MASK_VALUE = -1e30
LOG2_E = 1.4426950408889634
VMEM_LIMIT = 56 * 1024 * 1024

_P1 = Q_RANK + KV_RANK + HEAD_PAD
_P2 = _P1 + 2 * SG_WIDTH
_P3 = _P2 + 3 * CONV_WIDTH
GROUP_ROW0 = N_EXPERTS


def _dot(a, b):
    return jnp.dot(a, b, preferred_element_type=F32)


def _rms(xf, g):
    return xf * lax.rsqrt(jnp.mean(xf * xf, axis=-1, keepdims=True) + EPS) * g


def _gelu(x):
    return 0.5 * x * (1.0 + jnp.tanh(0.7978845608028654 * (x + 0.044715 * (x * x * x))))


def _rope_t(tt, cos, sin):
    lo, mid = QK_NOPE, QK_NOPE + QK_ROPE // 2
    x1, x2 = tt[lo:mid], tt[mid:QK_HEAD]
    return jnp.concatenate([tt[:lo], x1 * cos - x2 * sin, x2 * cos + x1 * sin, tt[QK_HEAD:]], axis=0)


def _const_spec(shape):
    nd = len(shape)
    return pl.BlockSpec(shape, lambda *_: (0,) * nd, pipeline_mode=pl.Buffered(1))


def _mixer_pre_kernel(x_ref, nm_ref, win_ref, qn_ref, wq_ref, kvn_ref, wk_ref, wv_ref, sgn_ref,
                      ws_ref, bs_ref, cw_ref, wb_ref, wc_ref, cos_ref, sin_ref,
                      qt_ref, k_ref, vt_ref, part_ref, ga_ref,
                      halo_ref, ybin_ref, *, tiles_per_seq, tm):
    i = pl.program_id(0)
    d_model = x_ref.shape[1]
    h = _rms(x_ref[...], nm_ref[...]).astype(BF16)

    p1 = _dot(h, win_ref[:, 0:_P1])
    cos, sin = cos_ref[...], sin_ref[...]
    qn = _rms(p1[:, :Q_RANK], qn_ref[...]).astype(BF16)
    qf = _dot(qn, wq_ref[...])
    scale = QK_HEAD ** -0.5 * LOG2_E
    for hd in range(MLA_HEADS):
        sl = slice(hd * HEAD_PAD, (hd + 1) * HEAD_PAD)
        qt_ref[0, 0, sl, :] = (_rope_t(qf[:, sl].T, cos, sin) * scale).astype(BF16)
    kvn = _rms(p1[:, Q_RANK:Q_RANK + KV_RANK], kvn_ref[...]).astype(BF16)
    kr = _rope_t(p1[:, Q_RANK + KV_RANK:_P1].T, cos, sin).T
    kf = _dot(kvn, wk_ref[...])
    for hd in range(MLA_HEADS):
        sl = slice(hd * HEAD_PAD, (hd + 1) * HEAD_PAD)
        k_ref[:, sl] = (kf[:, sl] + kr).astype(BF16)
    vt_ref[0, 0] = _dot(kvn, wv_ref[...]).T.astype(BF16)

    p2 = _dot(h, win_ref[:, _P1:_P2])
    u = _gelu(p2[:, :SG_WIDTH])
    vb = _rms(_gelu(p2[:, SG_WIDTH:]), sgn_ref[...]).astype(BF16)
    r_i = lax.broadcasted_iota(jnp.int32, (CHUNK, CHUNK), 0)
    c_i = lax.broadcasted_iota(jnp.int32, (CHUNK, CHUNK), 1)
    for g in range(SG_GROUPS):
        wg = jnp.where(c_i <= r_i, ws_ref[g], 0.0).astype(BF16)
        gs = slice(g * SG_GROUP_DIM, (g + 1) * SG_GROUP_DIM)
        for c in range(tm // CHUNK):
            cs = slice(c * CHUNK, (c + 1) * CHUNK)
            mixed = _dot(wg, vb[cs, gs]) + bs_ref[g]
            ybin_ref[cs, gs] = (u[cs, gs] * mixed).astype(BF16)

    @pl.when(i % tiles_per_seq == 0)
    def _():
        halo_ref[...] = jnp.zeros_like(halo_ref)

    p3 = _dot(h, win_ref[:, _P2:_P3])
    z = p3[:, CONV_WIDTH:2 * CONV_WIDTH] * p3[:, 2 * CONV_WIDTH:]
    row = lax.broadcasted_iota(jnp.int32, z.shape, 0)
    halo = halo_ref[...]
    last1 = halo[7:8, :]
    last2 = halo[6:7, :]
    z1 = jnp.where(row == 0, last1, pltpu.roll(z, 1, 0))
    z2 = jnp.where(row == 0, last2, jnp.where(row == 1, last1, pltpu.roll(z, 2, 0)))
    y = cw_ref[0:1, :] * z2 + cw_ref[1:2, :] * z1 + cw_ref[2:3, :] * z
    yc_in = (p3[:, :CONV_WIDTH] * y).astype(BF16)
    halo_ref[...] = z[tm - 8:tm, :]

    p4 = _dot(h, win_ref[:, _P3:_P3 + 3 * d_model])
    yb = _dot(ybin_ref[...], wb_ref[...])
    yc = _dot(yc_in, wc_ref[...])
    part = (jax.nn.sigmoid(p4[:, d_model:2 * d_model]) * yb
            + jax.nn.sigmoid(p4[:, 2 * d_model:]) * yc)
    part_ref[...] = part.astype(BF16)
    ga_ref[...] = jax.nn.sigmoid(p4[:, :d_model]).astype(BF16)


def _mixer_pre(x2, lw, cos_t, sin_t, *, seq, tm):
    t, d = x2.shape
    row = lambda w: pl.BlockSpec((tm, w), lambda i: (i, 0))
    consts = [lw["norm_mix"], lw["w_in"], lw["q_norm"], lw["w_q"], lw["kv_norm"], lw["w_k"], lw["w_v"],
              lw["sg_norm"], lw["w_s"], lw["b_s"], lw["conv_w"], lw["w_b_out"], lw["w_c_out"]]
    tps = seq // tm
    kern = functools.partial(_mixer_pre_kernel, tiles_per_seq=tps, tm=tm)
    slab = lambda w: pl.BlockSpec((1, 1, w, tm), lambda i: (i // tps, i % tps, 0, 0))
    return pl.pallas_call(
        kern,
        grid=(t // tm,),
        in_specs=[row(d)] + [_const_spec(c.shape) for c in consts]
        + [pl.BlockSpec((QK_ROPE // 2, tm), lambda i: (0, i))] * 2,
        out_specs=[slab(MLA_HEADS * HEAD_PAD), row(MLA_HEADS * HEAD_PAD), slab(MLA_HEADS * V_HEAD), row(d), row(d)],
        out_shape=[jax.ShapeDtypeStruct((t // seq, tps, MLA_HEADS * HEAD_PAD, tm), BF16),
                   jax.ShapeDtypeStruct((t, MLA_HEADS * HEAD_PAD), BF16),
                   jax.ShapeDtypeStruct((t // seq, tps, MLA_HEADS * V_HEAD, tm), BF16),
                   jax.ShapeDtypeStruct((t, d), BF16),
                   jax.ShapeDtypeStruct((t, d), BF16)],
        scratch_shapes=[pltpu.VMEM((8, CONV_WIDTH), F32), pltpu.VMEM((tm, SG_WIDTH), BF16)],
        compiler_params=pltpu.CompilerParams(dimension_semantics=("arbitrary",), vmem_limit_bytes=VMEM_LIMIT),
        name="mixer_pre",
    )(x2, *consts, cos_t, sin_t)


ACC_ROWS = V_HEAD + 16


def _attn_kernel(qt_ref, k_ref, vt_ref, o_ref, m_ref, acc_ref, sa_ref, sb_ref, mca_ref, mcb_ref, *, tq, nq):
    bufs = ((sa_ref, mca_ref), (sb_ref, mcb_ref))

    def score(qi, j, buf, masked):
        s_ref, mc_ref = bufs[buf]
        start = pl.multiple_of(j * tq, tq)
        for sub in range(2):
            k_j = k_ref[pl.ds(start, tq), sub * HEAD_PAD:(sub + 1) * HEAD_PAD]
            st = _dot(k_j, qt_ref[0, qi, sub * HEAD_PAD:(sub + 1) * HEAD_PAD, :])
            if masked:
                kv_i = lax.broadcasted_iota(jnp.int32, st.shape, 0)
                q_i = lax.broadcasted_iota(jnp.int32, st.shape, 1)
                st = jnp.where(kv_i <= q_i, st, MASK_VALUE)
            s_ref[sub] = st
            mc_ref[sub] = jnp.broadcast_to(jnp.max(st, axis=0, keepdims=True), (8, tq))

    def value(j, buf):
        s_ref, mc_ref = bufs[buf]
        for sub in range(2):
            m_prev = m_ref[sub]
            m_next = jnp.maximum(m_prev, mc_ref[sub])
            alpha = jnp.exp2(m_prev - m_next)
            pt = jnp.exp2((s_ref[sub] - m_next[0:1, :]).astype(BF16))
            m_ref[sub] = m_next
            vt_j = jnp.concatenate([vt_ref[0, j, sub * V_HEAD:(sub + 1) * V_HEAD, :],
                                    jnp.ones((ACC_ROWS - V_HEAD, tq), BF16)], axis=0)
            acc_ref[sub] = alpha[0:1, :] * acc_ref[sub] + _dot(vt_j, pt)

    def q_tile(qi, carry):
        m_ref[...] = jnp.full_like(m_ref, MASK_VALUE)
        acc_ref[...] = jnp.zeros_like(acc_ref)

        @pl.when(qi == 0)
        def _():
            score(qi, 0, 0, True)
            value(0, 0)

        @pl.when(qi > 0)
        def _():
            score(qi, 0, 0, False)
            n_pairs = (qi - 1) // 2

            def body(p, c):
                score(qi, 2 * p + 1, 1, False)
                value(2 * p, 0)
                score(qi, 2 * p + 2, 0, False)
                value(2 * p + 1, 1)
                return c

            lax.fori_loop(0, n_pairs, body, 0)
            t0 = 2 * n_pairs

            @pl.when(qi % 2 == 1)
            def _():
                score(qi, qi, 1, True)
                value(t0, 0)
                value(qi, 1)

            @pl.when(qi % 2 == 0)
            def _():
                score(qi, t0 + 1, 1, False)
                value(t0, 0)
                score(qi, qi, 0, True)
                value(t0 + 1, 1)
                value(qi, 0)

        ot = jnp.concatenate([acc_ref[sub, 0:V_HEAD, :] / acc_ref[sub, V_HEAD:V_HEAD + 1, :]
                              for sub in range(2)], axis=0)
        o_ref[pl.ds(pl.multiple_of(qi * tq, tq), tq), :] = ot.T.astype(BF16)
        return carry

    lax.fori_loop(0, nq, q_tile, 0)


def _attention(qt, k, vt, *, batch, seq, tq):
    t = k.shape[0]
    nq = seq // tq
    pairs = MLA_HEADS // 2
    return pl.pallas_call(
        functools.partial(_attn_kernel, tq=tq, nq=nq),
        grid=(batch, pairs),
        in_specs=[pl.BlockSpec((1, nq, 2 * HEAD_PAD, tq), lambda b, hp: (b, 0, hp, 0)),
                  pl.BlockSpec((seq, 2 * HEAD_PAD), lambda b, hp: (b, hp)),
                  pl.BlockSpec((1, nq, 2 * V_HEAD, tq), lambda b, hp: (b, 0, hp, 0))],
        out_specs=pl.BlockSpec((seq, 2 * V_HEAD), lambda b, hp: (b, hp)),
        out_shape=jax.ShapeDtypeStruct((t, MLA_HEADS * V_HEAD), BF16),
        scratch_shapes=[pltpu.VMEM((2, 8, tq), F32), pltpu.VMEM((2, ACC_ROWS, tq), F32),
                        pltpu.VMEM((2, tq, tq), F32), pltpu.VMEM((2, tq, tq), F32),
                        pltpu.VMEM((2, 8, tq), F32), pltpu.VMEM((2, 8, tq), F32)],
        compiler_params=pltpu.CompilerParams(dimension_semantics=("arbitrary",) * 2, vmem_limit_bytes=VMEM_LIMIT),
        name="attention",
    )(qt, k, vt)


def _pack_pair(lo, hi):
    lo_b = lax.bitcast_convert_type(lo, jnp.uint32)
    hi_b = lax.bitcast_convert_type(hi, jnp.uint32)
    return (hi_b & jnp.uint32(0xFFFF0000)) | (lo_b >> 16)


def _unpack_pair(p):
    lo = lax.bitcast_convert_type(p << 16, F32)
    hi = lax.bitcast_convert_type(p & jnp.uint32(0xFFFF0000), F32)
    return lo, hi


def _route_t(lt):
    big = float(LANES)
    g = lt[GROUP_ROW0:GROUP_ROW0 + 8]
    g_row = lax.broadcasted_iota(jnp.int32, g.shape, 0).astype(F32)
    g_log = jnp.where(g_row < float(N_GROUPS), g, MASK_VALUE)
    g_max = jnp.max(g_log, axis=0, keepdims=True)
    g_p = 1.0 / jnp.sum(jnp.exp(g_log - g_max), axis=0, keepdims=True)
    g_idx = jnp.min(jnp.where(g_log == g_max, g_row, big), axis=0, keepdims=True)
    e = lt[0:N_EXPERTS]
    e_row = lax.broadcasted_iota(jnp.int32, e.shape, 0).astype(F32)
    lo = g_idx * float(EXPERTS_PER_GROUP)
    in_grp = (e_row >= lo) & (e_row < lo + float(EXPERTS_PER_GROUP))
    e_log = jnp.where(in_grp, e, MASK_VALUE)
    e_max = jnp.max(e_log, axis=0, keepdims=True)
    e_exp = jnp.exp(e_log - e_max)
    e_prob = e_exp / jnp.sum(e_exp, axis=0, keepdims=True)
    cand = jnp.where(in_grp, e_prob, -1.0)
    p1 = jnp.max(cand, axis=0, keepdims=True)
    i1 = jnp.min(jnp.where(cand == p1, e_row, big), axis=0, keepdims=True)
    cand2 = jnp.where(e_row == i1, -1.0, cand)
    p2 = jnp.max(cand2, axis=0, keepdims=True)
    i2 = jnp.min(jnp.where(cand2 == p2, e_row, big), axis=0, keepdims=True)
    denom = p1 + p2
    return g_p * (p1 / denom), g_p * (p2 / denom), i1, i2, e_row


_L_W1, _L_W2 = 0, 1
_M_E1, _M_E2, _M_R1, _M_R2 = range(4)


def _mixer_post_kernel(o_ref, part_ref, ga_ref, x_ref, wa_ref, wo_ref, nf_ref, wrt_ref, brt_ref,
                       x1_ref, h2p_ref, info_ref, meta_ref, cnt_ref, run_ref, *, tm):
    i = pl.program_id(0)
    half = x_ref.shape[1] // 2

    @pl.when(i == 0)
    def _():
        run_ref[...] = jnp.zeros_like(run_ref)

    ya = _dot(o_ref[...], wa_ref[...])
    merged = ga_ref[...].astype(F32) * ya + part_ref[...].astype(F32)
    x1 = x_ref[...] + _dot(merged.astype(BF16), wo_ref[...])
    x1_ref[...] = x1
    h2 = _rms(x1, nf_ref[...]).astype(BF16)
    h2f = h2.astype(F32)
    h2p_ref[...] = _pack_pair(h2f[:, :half], h2f[:, half:])
    lt = lax.dot_general(wrt_ref[...], h2, (((1,), (1,)), ((), ())), preferred_element_type=F32) + brt_ref[...]
    w1, w2, i1, i2, e_row = _route_t(lt)

    oh1 = e_row == i1
    oh2 = e_row == i2
    both = jnp.where(oh1, 1.0, 0.0) + jnp.where(oh2, 1.0, 0.0)
    r_i = lax.broadcasted_iota(jnp.int32, (tm, tm), 0)
    c_i = lax.broadcasted_iota(jnp.int32, (tm, tm), 1)
    earlier = jnp.where(r_i < c_i, 1.0, 0.0).astype(BF16)
    run = run_ref[...]
    before = _dot(both.astype(BF16), earlier) + run[:, 0:1]
    rank1 = jnp.sum(jnp.where(oh1, before, 0.0), axis=0, keepdims=True)
    rank2 = jnp.sum(jnp.where(oh2, before, 0.0), axis=0, keepdims=True)
    run = run + jnp.sum(both, axis=1, keepdims=True)
    run_ref[...] = run
    cnt_ref[...] = run

    row8 = lax.broadcasted_iota(jnp.int32, (8, tm), 0)
    meta_ref[0] = jnp.where(row8 == _M_E1, i1, jnp.where(row8 == _M_E2, i2,
                            jnp.where(row8 == _M_R1, rank1, jnp.where(row8 == _M_R2, rank2, 0.0))))
    row128 = lax.broadcasted_iota(jnp.int32, (LANES, tm), 0)
    info_ref[...] = jnp.where(row128 == _L_W1, w1, jnp.where(row128 == _L_W2, w2, 0.0)).T


def _mixer_post(o, part, ga, x2, lw, *, tm):
    t, d = x2.shape
    row = lambda w: pl.BlockSpec((tm, w), lambda i: (i, 0))
    consts = [lw["w_a_out"], lw["w_o"], lw["norm_ffn"], lw["w_router_t"], lw["b_router_t"]]
    return pl.pallas_call(
        functools.partial(_mixer_post_kernel, tm=tm),
        grid=(t // tm,),
        in_specs=[row(o.shape[1]), row(d), row(d), row(d)] + [_const_spec(c.shape) for c in consts],
        out_specs=[row(d), row(d // 2), row(LANES), pl.BlockSpec((1, 8, tm), lambda i: (i, 0, 0)),
                   pl.BlockSpec((N_EXPERTS, LANES), lambda i: (0, 0))],
        out_shape=[jax.ShapeDtypeStruct((t, d), F32), jax.ShapeDtypeStruct((t, d // 2), jnp.uint32),
                   jax.ShapeDtypeStruct((t, LANES), F32), jax.ShapeDtypeStruct((t // tm, 8, tm), F32),
                   jax.ShapeDtypeStruct((N_EXPERTS, LANES), F32)],
        scratch_shapes=[pltpu.VMEM((N_EXPERTS, LANES), F32)],
        compiler_params=pltpu.CompilerParams(dimension_semantics=("arbitrary",), vmem_limit_bytes=VMEM_LIMIT),
        name="mixer_post",
    )(o, part, ga, x2, *consts)


DISPATCH_RING = 4


def _dispatch_kernel(slots_ref, h2p_hbm, xs_in_hbm, xs_hbm, buf, lsem, rsem, *, td):
    del xs_in_hbm
    i = pl.program_id(0)
    n = pl.num_programs(0)

    def load(tile, b):
        return pltpu.make_async_copy(h2p_hbm.at[pl.ds(tile * td, td)], buf.at[b], lsem.at[b])

    def wait_rows(b):
        for _ in range(2):
            pltpu.make_async_copy(buf.at[b], xs_hbm.at[pl.ds(0, td)], rsem.at[b]).wait()

    @pl.when(i == 0)
    def _():
        load(0, 0).start()
        load(1, 1).start()

    for k in range(DISPATCH_RING):
        tile = DISPATCH_RING * i + k
        load(tile, k).wait()
        for r in range(td):
            for a in range(2):
                dst = slots_ref[0, 0, (2 * k + a) * td + r]
                pltpu.make_async_copy(buf.at[k, pl.ds(r, 1)], xs_hbm.at[pl.ds(dst, 1)],
                                      rsem.at[k]).start(priority=a)
        if k > 0:
            wait_rows(k - 1)
        else:
            @pl.when(i > 0)
            def _():
                wait_rows(DISPATCH_RING - 1)
        if k + 2 < DISPATCH_RING:
            load(tile + 2, k + 2).start()
        else:
            @pl.when(i + 1 < n)
            def _():
                load(tile + 2, (k + 2) % DISPATCH_RING).start()

    @pl.when(i == n - 1)
    def _():
        wait_rows(DISPATCH_RING - 1)


def _dispatch(h2p, slots, xs_init, *, td):
    t, half = h2p.shape
    steps = t // (DISPATCH_RING * td)
    return pl.pallas_call(
        functools.partial(_dispatch_kernel, td=td),
        grid=(steps,),
        in_specs=[pl.BlockSpec((1, 1, DISPATCH_RING * 2 * td), lambda i: (i, 0, 0), memory_space=pltpu.SMEM),
                  pl.BlockSpec(memory_space=pl.ANY),
                  pl.BlockSpec(memory_space=pl.ANY)],
        out_specs=pl.BlockSpec(memory_space=pl.ANY),
        out_shape=jax.ShapeDtypeStruct(xs_init.shape, jnp.uint32),
        scratch_shapes=[pltpu.VMEM((DISPATCH_RING, td, half), jnp.uint32),
                        pltpu.SemaphoreType.DMA((DISPATCH_RING,)), pltpu.SemaphoreType.DMA((DISPATCH_RING,))],
        input_output_aliases={2: 0},
        compiler_params=pltpu.CompilerParams(dimension_semantics=("arbitrary",), vmem_limit_bytes=VMEM_LIMIT,
                                             has_side_effects=True),
        name="dispatch",
    )(slots.reshape(steps, 1, DISPATCH_RING * 2 * td), h2p, xs_init)


def _expert_kernel(te_ref, nu_ref, xs_ref, wgu_ref, wdn_ref, ys_ref):
    i = pl.program_id(0)
    half = wgu_ref.shape[1] // 2

    @pl.when(i < nu_ref[0])
    def _():
        lo, hi = _unpack_pair(xs_ref[...])
        gu = _dot(lo.astype(BF16), wgu_ref[0, :half, :]) + _dot(hi.astype(BF16), wgu_ref[0, half:, :])
        act = (jax.nn.silu(gu[:, :D_EXPERT]) * gu[:, D_EXPERT:]).astype(BF16)
        y = _dot(act, wdn_ref[0]).astype(BF16).astype(F32)
        ys_ref[...] = _pack_pair(y[:, :half], y[:, half:])

    @pl.when(i >= nu_ref[0])
    def _():
        ys_ref[...] = jnp.zeros_like(ys_ref)


def _experts(xs, tile_expert, n_used, lw, *, tme):
    n_slots, half = xs.shape
    d = 2 * half
    grid_spec = pltpu.PrefetchScalarGridSpec(
        num_scalar_prefetch=2,
        grid=(n_slots // tme,),
        in_specs=[pl.BlockSpec((tme, half), lambda i, te, nu: (i, 0)),
                  pl.BlockSpec((1, d, 2 * D_EXPERT), lambda i, te, nu: (te[i], 0, 0)),
                  pl.BlockSpec((1, D_EXPERT, d), lambda i, te, nu: (te[i], 0, 0))],
        out_specs=pl.BlockSpec((tme, half), lambda i, te, nu: (i, 0)),
    )
    return pl.pallas_call(
        _expert_kernel,
        grid_spec=grid_spec,
        out_shape=jax.ShapeDtypeStruct((n_slots, half), jnp.uint32),
        compiler_params=pltpu.CompilerParams(dimension_semantics=("arbitrary",), vmem_limit_bytes=VMEM_LIMIT),
        name="experts",
    )(tile_expert, n_used, xs, lw["w_gate_up"], lw["w_down"])


def _combine_kernel(s0_ref, s1_ref, s2_ref, x1_ref, info_ref, ys_hbm, fn_ref, out_ref, ya, yb, sem, *, tmc):
    i = pl.program_id(0)
    n = pl.num_programs(0)
    half = x1_ref.shape[1] // 2

    def gather(idx_ref, buf, k):
        for r in range(2 * tmc):
            src = idx_ref[0, 0, r]
            pltpu.make_async_copy(ys_hbm.at[pl.ds(src, 1)], buf.at[pl.ds(r, 1)],
                                  sem.at[k]).start(priority=r % 2)

    def finish(buf, k, rows):
        pltpu.make_async_copy(ys_hbm.at[pl.ds(0, 2 * tmc)], buf, sem.at[k]).wait()
        lo1, hi1 = _unpack_pair(buf[0:tmc, :])
        lo2, hi2 = _unpack_pair(buf[tmc:2 * tmc, :])
        info = info_ref[rows, :]
        w1 = info[:, _L_W1:_L_W1 + 1]
        w2 = info[:, _L_W2:_L_W2 + 1]
        lo = x1_ref[rows, :half] + w1 * lo1 + w2 * lo2
        hi = x1_ref[rows, half:] + w1 * hi1 + w2 * hi2
        if fn_ref is not None:
            ms = (jnp.sum(lo * lo, axis=-1, keepdims=True) + jnp.sum(hi * hi, axis=-1, keepdims=True)) / (2 * half)
            inv = lax.rsqrt(ms + EPS)
            lo = lo * inv * fn_ref[:, :half]
            hi = hi * inv * fn_ref[:, half:]
        out_ref[rows, :half] = lo
        out_ref[rows, half:] = hi

    @pl.when(i == 0)
    def _():
        gather(s0_ref, ya, 0)

    gather(s1_ref, yb, 1)
    finish(ya, 0, slice(0, tmc))

    @pl.when(i + 1 < n)
    def _():
        gather(s2_ref, ya, 0)

    finish(yb, 1, slice(tmc, 2 * tmc))


def _combine_plain_kernel(s0_ref, s1_ref, s2_ref, x1_ref, info_ref, ys_hbm, out_ref, ya, yb, sem, *, tmc):
    _combine_kernel(s0_ref, s1_ref, s2_ref, x1_ref, info_ref, ys_hbm, None, out_ref, ya, yb, sem, tmc=tmc)


def _combine(x1, info, ys, slots, fn, *, tmc, final_norm):
    t, d = x1.shape
    n_tiles = t // tmc
    n = n_tiles // 2
    smem = lambda f: pl.BlockSpec((1, 1, 2 * tmc), f, memory_space=pltpu.SMEM)
    in_specs = [smem(lambda i: (2 * i, 0, 0)), smem(lambda i: (2 * i + 1, 0, 0)),
                smem(lambda i: (jnp.minimum(2 * i + 2, n_tiles - 1), 0, 0)),
                pl.BlockSpec((2 * tmc, d), lambda i: (i, 0)),
                pl.BlockSpec((2 * tmc, LANES), lambda i: (i, 0)),
                pl.BlockSpec(memory_space=pl.ANY)]
    args = [slots, slots, slots, x1, info, ys]
    if final_norm:
        in_specs.append(pl.BlockSpec((1, d), lambda i: (0, 0)))
        args.append(fn)
    return pl.pallas_call(
        functools.partial(_combine_kernel if final_norm else _combine_plain_kernel, tmc=tmc),
        grid=(n,),
        in_specs=in_specs,
        out_specs=pl.BlockSpec((2 * tmc, d), lambda i: (i, 0)),
        out_shape=jax.ShapeDtypeStruct((t, d), F32),
        scratch_shapes=[pltpu.VMEM((2 * tmc, d // 2), jnp.uint32), pltpu.VMEM((2 * tmc, d // 2), jnp.uint32),
                        pltpu.SemaphoreType.DMA((2,))],
        compiler_params=pltpu.CompilerParams(dimension_semantics=("arbitrary",), vmem_limit_bytes=VMEM_LIMIT),
        name="combine",
    )(*args)


def _moe(x1, h2p, info, meta, cnt, lw, fn, xs_init, *, tme, tmc, final_norm):
    t = x1.shape[0]
    n_tiles = (2 * t) // tme + N_EXPERTS
    counts = cnt[:, 0].astype(jnp.int32)
    padded = ((counts + tme - 1) // tme) * tme
    ends = jnp.cumsum(padded)
    starts = ends - padded
    mi = meta.astype(jnp.int32)
    expert_ids = jnp.arange(N_EXPERTS, dtype=jnp.int32)

    def slot_of(e, rank):
        start = jnp.sum(jnp.where(e[..., None] == expert_ids, starts, 0), axis=-1)
        return (start + rank).reshape(t // tmc, 1, tmc)

    slot1 = slot_of(mi[:, _M_E1], mi[:, _M_R1])
    slot2 = slot_of(mi[:, _M_E2], mi[:, _M_R2])
    slots = jnp.concatenate([slot1, slot2], axis=2)
    tile_start = jnp.arange(n_tiles, dtype=jnp.int32) * tme
    tile_expert = jnp.minimum(jnp.sum((ends[None, :] <= tile_start[:, None]).astype(jnp.int32), axis=1),
                              N_EXPERTS - 1)
    n_used = ends[-1:] // tme
    xs = _dispatch(h2p, slots, xs_init, td=tmc)
    ys = _experts(xs, tile_expert, n_used, lw, tme=tme)
    return _combine(x1, info, ys, slots, fn, tmc=tmc, final_norm=final_norm), xs


def _pack_layer(l, w_in, norm_mix, q_norm, w_q_up, kv_norm, w_kv_up, sg_norm, w_s, b_s, conv_w,
                w_a_out, w_b_out, w_c_out, w_o, norm_ffn, w_group_router, b_group_router,
                w_expert_router, b_expert_router, w_gate_up, w_down):
    d = w_in.shape[1]
    wi = w_in[l]
    o_kr = Q_RANK + KV_RANK
    kr_slot = jnp.zeros((d, HEAD_PAD), F32).at[:, QK_NOPE:QK_HEAD].set(wi[:, o_kr:o_kr + QK_ROPE])
    w_in_p = jnp.concatenate([wi[:, :o_kr], kr_slot, wi[:, o_kr + QK_ROPE:]], axis=1).astype(BF16)
    wq = w_q_up[l].reshape(Q_RANK, MLA_HEADS, QK_HEAD)
    wq = jnp.pad(wq, ((0, 0), (0, 0), (0, HEAD_PAD - QK_HEAD))).reshape(Q_RANK, MLA_HEADS * HEAD_PAD)
    wkv = w_kv_up[l].reshape(KV_RANK, MLA_HEADS, QK_NOPE + V_HEAD)
    wk = jnp.pad(wkv[:, :, :QK_NOPE], ((0, 0), (0, 0), (0, HEAD_PAD - QK_NOPE)))
    wk = wk.reshape(KV_RANK, MLA_HEADS * HEAD_PAD)
    wv = wkv[:, :, QK_NOPE:].reshape(KV_RANK, MLA_HEADS * V_HEAD)
    w_r = jnp.zeros((LANES, d), F32)
    w_r = w_r.at[:N_EXPERTS].set(w_expert_router[l].T)
    w_r = w_r.at[GROUP_ROW0:GROUP_ROW0 + N_GROUPS].set(w_group_router[l].T)
    b_r = jnp.zeros((LANES, 1), F32)
    b_r = b_r.at[:N_EXPERTS, 0].set(b_expert_router[l])
    b_r = b_r.at[GROUP_ROW0:GROUP_ROW0 + N_GROUPS, 0].set(b_group_router[l])
    return {
        "norm_mix": norm_mix[l][None, :], "w_in": w_in_p,
        "q_norm": q_norm[l][None, :], "w_q": wq.astype(BF16),
        "kv_norm": kv_norm[l][None, :], "w_k": wk.astype(BF16), "w_v": wv.astype(BF16),
        "sg_norm": sg_norm[l][None, :], "w_s": w_s[l],
        "b_s": jnp.broadcast_to(b_s[l][:, :, None], (SG_GROUPS, CHUNK, SG_GROUP_DIM)),
        "conv_w": conv_w[l],
        "w_a_out": w_a_out[l].astype(BF16), "w_b_out": w_b_out[l].astype(BF16),
        "w_c_out": w_c_out[l].astype(BF16), "w_o": w_o[l].astype(BF16),
        "norm_ffn": norm_ffn[l][None, :], "w_router_t": w_r.astype(BF16), "b_router_t": b_r,
        "w_gate_up": w_gate_up[l].astype(BF16), "w_down": w_down[l].astype(BF16),
    }


def _rope_tables(positions):
    inv = ROPE_BASE ** (-jnp.arange(0, QK_ROPE, 2, dtype=F32) / QK_ROPE)
    ang = inv[:, None] * positions.astype(F32).reshape(-1)[None, :]
    return jnp.cos(ang), jnp.sin(ang)


def kernel(x, positions, norm_mix, w_in, q_norm, w_q_up, kv_norm, w_kv_up, sg_norm, w_s, b_s, conv_w,
           w_a_out, w_b_out, w_c_out, w_o, norm_ffn, w_group_router, b_group_router, w_expert_router,
           b_expert_router, w_gate_up, w_down, final_norm):
    batch, seq, d = x.shape
    depth = w_in.shape[0]
    tm = min(512, seq)
    tq = tm
    tme = 512
    tmc = min(256, seq)
    cos_t, sin_t = _rope_tables(positions)
    x2 = x.reshape(batch * seq, d)
    fn = final_norm[None, :]
    xs = jnp.zeros((2 * batch * seq + N_EXPERTS * tme, d // 2), jnp.uint32)
    for l in range(depth):
        lw = _pack_layer(l, w_in, norm_mix, q_norm, w_q_up, kv_norm, w_kv_up, sg_norm, w_s, b_s, conv_w,
                         w_a_out, w_b_out, w_c_out, w_o, norm_ffn, w_group_router, b_group_router,
                         w_expert_router, b_expert_router, w_gate_up, w_down)
        q, k, v, part, ga = _mixer_pre(x2, lw, cos_t, sin_t, seq=seq, tm=tm)
        o = _attention(q, k, v, batch=batch, seq=seq, tq=tq)
        x1, h2p, info, meta, cnt = _mixer_post(o, part, ga, x2, lw, tm=tm)
        x2, xs = _moe(x1, h2p, info, meta, cnt, lw, fn, xs, tme=tme, tmc=tmc, final_norm=(l == depth - 1))
    return x2.reshape(batch, seq, d)
```

```python
import functools

import jax
import jax.numpy as jnp
from jax import lax
from jax.experimental import pallas as pl
from jax.experimental.pallas import tpu as pltpu

F32 = jnp.float32
BF16 = jnp.bfloat16

EPS = 1e-6
MLA_HEADS = 8
QK_NOPE = 64
QK_ROPE = 32
QK_HEAD = QK_NOPE + QK_ROPE
V_HEAD = 64
Q_RANK = 384
KV_RANK = 256
ROPE_BASE = 10000.0
SG_GROUPS = 4
SG_GROUP_DIM = 128
SG_WIDTH = SG_GROUPS * SG_GROUP_DIM
CHUNK = 128
CONV_WIDTH = 512
CONV_K = 3
N_GROUPS = 4
EXPERTS_PER_GROUP = 8
N_EXPERTS = N_GROUPS * EXPERTS_PER_GROUP
D_EXPERT = 256

LANES = 128
HEAD_PAD = 128
MASK_VALUE = -1e30
LOG2_E = 1.4426950408889634
ROW_CHUNKS = 4
VMEM_LIMIT = 56 * 1024 * 1024

_P1 = Q_RANK + KV_RANK + HEAD_PAD
_P2 = _P1 + 2 * SG_WIDTH
_P3 = _P2 + 3 * CONV_WIDTH
GROUP_ROW0 = N_EXPERTS


def _dot(a, b):
    return jnp.dot(a, b, preferred_element_type=F32)


def _rms(xf, g):
    return xf * lax.rsqrt(jnp.mean(xf * xf, axis=-1, keepdims=True) + EPS) * g


def _gelu(x):
    return 0.5 * x * (1.0 + jnp.tanh(0.7978845608028654 * (x + 0.044715 * (x * x * x))))


def _rope_t(tt, cos, sin):
    lo, mid = QK_NOPE, QK_NOPE + QK_ROPE // 2
    x1, x2 = tt[lo:mid], tt[mid:QK_HEAD]
    return jnp.concatenate([tt[:lo], x1 * cos - x2 * sin, x2 * cos + x1 * sin, tt[QK_HEAD:]], axis=0)


def _const_spec(shape):
    nd = len(shape)
    return pl.BlockSpec(shape, lambda *_: (0,) * nd, pipeline_mode=pl.Buffered(1))


def _mixer_pre_kernel(x_ref, nm_ref, win_ref, qn_ref, wq_ref, kvn_ref, wk_ref, wv_ref, sgn_ref,
                      ws_ref, bs_ref, cw_ref, wb_ref, wc_ref, cos_ref, sin_ref,
                      qt_ref, k_ref, vt_ref, part_ref, ga_ref,
                      halo_ref, ybin_ref, *, tiles_per_seq, tm):
    i = pl.program_id(0)
    d_model = x_ref.shape[1]
    h = _rms(x_ref[...], nm_ref[...]).astype(BF16)

    p1 = _dot(h, win_ref[:, 0:_P1])
    cos, sin = cos_ref[...], sin_ref[...]
    qn = _rms(p1[:, :Q_RANK], qn_ref[...]).astype(BF16)
    qf = _dot(qn, wq_ref[...])
    scale = QK_HEAD ** -0.5 * LOG2_E
    for hd in range(MLA_HEADS):
        sl = slice(hd * HEAD_PAD, (hd + 1) * HEAD_PAD)
        qt_ref[0, 0, sl, :] = (_rope_t(qf[:, sl].T, cos, sin) * scale).astype(BF16)
    kvn = _rms(p1[:, Q_RANK:Q_RANK + KV_RANK], kvn_ref[...]).astype(BF16)
    kr = _rope_t(p1[:, Q_RANK + KV_RANK:_P1].T, cos, sin).T
    kf = _dot(kvn, wk_ref[...])
    for hd in range(MLA_HEADS):
        sl = slice(hd * HEAD_PAD, (hd + 1) * HEAD_PAD)
        k_ref[:, sl] = (kf[:, sl] + kr).astype(BF16)
    vt_ref[0, 0] = _dot(kvn, wv_ref[...]).T.astype(BF16)

    p2 = _dot(h, win_ref[:, _P1:_P2])
    u = _gelu(p2[:, :SG_WIDTH])
    vb = _rms(_gelu(p2[:, SG_WIDTH:]), sgn_ref[...]).astype(BF16)
    r_i = lax.broadcasted_iota(jnp.int32, (CHUNK, CHUNK), 0)
    c_i = lax.broadcasted_iota(jnp.int32, (CHUNK, CHUNK), 1)
    for g in range(SG_GROUPS):
        wg = jnp.where(c_i <= r_i, ws_ref[g], 0.0).astype(BF16)
        gs = slice(g * SG_GROUP_DIM, (g + 1) * SG_GROUP_DIM)
        for c in range(tm // CHUNK):
            cs = slice(c * CHUNK, (c + 1) * CHUNK)
            mixed = _dot(wg, vb[cs, gs]) + bs_ref[g]
            ybin_ref[cs, gs] = (u[cs, gs] * mixed).astype(BF16)

    @pl.when(i % tiles_per_seq == 0)
    def _():
        halo_ref[...] = jnp.zeros_like(halo_ref)

    p3 = _dot(h, win_ref[:, _P2:_P3])
    z = p3[:, CONV_WIDTH:2 * CONV_WIDTH] * p3[:, 2 * CONV_WIDTH:]
    row = lax.broadcasted_iota(jnp.int32, z.shape, 0)
    halo = halo_ref[...]
    last1 = halo[7:8, :]
    last2 = halo[6:7, :]
    z1 = jnp.where(row == 0, last1, pltpu.roll(z, 1, 0))
    z2 = jnp.where(row == 0, last2, jnp.where(row == 1, last1, pltpu.roll(z, 2, 0)))
    y = cw_ref[0:1, :] * z2 + cw_ref[1:2, :] * z1 + cw_ref[2:3, :] * z
    yc_in = (p3[:, :CONV_WIDTH] * y).astype(BF16)
    halo_ref[...] = z[tm - 8:tm, :]

    p4 = _dot(h, win_ref[:, _P3:_P3 + 3 * d_model])
    yb = _dot(ybin_ref[...], wb_ref[...])
    yc = _dot(yc_in, wc_ref[...])
    part = (jax.nn.sigmoid(p4[:, d_model:2 * d_model]) * yb
            + jax.nn.sigmoid(p4[:, 2 * d_model:]) * yc)
    part_ref[...] = part.astype(BF16)
    ga_ref[...] = jax.nn.sigmoid(p4[:, :d_model]).astype(BF16)


def _mixer_pre(x2, lw, cos_t, sin_t, *, seq, tm):
    t, d = x2.shape
    row = lambda w: pl.BlockSpec((tm, w), lambda i: (i, 0))
    consts = [lw["norm_mix"], lw["w_in"], lw["q_norm"], lw["w_q"], lw["kv_norm"], lw["w_k"], lw["w_v"],
              lw["sg_norm"], lw["w_s"], lw["b_s"], lw["conv_w"], lw["w_b_out"], lw["w_c_out"]]
    tps = seq // tm
    kern = functools.partial(_mixer_pre_kernel, tiles_per_seq=tps, tm=tm)
    slab = lambda w: pl.BlockSpec((1, 1, w, tm), lambda i: (i // tps, i % tps, 0, 0))
    return pl.pallas_call(
        kern,
        grid=(t // tm,),
        in_specs=[row(d)] + [_const_spec(c.shape) for c in consts]
        + [pl.BlockSpec((QK_ROPE // 2, tm), lambda i: (0, i))] * 2,
        out_specs=[slab(MLA_HEADS * HEAD_PAD), row(MLA_HEADS * HEAD_PAD), slab(MLA_HEADS * V_HEAD), row(d), row(d)],
        out_shape=[jax.ShapeDtypeStruct((t // seq, tps, MLA_HEADS * HEAD_PAD, tm), BF16),
                   jax.ShapeDtypeStruct((t, MLA_HEADS * HEAD_PAD), BF16),
                   jax.ShapeDtypeStruct((t // seq, tps, MLA_HEADS * V_HEAD, tm), BF16),
                   jax.ShapeDtypeStruct((t, d), BF16),
                   jax.ShapeDtypeStruct((t, d), BF16)],
        scratch_shapes=[pltpu.VMEM((8, CONV_WIDTH), F32), pltpu.VMEM((tm, SG_WIDTH), BF16)],
        compiler_params=pltpu.CompilerParams(dimension_semantics=("arbitrary",), vmem_limit_bytes=VMEM_LIMIT),
        name="mixer_pre",
    )(x2, *consts, cos_t, sin_t)


ACC_ROWS = V_HEAD + 16


def _attn_kernel(qt_ref, k_ref, vt_ref, o_ref, m_ref, acc_ref, sa_ref, sb_ref, mca_ref, mcb_ref, *, tq, nq):
    bufs = ((sa_ref, mca_ref), (sb_ref, mcb_ref))

    def score(qi, j, buf, masked):
        s_ref, mc_ref = bufs[buf]
        start = pl.multiple_of(j * tq, tq)
        for sub in range(2):
            k_j = k_ref[pl.ds(start, tq), sub * HEAD_PAD:(sub + 1) * HEAD_PAD]
            st = _dot(k_j, qt_ref[0, qi, sub * HEAD_PAD:(sub + 1) * HEAD_PAD, :])
            if masked:
                kv_i = lax.broadcasted_iota(jnp.int32, st.shape, 0)
                q_i = lax.broadcasted_iota(jnp.int32, st.shape, 1)
                st = jnp.where(kv_i <= q_i, st, MASK_VALUE)
            s_ref[sub] = st
            mc_ref[sub] = jnp.broadcast_to(jnp.max(st, axis=0, keepdims=True), (8, tq))

    def value(j, buf):
        s_ref, mc_ref = bufs[buf]
        for sub in range(2):
            m_prev = m_ref[sub]
            m_next = jnp.maximum(m_prev, mc_ref[sub])
            alpha = jnp.exp2(m_prev - m_next)
            pt = jnp.exp2((s_ref[sub] - m_next[0:1, :]).astype(BF16))
            m_ref[sub] = m_next
            vt_j = jnp.concatenate([vt_ref[0, j, sub * V_HEAD:(sub + 1) * V_HEAD, :],
                                    jnp.ones((ACC_ROWS - V_HEAD, tq), BF16)], axis=0)
            acc_ref[sub] = alpha[0:1, :] * acc_ref[sub] + _dot(vt_j, pt)

    def q_tile(qi, carry):
        m_ref[...] = jnp.full_like(m_ref, MASK_VALUE)
        acc_ref[...] = jnp.zeros_like(acc_ref)

        @pl.when(qi == 0)
        def _():
            score(qi, 0, 0, True)
            value(0, 0)

        @pl.when(qi > 0)
        def _():
            score(qi, 0, 0, False)
            n_pairs = (qi - 1) // 2

            def body(p, c):
                score(qi, 2 * p + 1, 1, False)
                value(2 * p, 0)
                score(qi, 2 * p + 2, 0, False)
                value(2 * p + 1, 1)
                return c

            lax.fori_loop(0, n_pairs, body, 0)
            t0 = 2 * n_pairs

            @pl.when(qi % 2 == 1)
            def _():
                score(qi, qi, 1, True)
                value(t0, 0)
                value(qi, 1)

            @pl.when(qi % 2 == 0)
            def _():
                score(qi, t0 + 1, 1, False)
                value(t0, 0)
                score(qi, qi, 0, True)
                value(t0 + 1, 1)
                value(qi, 0)

        ot = jnp.concatenate([acc_ref[sub, 0:V_HEAD, :] / acc_ref[sub, V_HEAD:V_HEAD + 1, :]
                              for sub in range(2)], axis=0)
        o_ref[pl.ds(pl.multiple_of(qi * tq, tq), tq), :] = ot.T.astype(BF16)
        return carry

    lax.fori_loop(0, nq, q_tile, 0)


def _attention(qt, k, vt, *, batch, seq, tq):
    t = k.shape[0]
    nq = seq // tq
    pairs = MLA_HEADS // 2
    return pl.pallas_call(
        functools.partial(_attn_kernel, tq=tq, nq=nq),
        grid=(batch, pairs),
        in_specs=[pl.BlockSpec((1, nq, 2 * HEAD_PAD, tq), lambda b, hp: (b, 0, hp, 0)),
                  pl.BlockSpec((seq, 2 * HEAD_PAD), lambda b, hp: (b, hp)),
                  pl.BlockSpec((1, nq, 2 * V_HEAD, tq), lambda b, hp: (b, 0, hp, 0))],
        out_specs=pl.BlockSpec((seq, 2 * V_HEAD), lambda b, hp: (b, hp)),
        out_shape=jax.ShapeDtypeStruct((t, MLA_HEADS * V_HEAD), BF16),
        scratch_shapes=[pltpu.VMEM((2, 8, tq), F32), pltpu.VMEM((2, ACC_ROWS, tq), F32),
                        pltpu.VMEM((2, tq, tq), F32), pltpu.VMEM((2, tq, tq), F32),
                        pltpu.VMEM((2, 8, tq), F32), pltpu.VMEM((2, 8, tq), F32)],
        compiler_params=pltpu.CompilerParams(dimension_semantics=("arbitrary",) * 2, vmem_limit_bytes=VMEM_LIMIT),
        name="attention",
    )(qt, k, vt)


def _pack_pair(lo, hi):
    lo_b = lax.bitcast_convert_type(lo, jnp.uint32)
    hi_b = lax.bitcast_convert_type(hi, jnp.uint32)
    return (hi_b & jnp.uint32(0xFFFF0000)) | (lo_b >> 16)


def _unpack_pair(p):
    lo = lax.bitcast_convert_type(p << 16, F32)
    hi = lax.bitcast_convert_type(p & jnp.uint32(0xFFFF0000), F32)
    return lo, hi


def _store_rows(ref, first, n_rows, packed):
    for c in range(ROW_CHUNKS):
        ref[pl.ds(first * ROW_CHUNKS + c, n_rows, stride=ROW_CHUNKS), :] = packed[:, c * LANES:(c + 1) * LANES]


def _load_rows(ref, first, n_rows):
    return jnp.concatenate([ref[pl.ds(first * ROW_CHUNKS + c, n_rows, stride=ROW_CHUNKS), :]
                            for c in range(ROW_CHUNKS)], axis=1)


def _route_t(lt):
    big = float(LANES)
    g = lt[GROUP_ROW0:GROUP_ROW0 + 8]
    g_row = lax.broadcasted_iota(jnp.int32, g.shape, 0).astype(F32)
    g_log = jnp.where(g_row < float(N_GROUPS), g, MASK_VALUE)
    g_max = jnp.max(g_log, axis=0, keepdims=True)
    g_p = 1.0 / jnp.sum(jnp.exp(g_log - g_max), axis=0, keepdims=True)
    g_idx = jnp.min(jnp.where(g_log == g_max, g_row, big), axis=0, keepdims=True)
    e = lt[0:N_EXPERTS]
    e_row = lax.broadcasted_iota(jnp.int32, e.shape, 0).astype(F32)
    lo = g_idx * float(EXPERTS_PER_GROUP)
    in_grp = (e_row >= lo) & (e_row < lo + float(EXPERTS_PER_GROUP))
    e_log = jnp.where(in_grp, e, MASK_VALUE)
    e_max = jnp.max(e_log, axis=0, keepdims=True)
    e_exp = jnp.exp(e_log - e_max)
    e_prob = e_exp / jnp.sum(e_exp, axis=0, keepdims=True)
    cand = jnp.where(in_grp, e_prob, -1.0)
    p1 = jnp.max(cand, axis=0, keepdims=True)
    i1 = jnp.min(jnp.where(cand == p1, e_row, big), axis=0, keepdims=True)
    cand2 = jnp.where(e_row == i1, -1.0, cand)
    p2 = jnp.max(cand2, axis=0, keepdims=True)
    i2 = jnp.min(jnp.where(cand2 == p2, e_row, big), axis=0, keepdims=True)
    denom = p1 + p2
    return g_p * (p1 / denom), g_p * (p2 / denom), i1, i2, e_row


_L_W1, _L_W2 = 0, 1
_M_E1, _M_E2, _M_R1, _M_R2 = range(4)


def _mixer_post_kernel(o_ref, part_ref, ga_ref, x_ref, wa_ref, wo_ref, nf_ref, wrt_ref, brt_ref,
                       x1_ref, h2p_ref, info_ref, meta_ref, cnt_ref, run_ref, *, tm):
    i = pl.program_id(0)
    half = x_ref.shape[1] // 2

    @pl.when(i == 0)
    def _():
        run_ref[...] = jnp.zeros_like(run_ref)

    ya = _dot(o_ref[...], wa_ref[...])
    merged = ga_ref[...].astype(F32) * ya + part_ref[...].astype(F32)
    x1 = x_ref[...] + _dot(merged.astype(BF16), wo_ref[...])
    x1_ref[...] = x1
    h2 = _rms(x1, nf_ref[...]).astype(BF16)
    h2f = h2.astype(F32)
    _store_rows(h2p_ref, 0, tm, _pack_pair(h2f[:, :half], h2f[:, half:]))
    lt = lax.dot_general(wrt_ref[...], h2, (((1,), (1,)), ((), ())), preferred_element_type=F32) + brt_ref[...]
    w1, w2, i1, i2, e_row = _route_t(lt)

    oh1 = e_row == i1
    oh2 = e_row == i2
    both = jnp.where(oh1, 1.0, 0.0) + jnp.where(oh2, 1.0, 0.0)
    r_i = lax.broadcasted_iota(jnp.int32, (tm, tm), 0)
    c_i = lax.broadcasted_iota(jnp.int32, (tm, tm), 1)
    earlier = jnp.where(r_i < c_i, 1.0, 0.0).astype(BF16)
    run = run_ref[...]
    before = _dot(both.astype(BF16), earlier) + run[:, 0:1]
    rank1 = jnp.sum(jnp.where(oh1, before, 0.0), axis=0, keepdims=True)
    rank2 = jnp.sum(jnp.where(oh2, before, 0.0), axis=0, keepdims=True)
    run = run + jnp.sum(both, axis=1, keepdims=True)
    run_ref[...] = run
    cnt_ref[...] = run

    row8 = lax.broadcasted_iota(jnp.int32, (8, tm), 0)
    meta_ref[0] = jnp.where(row8 == _M_E1, i1, jnp.where(row8 == _M_E2, i2,
                            jnp.where(row8 == _M_R1, rank1, jnp.where(row8 == _M_R2, rank2, 0.0))))
    row128 = lax.broadcasted_iota(jnp.int32, (LANES, tm), 0)
    info_ref[...] = jnp.where(row128 == _L_W1, w1, jnp.where(row128 == _L_W2, w2, 0.0)).T


def _mixer_post(o, part, ga, x2, lw, *, tm):
    t, d = x2.shape
    row = lambda w: pl.BlockSpec((tm, w), lambda i: (i, 0))
    consts = [lw["w_a_out"], lw["w_o"], lw["norm_ffn"], lw["w_router_t"], lw["b_router_t"]]
    return pl.pallas_call(
        functools.partial(_mixer_post_kernel, tm=tm),
        grid=(t // tm,),
        in_specs=[row(o.shape[1]), row(d), row(d), row(d)] + [_const_spec(c.shape) for c in consts],
        out_specs=[row(d), pl.BlockSpec((tm * ROW_CHUNKS, LANES), lambda i: (i, 0)), row(LANES),
                   pl.BlockSpec((1, 8, tm), lambda i: (i, 0, 0)),
                   pl.BlockSpec((N_EXPERTS, LANES), lambda i: (0, 0))],
        out_shape=[jax.ShapeDtypeStruct((t, d), F32), jax.ShapeDtypeStruct((t * ROW_CHUNKS, LANES), jnp.uint32),
                   jax.ShapeDtypeStruct((t, LANES), F32), jax.ShapeDtypeStruct((t // tm, 8, tm), F32),
                   jax.ShapeDtypeStruct((N_EXPERTS, LANES), F32)],
        scratch_shapes=[pltpu.VMEM((N_EXPERTS, LANES), F32)],
        compiler_params=pltpu.CompilerParams(dimension_semantics=("arbitrary",), vmem_limit_bytes=VMEM_LIMIT),
        name="mixer_post",
    )(o, part, ga, x2, *consts)


DISPATCH_RING = 4


def _dispatch_kernel(slots_ref, h2p_hbm, xs_in_hbm, xs_hbm, buf, lsem, rsem, *, td):
    del xs_in_hbm
    i = pl.program_id(0)
    n = pl.num_programs(0)

    def load(tile, b):
        return pltpu.make_async_copy(h2p_hbm.at[pl.ds(tile * (td * ROW_CHUNKS), td * ROW_CHUNKS)], buf.at[b],
                                     lsem.at[b])

    def wait_rows(b):
        for _ in range(2):
            pltpu.make_async_copy(buf.at[b], xs_hbm.at[pl.ds(0, td * ROW_CHUNKS)], rsem.at[b]).wait()

    @pl.when(i == 0)
    def _():
        load(0, 0).start()
        load(1, 1).start()

    for k in range(DISPATCH_RING):
        tile = DISPATCH_RING * i + k
        load(tile, k).wait()
        for r in range(td):
            for a in range(2):
                dst = pl.multiple_of(slots_ref[0, 0, (2 * k + a) * td + r], ROW_CHUNKS)
                pltpu.make_async_copy(buf.at[k, pl.ds(r * ROW_CHUNKS, ROW_CHUNKS)],
                                      xs_hbm.at[pl.ds(dst, ROW_CHUNKS)], rsem.at[k]).start(priority=a)
        if k > 0:
            wait_rows(k - 1)
        else:
            @pl.when(i > 0)
            def _():
                wait_rows(DISPATCH_RING - 1)
        if k + 2 < DISPATCH_RING:
            load(tile + 2, k + 2).start()
        else:
            @pl.when(i + 1 < n)
            def _():
                load(tile + 2, (k + 2) % DISPATCH_RING).start()

    @pl.when(i == n - 1)
    def _():
        wait_rows(DISPATCH_RING - 1)


def _dispatch(h2p, slots, xs_init, *, td):
    t = h2p.shape[0] // ROW_CHUNKS
    steps = t // (DISPATCH_RING * td)
    return pl.pallas_call(
        functools.partial(_dispatch_kernel, td=td),
        grid=(steps,),
        in_specs=[pl.BlockSpec((1, 1, DISPATCH_RING * 2 * td), lambda i: (i, 0, 0), memory_space=pltpu.SMEM),
                  pl.BlockSpec(memory_space=pl.ANY),
                  pl.BlockSpec(memory_space=pl.ANY)],
        out_specs=pl.BlockSpec(memory_space=pl.ANY),
        out_shape=jax.ShapeDtypeStruct(xs_init.shape, jnp.uint32),
        scratch_shapes=[pltpu.VMEM((DISPATCH_RING, td * ROW_CHUNKS, LANES), jnp.uint32),
                        pltpu.SemaphoreType.DMA((DISPATCH_RING,)), pltpu.SemaphoreType.DMA((DISPATCH_RING,))],
        input_output_aliases={2: 0},
        compiler_params=pltpu.CompilerParams(dimension_semantics=("arbitrary",), vmem_limit_bytes=VMEM_LIMIT,
                                             has_side_effects=True),
        name="dispatch",
    )(slots.reshape(steps, 1, DISPATCH_RING * 2 * td), h2p, xs_init)


def _expert_kernel(te_ref, nu_ref, xs_ref, wgu_ref, wdn_ref, ys_ref, *, tme):
    i = pl.program_id(0)
    half = wgu_ref.shape[1] // 2

    @pl.when(i < nu_ref[0])
    def _():
        lo, hi = _unpack_pair(_load_rows(xs_ref, 0, tme))
        gu = _dot(lo.astype(BF16), wgu_ref[0, :half, :]) + _dot(hi.astype(BF16), wgu_ref[0, half:, :])
        act = (jax.nn.silu(gu[:, :D_EXPERT]) * gu[:, D_EXPERT:]).astype(BF16)
        y = _dot(act, wdn_ref[0]).astype(BF16).astype(F32)
        _store_rows(ys_ref, 0, tme, _pack_pair(y[:, :half], y[:, half:]))

    @pl.when(i >= nu_ref[0])
    def _():
        ys_ref[...] = jnp.zeros_like(ys_ref)


def _experts(xs, tile_expert, n_used, lw, *, tme):
    n_slots = xs.shape[0] // ROW_CHUNKS
    d = lw["w_gate_up"].shape[1]
    grid_spec = pltpu.PrefetchScalarGridSpec(
        num_scalar_prefetch=2,
        grid=(n_slots // tme,),
        in_specs=[pl.BlockSpec((tme * ROW_CHUNKS, LANES), lambda i, te, nu: (i, 0)),
                  pl.BlockSpec((1, d, 2 * D_EXPERT), lambda i, te, nu: (te[i], 0, 0)),
                  pl.BlockSpec((1, D_EXPERT, d), lambda i, te, nu: (te[i], 0, 0))],
        out_specs=pl.BlockSpec((tme * ROW_CHUNKS, LANES), lambda i, te, nu: (i, 0)),
    )
    return pl.pallas_call(
        functools.partial(_expert_kernel, tme=tme),
        grid_spec=grid_spec,
        out_shape=jax.ShapeDtypeStruct(xs.shape, jnp.uint32),
        compiler_params=pltpu.CompilerParams(dimension_semantics=("arbitrary",), vmem_limit_bytes=VMEM_LIMIT),
        name="experts",
    )(tile_expert, n_used, xs, lw["w_gate_up"], lw["w_down"])


def _combine_kernel(s0_ref, s1_ref, s2_ref, x1_ref, info_ref, ys_hbm, fn_ref, out_ref, ya, yb, sem, *, tmc):
    i = pl.program_id(0)
    n = pl.num_programs(0)
    half = x1_ref.shape[1] // 2

    def gather(idx_ref, buf, k):
        for r in range(2 * tmc):
            src = pl.multiple_of(idx_ref[0, 0, r], ROW_CHUNKS)
            pltpu.make_async_copy(ys_hbm.at[pl.ds(src, ROW_CHUNKS)], buf.at[pl.ds(r * ROW_CHUNKS, ROW_CHUNKS)],
                                  sem.at[k]).start(priority=r % 2)

    def finish(buf, k, rows):
        pltpu.make_async_copy(ys_hbm.at[pl.ds(0, 2 * tmc * ROW_CHUNKS)], buf, sem.at[k]).wait()
        lo1, hi1 = _unpack_pair(_load_rows(buf, 0, tmc))
        lo2, hi2 = _unpack_pair(_load_rows(buf, tmc, tmc))
        info = info_ref[rows, :]
        w1 = info[:, _L_W1:_L_W1 + 1]
        w2 = info[:, _L_W2:_L_W2 + 1]
        lo = x1_ref[rows, :half] + w1 * lo1 + w2 * lo2
        hi = x1_ref[rows, half:] + w1 * hi1 + w2 * hi2
        if fn_ref is not None:
            ms = (jnp.sum(lo * lo, axis=-1, keepdims=True) + jnp.sum(hi * hi, axis=-1, keepdims=True)) / (2 * half)
            inv = lax.rsqrt(ms + EPS)
            lo = lo * inv * fn_ref[:, :half]
            hi = hi * inv * fn_ref[:, half:]
        out_ref[rows, :half] = lo
        out_ref[rows, half:] = hi

    @pl.when(i == 0)
    def _():
        gather(s0_ref, ya, 0)

    gather(s1_ref, yb, 1)
    finish(ya, 0, slice(0, tmc))

    @pl.when(i + 1 < n)
    def _():
        gather(s2_ref, ya, 0)

    finish(yb, 1, slice(tmc, 2 * tmc))


def _combine_plain_kernel(s0_ref, s1_ref, s2_ref, x1_ref, info_ref, ys_hbm, out_ref, ya, yb, sem, *, tmc):
    _combine_kernel(s0_ref, s1_ref, s2_ref, x1_ref, info_ref, ys_hbm, None, out_ref, ya, yb, sem, tmc=tmc)


def _combine(x1, info, ys, slots, fn, *, tmc, final_norm):
    t, d = x1.shape
    n_tiles = t // tmc
    n = n_tiles // 2
    smem = lambda f: pl.BlockSpec((1, 1, 2 * tmc), f, memory_space=pltpu.SMEM)
    in_specs = [smem(lambda i: (2 * i, 0, 0)), smem(lambda i: (2 * i + 1, 0, 0)),
                smem(lambda i: (jnp.minimum(2 * i + 2, n_tiles - 1), 0, 0)),
                pl.BlockSpec((2 * tmc, d), lambda i: (i, 0)),
                pl.BlockSpec((2 * tmc, LANES), lambda i: (i, 0)),
                pl.BlockSpec(memory_space=pl.ANY)]
    args = [slots, slots, slots, x1, info, ys]
    if final_norm:
        in_specs.append(pl.BlockSpec((1, d), lambda i: (0, 0)))
        args.append(fn)
    return pl.pallas_call(
        functools.partial(_combine_kernel if final_norm else _combine_plain_kernel, tmc=tmc),
        grid=(n,),
        in_specs=in_specs,
        out_specs=pl.BlockSpec((2 * tmc, d), lambda i: (i, 0)),
        out_shape=jax.ShapeDtypeStruct((t, d), F32),
        scratch_shapes=[pltpu.VMEM((2 * tmc * ROW_CHUNKS, LANES), jnp.uint32),
                        pltpu.VMEM((2 * tmc * ROW_CHUNKS, LANES), jnp.uint32), pltpu.SemaphoreType.DMA((2,))],
        compiler_params=pltpu.CompilerParams(dimension_semantics=("arbitrary",), vmem_limit_bytes=VMEM_LIMIT),
        name="combine",
    )(*args)


def _moe(x1, h2p, info, meta, cnt, lw, fn, xs_init, *, tme, tmc, final_norm):
    t = x1.shape[0]
    n_tiles = (2 * t) // tme + N_EXPERTS
    counts = cnt[:, 0].astype(jnp.int32)
    padded = ((counts + tme - 1) // tme) * tme
    ends = jnp.cumsum(padded)
    starts = ends - padded
    mi = meta.astype(jnp.int32)
    expert_ids = jnp.arange(N_EXPERTS, dtype=jnp.int32)

    def slot_of(e, rank):
        start = jnp.sum(jnp.where(e[..., None] == expert_ids, starts, 0), axis=-1)
        return ((start + rank) * ROW_CHUNKS).reshape(t // tmc, 1, tmc)

    slot1 = slot_of(mi[:, _M_E1], mi[:, _M_R1])
    slot2 = slot_of(mi[:, _M_E2], mi[:, _M_R2])
    slots = jnp.concatenate([slot1, slot2], axis=2)
    tile_start = jnp.arange(n_tiles, dtype=jnp.int32) * tme
    tile_expert = jnp.minimum(jnp.sum((ends[None, :] <= tile_start[:, None]).astype(jnp.int32), axis=1),
                              N_EXPERTS - 1)
    n_used = ends[-1:] // tme
    xs = _dispatch(h2p, slots, xs_init, td=tmc)
    ys = _experts(xs, tile_expert, n_used, lw, tme=tme)
    return _combine(x1, info, ys, slots, fn, tmc=tmc, final_norm=final_norm), xs


def _pack_layer(l, w_in, norm_mix, q_norm, w_q_up, kv_norm, w_kv_up, sg_norm, w_s, b_s, conv_w,
                w_a_out, w_b_out, w_c_out, w_o, norm_ffn, w_group_router, b_group_router,
                w_expert_router, b_expert_router, w_gate_up, w_down):
    d = w_in.shape[1]
    wi = w_in[l]
    o_kr = Q_RANK + KV_RANK
    kr_slot = jnp.zeros((d, HEAD_PAD), F32).at[:, QK_NOPE:QK_HEAD].set(wi[:, o_kr:o_kr + QK_ROPE])
    w_in_p = jnp.concatenate([wi[:, :o_kr], kr_slot, wi[:, o_kr + QK_ROPE:]], axis=1).astype(BF16)
    wq = w_q_up[l].reshape(Q_RANK, MLA_HEADS, QK_HEAD)
    wq = jnp.pad(wq, ((0, 0), (0, 0), (0, HEAD_PAD - QK_HEAD))).reshape(Q_RANK, MLA_HEADS * HEAD_PAD)
    wkv = w_kv_up[l].reshape(KV_RANK, MLA_HEADS, QK_NOPE + V_HEAD)
    wk = jnp.pad(wkv[:, :, :QK_NOPE], ((0, 0), (0, 0), (0, HEAD_PAD - QK_NOPE)))
    wk = wk.reshape(KV_RANK, MLA_HEADS * HEAD_PAD)
    wv = wkv[:, :, QK_NOPE:].reshape(KV_RANK, MLA_HEADS * V_HEAD)
    w_r = jnp.zeros((LANES, d), F32)
    w_r = w_r.at[:N_EXPERTS].set(w_expert_router[l].T)
    w_r = w_r.at[GROUP_ROW0:GROUP_ROW0 + N_GROUPS].set(w_group_router[l].T)
    b_r = jnp.zeros((LANES, 1), F32)
    b_r = b_r.at[:N_EXPERTS, 0].set(b_expert_router[l])
    b_r = b_r.at[GROUP_ROW0:GROUP_ROW0 + N_GROUPS, 0].set(b_group_router[l])
    return {
        "norm_mix": norm_mix[l][None, :], "w_in": w_in_p,
        "q_norm": q_norm[l][None, :], "w_q": wq.astype(BF16),
        "kv_norm": kv_norm[l][None, :], "w_k": wk.astype(BF16), "w_v": wv.astype(BF16),
        "sg_norm": sg_norm[l][None, :], "w_s": w_s[l],
        "b_s": jnp.broadcast_to(b_s[l][:, :, None], (SG_GROUPS, CHUNK, SG_GROUP_DIM)),
        "conv_w": conv_w[l],
        "w_a_out": w_a_out[l].astype(BF16), "w_b_out": w_b_out[l].astype(BF16),
        "w_c_out": w_c_out[l].astype(BF16), "w_o": w_o[l].astype(BF16),
        "norm_ffn": norm_ffn[l][None, :], "w_router_t": w_r.astype(BF16), "b_router_t": b_r,
        "w_gate_up": w_gate_up[l].astype(BF16), "w_down": w_down[l].astype(BF16),
    }


def _rope_tables(positions):
    inv = ROPE_BASE ** (-jnp.arange(0, QK_ROPE, 2, dtype=F32) / QK_ROPE)
    ang = inv[:, None] * positions.astype(F32).reshape(-1)[None, :]
    return jnp.cos(ang), jnp.sin(ang)


def kernel(x, positions, norm_mix, w_in, q_norm, w_q_up, kv_norm, w_kv_up, sg_norm, w_s, b_s, conv_w,
           w_a_out, w_b_out, w_c_out, w_o, norm_ffn, w_group_router, b_group_router, w_expert_router,
           b_expert_router, w_gate_up, w_down, final_norm):
    batch, seq, d = x.shape
    depth = w_in.shape[0]
    tm = min(512, seq)
    tq = tm
    tme = 512
    tmc = min(256, seq)
    cos_t, sin_t = _rope_tables(positions)
    x2 = x.reshape(batch * seq, d)
    fn = final_norm[None, :]
    xs = jnp.zeros(((2 * batch * seq + N_EXPERTS * tme) * ROW_CHUNKS, LANES), jnp.uint32)
    for l in range(depth):
        lw = _pack_layer(l, w_in, norm_mix, q_norm, w_q_up, kv_norm, w_kv_up, sg_norm, w_s, b_s, conv_w,
                         w_a_out, w_b_out, w_c_out, w_o, norm_ffn, w_group_router, b_group_router,
                         w_expert_router, b_expert_router, w_gate_up, w_down)
        q, k, v, part, ga = _mixer_pre(x2, lw, cos_t, sin_t, seq=seq, tm=tm)
        o = _attention(q, k, v, batch=batch, seq=seq, tq=tq)
        x1, h2p, info, meta, cnt = _mixer_post(o, part, ga, x2, lw, tm=tm)
        x2, xs = _moe(x1, h2p, info, meta, cnt, lw, fn, xs, tme=tme, tmc=tmc, final_norm=(l == depth - 1))
    return x2.reshape(batch, seq, d)
```

```python
import functools

import jax
import jax.numpy as jnp
from jax import lax
from jax.experimental import pallas as pl
from jax.experimental.pallas import tpu as pltpu

F32 = jnp.float32
BF16 = jnp.bfloat16

EPS = 1e-6
MLA_HEADS = 8
QK_NOPE = 64
QK_ROPE = 32
QK_HEAD = QK_NOPE + QK_ROPE
V_HEAD = 64
Q_RANK = 384
KV_RANK = 256
ROPE_BASE = 10000.0
SG_GROUPS = 4
SG_GROUP_DIM = 128
SG_WIDTH = SG_GROUPS * SG_GROUP_DIM
CHUNK = 128
CONV_WIDTH = 512
CONV_K = 3
N_GROUPS = 4
EXPERTS_PER_GROUP = 8
N_EXPERTS = N_GROUPS * EXPERTS_PER_GROUP
D_EXPERT = 256

LANES = 128
HEAD_PAD = 128
MASK_VALUE = -1e30
LOG2_E = 1.4426950408889634
ROW_CHUNKS = 4
VMEM_LIMIT = 56 * 1024 * 1024

_P1 = Q_RANK + KV_RANK + HEAD_PAD
_P2 = _P1 + 2 * SG_WIDTH
_P3 = _P2 + 3 * CONV_WIDTH
GROUP_ROW0 = N_EXPERTS


def _dot(a, b):
    return jnp.dot(a, b, preferred_element_type=F32)


def _rms(xf, g):
    return xf * lax.rsqrt(jnp.mean(xf * xf, axis=-1, keepdims=True) + EPS) * g


def _gelu(x):
    return 0.5 * x * (1.0 + jnp.tanh(0.7978845608028654 * (x + 0.044715 * (x * x * x))))


def _rope_t(tt, cos, sin):
    lo, mid = QK_NOPE, QK_NOPE + QK_ROPE // 2
    x1, x2 = tt[lo:mid], tt[mid:QK_HEAD]
    return jnp.concatenate([tt[:lo], x1 * cos - x2 * sin, x2 * cos + x1 * sin, tt[QK_HEAD:]], axis=0)


def _const_spec(shape):
    nd = len(shape)
    return pl.BlockSpec(shape, lambda *_: (0,) * nd, pipeline_mode=pl.Buffered(1))


def _mixer_pre_kernel(x_ref, nm_ref, win_ref, qn_ref, wq_ref, kvn_ref, wk_ref, wv_ref, sgn_ref,
                      ws_ref, bs_ref, cw_ref, wb_ref, wc_ref, cos_ref, sin_ref,
                      qt_ref, k_ref, vt_ref, part_ref, ga_ref,
                      halo_ref, ybin_ref, *, tiles_per_seq, tm):
    i = pl.program_id(0)
    d_model = x_ref.shape[1]
    h = _rms(x_ref[...], nm_ref[...]).astype(BF16)

    p1 = _dot(h, win_ref[:, 0:_P1])
    cos, sin = cos_ref[...], sin_ref[...]
    qn = _rms(p1[:, :Q_RANK], qn_ref[...]).astype(BF16)
    qf = _dot(qn, wq_ref[...])
    scale = QK_HEAD ** -0.5 * LOG2_E
    for hd in range(MLA_HEADS):
        sl = slice(hd * HEAD_PAD, (hd + 1) * HEAD_PAD)
        qt_ref[0, 0, sl, :] = (_rope_t(qf[:, sl].T, cos, sin) * scale).astype(BF16)
    kvn = _rms(p1[:, Q_RANK:Q_RANK + KV_RANK], kvn_ref[...]).astype(BF16)
    kr = _rope_t(p1[:, Q_RANK + KV_RANK:_P1].T, cos, sin).T
    kf = _dot(kvn, wk_ref[...])
    for hd in range(MLA_HEADS):
        sl = slice(hd * HEAD_PAD, (hd + 1) * HEAD_PAD)
        k_ref[:, sl] = (kf[:, sl] + kr).astype(BF16)
    vt_ref[0, 0] = _dot(kvn, wv_ref[...]).T.astype(BF16)

    p2 = _dot(h, win_ref[:, _P1:_P2])
    u = _gelu(p2[:, :SG_WIDTH])
    vb = _rms(_gelu(p2[:, SG_WIDTH:]), sgn_ref[...]).astype(BF16)
    r_i = lax.broadcasted_iota(jnp.int32, (CHUNK, CHUNK), 0)
    c_i = lax.broadcasted_iota(jnp.int32, (CHUNK, CHUNK), 1)
    for g in range(SG_GROUPS):
        wg = jnp.where(c_i <= r_i, ws_ref[g], 0.0).astype(BF16)
        gs = slice(g * SG_GROUP_DIM, (g + 1) * SG_GROUP_DIM)
        for c in range(tm // CHUNK):
            cs = slice(c * CHUNK, (c + 1) * CHUNK)
            mixed = _dot(wg, vb[cs, gs]) + bs_ref[g]
            ybin_ref[cs, gs] = (u[cs, gs] * mixed).astype(BF16)

    @pl.when(i % tiles_per_seq == 0)
    def _():
        halo_ref[...] = jnp.zeros_like(halo_ref)

    p3 = _dot(h, win_ref[:, _P2:_P3])
    z = p3[:, CONV_WIDTH:2 * CONV_WIDTH] * p3[:, 2 * CONV_WIDTH:]
    row = lax.broadcasted_iota(jnp.int32, z.shape, 0)
    halo = halo_ref[...]
    last1 = halo[7:8, :]
    last2 = halo[6:7, :]
    z1 = jnp.where(row == 0, last1, pltpu.roll(z, 1, 0))
    z2 = jnp.where(row == 0, last2, jnp.where(row == 1, last1, pltpu.roll(z, 2, 0)))
    y = cw_ref[0:1, :] * z2 + cw_ref[1:2, :] * z1 + cw_ref[2:3, :] * z
    yc_in = (p3[:, :CONV_WIDTH] * y).astype(BF16)
    halo_ref[...] = z[tm - 8:tm, :]

    p4 = _dot(h, win_ref[:, _P3:_P3 + 3 * d_model])
    yb = _dot(ybin_ref[...], wb_ref[...])
    yc = _dot(yc_in, wc_ref[...])
    part = (jax.nn.sigmoid(p4[:, d_model:2 * d_model]) * yb
            + jax.nn.sigmoid(p4[:, 2 * d_model:]) * yc)
    part_ref[...] = part.astype(BF16)
    ga_ref[...] = jax.nn.sigmoid(p4[:, :d_model]).astype(BF16)


def _mixer_pre(x2, lw, cos_t, sin_t, *, seq, tm):
    t, d = x2.shape
    row = lambda w: pl.BlockSpec((tm, w), lambda i: (i, 0))
    consts = [lw["norm_mix"], lw["w_in"], lw["q_norm"], lw["w_q"], lw["kv_norm"], lw["w_k"], lw["w_v"],
              lw["sg_norm"], lw["w_s"], lw["b_s"], lw["conv_w"], lw["w_b_out"], lw["w_c_out"]]
    tps = seq // tm
    kern = functools.partial(_mixer_pre_kernel, tiles_per_seq=tps, tm=tm)
    slab = lambda w: pl.BlockSpec((1, 1, w, tm), lambda i: (i // tps, i % tps, 0, 0))
    return pl.pallas_call(
        kern,
        grid=(t // tm,),
        in_specs=[row(d)] + [_const_spec(c.shape) for c in consts]
        + [pl.BlockSpec((QK_ROPE // 2, tm), lambda i: (0, i))] * 2,
        out_specs=[slab(MLA_HEADS * HEAD_PAD), row(MLA_HEADS * HEAD_PAD), slab(MLA_HEADS * V_HEAD), row(d), row(d)],
        out_shape=[jax.ShapeDtypeStruct((t // seq, tps, MLA_HEADS * HEAD_PAD, tm), BF16),
                   jax.ShapeDtypeStruct((t, MLA_HEADS * HEAD_PAD), BF16),
                   jax.ShapeDtypeStruct((t // seq, tps, MLA_HEADS * V_HEAD, tm), BF16),
                   jax.ShapeDtypeStruct((t, d), BF16),
                   jax.ShapeDtypeStruct((t, d), BF16)],
        scratch_shapes=[pltpu.VMEM((8, CONV_WIDTH), F32), pltpu.VMEM((tm, SG_WIDTH), BF16)],
        compiler_params=pltpu.CompilerParams(dimension_semantics=("arbitrary",), vmem_limit_bytes=VMEM_LIMIT),
        name="mixer_pre",
    )(x2, *consts, cos_t, sin_t)


ACC_ROWS = V_HEAD + 16


def _attn_kernel(qt_ref, k_ref, vt_ref, o_ref, m_ref, acc_ref, *bufs, tq, nq):
    (sa, mca), (sb, mcb), (sf0, mcf0), (sf1, mcf1) = [bufs[2 * n:2 * n + 2] for n in range(4)]
    buf_a, buf_b, first = (sa, mca), (sb, mcb), ((sf0, mcf0), (sf1, mcf1))

    def score(qi, j, buf, masked):
        s_ref, mc_ref = buf
        start = pl.multiple_of(j * tq, tq)
        for sub in range(2):
            k_j = k_ref[pl.ds(start, tq), sub * HEAD_PAD:(sub + 1) * HEAD_PAD]
            st = _dot(k_j, qt_ref[0, qi, sub * HEAD_PAD:(sub + 1) * HEAD_PAD, :])
            if masked:
                kv_i = lax.broadcasted_iota(jnp.int32, st.shape, 0)
                q_i = lax.broadcasted_iota(jnp.int32, st.shape, 1)
                st = jnp.where(kv_i <= q_i, st, MASK_VALUE)
            s_ref[sub] = st
            mc_ref[sub] = jnp.broadcast_to(jnp.max(st, axis=0, keepdims=True), (8, tq))

    def value(j, buf):
        s_ref, mc_ref = buf
        for sub in range(2):
            m_prev = m_ref[sub]
            m_next = jnp.maximum(m_prev, mc_ref[sub])
            alpha = jnp.exp2(m_prev - m_next)
            pt = jnp.exp2((s_ref[sub] - m_next[0:1, :]).astype(BF16))
            m_ref[sub] = m_next
            vt_j = jnp.concatenate([vt_ref[0, j, sub * V_HEAD:(sub + 1) * V_HEAD, :],
                                    jnp.ones((ACC_ROWS - V_HEAD, tq), BF16)], axis=0)
            acc_ref[sub] = alpha[0:1, :] * acc_ref[sub] + _dot(vt_j, pt)

    def q_tile(qi, par):
        cur, nxt = first[par], first[1 - par]
        m_ref[...] = jnp.full_like(m_ref, MASK_VALUE)
        acc_ref[...] = jnp.zeros_like(acc_ref)

        def prefetch():
            score(jnp.minimum(qi + 1, nq - 1), 0, nxt, False)

        def ramp(first_tail_tile):
            score(qi, 1, buf_a, False)
            value(0, cur)

            def body(p, c):
                score(qi, 2 * p + 2, buf_b, False)
                value(2 * p + 1, buf_a)
                score(qi, 2 * p + 3, buf_a, False)
                value(2 * p + 2, buf_b)
                return c

            lax.fori_loop(0, (first_tail_tile - 1) // 2, body, 0)

        if par == 0:
            @pl.when(qi == 0)
            def _():
                prefetch()
                value(0, cur)

            @pl.when(qi >= 2)
            def _():
                ramp(qi - 1)
                score(qi, qi, buf_b, True)
                value(qi - 1, buf_a)
                prefetch()
                value(qi, buf_b)
        else:
            @pl.when(qi == 1)
            def _():
                score(qi, 1, buf_a, True)
                value(0, cur)
                prefetch()
                value(1, buf_a)

            @pl.when(qi >= 3)
            def _():
                ramp(qi - 2)
                score(qi, qi - 1, buf_b, False)
                value(qi - 2, buf_a)
                score(qi, qi, buf_a, True)
                value(qi - 1, buf_b)
                prefetch()
                value(qi, buf_a)

        ot = jnp.concatenate([acc_ref[sub, 0:V_HEAD, :] / acc_ref[sub, V_HEAD:V_HEAD + 1, :]
                              for sub in range(2)], axis=0)
        o_ref[pl.ds(pl.multiple_of(qi * tq, tq), tq), :] = ot.T.astype(BF16)

    score(0, 0, first[0], True)

    def q_pair(p, carry):
        q_tile(2 * p, 0)
        q_tile(2 * p + 1, 1)
        return carry

    lax.fori_loop(0, nq // 2, q_pair, 0)

def _attention(qt, k, vt, *, batch, seq, tq):
    t = k.shape[0]
    nq = seq // tq
    assert nq % 2 == 0
    pairs = MLA_HEADS // 2
    score_buf = [pltpu.VMEM((2, tq, tq), F32), pltpu.VMEM((2, 8, tq), F32)]
    return pl.pallas_call(
        functools.partial(_attn_kernel, tq=tq, nq=nq),
        grid=(batch, pairs),
        in_specs=[pl.BlockSpec((1, nq, 2 * HEAD_PAD, tq), lambda b, hp: (b, 0, hp, 0)),
                  pl.BlockSpec((seq, 2 * HEAD_PAD), lambda b, hp: (b, hp)),
                  pl.BlockSpec((1, nq, 2 * V_HEAD, tq), lambda b, hp: (b, 0, hp, 0))],
        out_specs=pl.BlockSpec((seq, 2 * V_HEAD), lambda b, hp: (b, hp)),
        out_shape=jax.ShapeDtypeStruct((t, MLA_HEADS * V_HEAD), BF16),
        scratch_shapes=[pltpu.VMEM((2, 8, tq), F32), pltpu.VMEM((2, ACC_ROWS, tq), F32)] + score_buf * 4,
        compiler_params=pltpu.CompilerParams(dimension_semantics=("arbitrary",) * 2, vmem_limit_bytes=VMEM_LIMIT),
        name="attention",
    )(qt, k, vt)


def _pack_pair(lo, hi):
    lo_b = lax.bitcast_convert_type(lo, jnp.uint32)
    hi_b = lax.bitcast_convert_type(hi, jnp.uint32)
    return (hi_b & jnp.uint32(0xFFFF0000)) | (lo_b >> 16)


def _unpack_pair(p):
    lo = lax.bitcast_convert_type(p << 16, F32)
    hi = lax.bitcast_convert_type(p & jnp.uint32(0xFFFF0000), F32)
    return lo, hi


def _store_rows(ref, first, n_rows, packed):
    for c in range(ROW_CHUNKS):
        ref[pl.ds(first * ROW_CHUNKS + c, n_rows, stride=ROW_CHUNKS), :] = packed[:, c * LANES:(c + 1) * LANES]


def _load_rows(ref, first, n_rows):
    return jnp.concatenate([ref[pl.ds(first * ROW_CHUNKS + c, n_rows, stride=ROW_CHUNKS), :]
                            for c in range(ROW_CHUNKS)], axis=1)


def _route_t(lt):
    big = float(LANES)
    g = lt[GROUP_ROW0:GROUP_ROW0 + 8]
    g_row = lax.broadcasted_iota(jnp.int32, g.shape, 0).astype(F32)
    g_log = jnp.where(g_row < float(N_GROUPS), g, MASK_VALUE)
    g_max = jnp.max(g_log, axis=0, keepdims=True)
    g_p = 1.0 / jnp.sum(jnp.exp(g_log - g_max), axis=0, keepdims=True)
    g_idx = jnp.min(jnp.where(g_log == g_max, g_row, big), axis=0, keepdims=True)
    e = lt[0:N_EXPERTS]
    e_row = lax.broadcasted_iota(jnp.int32, e.shape, 0).astype(F32)
    lo = g_idx * float(EXPERTS_PER_GROUP)
    in_grp = (e_row >= lo) & (e_row < lo + float(EXPERTS_PER_GROUP))
    e_log = jnp.where(in_grp, e, MASK_VALUE)
    e_max = jnp.max(e_log, axis=0, keepdims=True)
    e_exp = jnp.exp(e_log - e_max)
    e_prob = e_exp / jnp.sum(e_exp, axis=0, keepdims=True)
    cand = jnp.where(in_grp, e_prob, -1.0)
    p1 = jnp.max(cand, axis=0, keepdims=True)
    i1 = jnp.min(jnp.where(cand == p1, e_row, big), axis=0, keepdims=True)
    cand2 = jnp.where(e_row == i1, -1.0, cand)
    p2 = jnp.max(cand2, axis=0, keepdims=True)
    i2 = jnp.min(jnp.where(cand2 == p2, e_row, big), axis=0, keepdims=True)
    denom = p1 + p2
    return g_p * (p1 / denom), g_p * (p2 / denom), i1, i2, e_row


_L_W1, _L_W2 = 0, 1
_M_E1, _M_E2, _M_R1, _M_R2 = range(4)


def _mixer_post_kernel(o_ref, part_ref, ga_ref, x_ref, wa_ref, wo_ref, nf_ref, wrt_ref, brt_ref,
                       x1_ref, h2p_ref, info_ref, meta_ref, cnt_ref, run_ref, *, tm):
    i = pl.program_id(0)
    half = x_ref.shape[1] // 2

    @pl.when(i == 0)
    def _():
        run_ref[...] = jnp.zeros_like(run_ref)

    ya = _dot(o_ref[...], wa_ref[...])
    merged = ga_ref[...].astype(F32) * ya + part_ref[...].astype(F32)
    x1 = x_ref[...] + _dot(merged.astype(BF16), wo_ref[...])
    x1_ref[...] = x1
    h2 = _rms(x1, nf_ref[...]).astype(BF16)
    h2f = h2.astype(F32)
    _store_rows(h2p_ref, 0, tm, _pack_pair(h2f[:, :half], h2f[:, half:]))
    lt = lax.dot_general(wrt_ref[...], h2, (((1,), (1,)), ((), ())), preferred_element_type=F32) + brt_ref[...]
    w1, w2, i1, i2, e_row = _route_t(lt)

    oh1 = e_row == i1
    oh2 = e_row == i2
    both = jnp.where(oh1, 1.0, 0.0) + jnp.where(oh2, 1.0, 0.0)
    r_i = lax.broadcasted_iota(jnp.int32, (tm, tm), 0)
    c_i = lax.broadcasted_iota(jnp.int32, (tm, tm), 1)
    earlier = jnp.where(r_i < c_i, 1.0, 0.0).astype(BF16)
    run = run_ref[...]
    before = _dot(both.astype(BF16), earlier) + run[:, 0:1]
    rank1 = jnp.sum(jnp.where(oh1, before, 0.0), axis=0, keepdims=True)
    rank2 = jnp.sum(jnp.where(oh2, before, 0.0), axis=0, keepdims=True)
    run = run + jnp.sum(both, axis=1, keepdims=True)
    run_ref[...] = run
    cnt_ref[...] = run

    row8 = lax.broadcasted_iota(jnp.int32, (8, tm), 0)
    meta_ref[0] = jnp.where(row8 == _M_E1, i1, jnp.where(row8 == _M_E2, i2,
                            jnp.where(row8 == _M_R1, rank1, jnp.where(row8 == _M_R2, rank2, 0.0))))
    row128 = lax.broadcasted_iota(jnp.int32, (LANES, tm), 0)
    info_ref[...] = jnp.where(row128 == _L_W1, w1, jnp.where(row128 == _L_W2, w2, 0.0)).T


def _mixer_post(o, part, ga, x2, lw, *, tm):
    t, d = x2.shape
    row = lambda w: pl.BlockSpec((tm, w), lambda i: (i, 0))
    consts = [lw["w_a_out"], lw["w_o"], lw["norm_ffn"], lw["w_router_t"], lw["b_router_t"]]
    return pl.pallas_call(
        functools.partial(_mixer_post_kernel, tm=tm),
        grid=(t // tm,),
        in_specs=[row(o.shape[1]), row(d), row(d), row(d)] + [_const_spec(c.shape) for c in consts],
        out_specs=[row(d), pl.BlockSpec((tm * ROW_CHUNKS, LANES), lambda i: (i, 0)), row(LANES),
                   pl.BlockSpec((1, 8, tm), lambda i: (i, 0, 0)),
                   pl.BlockSpec((N_EXPERTS, LANES), lambda i: (0, 0))],
        out_shape=[jax.ShapeDtypeStruct((t, d), F32), jax.ShapeDtypeStruct((t * ROW_CHUNKS, LANES), jnp.uint32),
                   jax.ShapeDtypeStruct((t, LANES), F32), jax.ShapeDtypeStruct((t // tm, 8, tm), F32),
                   jax.ShapeDtypeStruct((N_EXPERTS, LANES), F32)],
        scratch_shapes=[pltpu.VMEM((N_EXPERTS, LANES), F32)],
        compiler_params=pltpu.CompilerParams(dimension_semantics=("arbitrary",), vmem_limit_bytes=VMEM_LIMIT),
        name="mixer_post",
    )(o, part, ga, x2, *consts)


DISPATCH_RING = 4


def _dispatch_kernel(slots_ref, h2p_hbm, xs_in_hbm, xs_hbm, buf, lsem, rsem, *, td):
    del xs_in_hbm
    i = pl.program_id(0)
    n = pl.num_programs(0)

    def load(tile, b):
        return pltpu.make_async_copy(h2p_hbm.at[pl.ds(tile * (td * ROW_CHUNKS), td * ROW_CHUNKS)], buf.at[b],
                                     lsem.at[b])

    def wait_rows(b):
        for _ in range(2):
            pltpu.make_async_copy(buf.at[b], xs_hbm.at[pl.ds(0, td * ROW_CHUNKS)], rsem.at[b]).wait()

    @pl.when(i == 0)
    def _():
        load(0, 0).start()
        load(1, 1).start()

    for k in range(DISPATCH_RING):
        tile = DISPATCH_RING * i + k
        load(tile, k).wait()
        for r in range(td):
            for a in range(2):
                dst = pl.multiple_of(slots_ref[0, 0, (2 * k + a) * td + r], ROW_CHUNKS)
                pltpu.make_async_copy(buf.at[k, pl.ds(r * ROW_CHUNKS, ROW_CHUNKS)],
                                      xs_hbm.at[pl.ds(dst, ROW_CHUNKS)], rsem.at[k]).start(priority=a)
        if k > 0:
            wait_rows(k - 1)
        else:
            @pl.when(i > 0)
            def _():
                wait_rows(DISPATCH_RING - 1)
        if k + 2 < DISPATCH_RING:
            load(tile + 2, k + 2).start()
        else:
            @pl.when(i + 1 < n)
            def _():
                load(tile + 2, (k + 2) % DISPATCH_RING).start()

    @pl.when(i == n - 1)
    def _():
        wait_rows(DISPATCH_RING - 1)


def _dispatch(h2p, slots, xs_init, *, td):
    t = h2p.shape[0] // ROW_CHUNKS
    steps = t // (DISPATCH_RING * td)
    return pl.pallas_call(
        functools.partial(_dispatch_kernel, td=td),
        grid=(steps,),
        in_specs=[pl.BlockSpec((1, 1, DISPATCH_RING * 2 * td), lambda i: (i, 0, 0), memory_space=pltpu.SMEM),
                  pl.BlockSpec(memory_space=pl.ANY),
                  pl.BlockSpec(memory_space=pl.ANY)],
        out_specs=pl.BlockSpec(memory_space=pl.ANY),
        out_shape=jax.ShapeDtypeStruct(xs_init.shape, jnp.uint32),
        scratch_shapes=[pltpu.VMEM((DISPATCH_RING, td * ROW_CHUNKS, LANES), jnp.uint32),
                        pltpu.SemaphoreType.DMA((DISPATCH_RING,)), pltpu.SemaphoreType.DMA((DISPATCH_RING,))],
        input_output_aliases={2: 0},
        compiler_params=pltpu.CompilerParams(dimension_semantics=("arbitrary",), vmem_limit_bytes=VMEM_LIMIT,
                                             has_side_effects=True),
        name="dispatch",
    )(slots.reshape(steps, 1, DISPATCH_RING * 2 * td), h2p, xs_init)


def _expert_kernel(te_ref, nu_ref, xs_ref, wgu_ref, wdn_ref, ys_ref, *, tme):
    i = pl.program_id(0)
    half = wgu_ref.shape[1] // 2

    @pl.when(i < nu_ref[0])
    def _():
        lo, hi = _unpack_pair(_load_rows(xs_ref, 0, tme))
        gu = _dot(lo.astype(BF16), wgu_ref[0, :half, :]) + _dot(hi.astype(BF16), wgu_ref[0, half:, :])
        act = (jax.nn.silu(gu[:, :D_EXPERT]) * gu[:, D_EXPERT:]).astype(BF16)
        y = _dot(act, wdn_ref[0]).astype(BF16).astype(F32)
        _store_rows(ys_ref, 0, tme, _pack_pair(y[:, :half], y[:, half:]))

    @pl.when(i >= nu_ref[0])
    def _():
        ys_ref[...] = jnp.zeros_like(ys_ref)


def _experts(xs, tile_expert, n_used, lw, *, tme):
    n_slots = xs.shape[0] // ROW_CHUNKS
    d = lw["w_gate_up"].shape[1]
    grid_spec = pltpu.PrefetchScalarGridSpec(
        num_scalar_prefetch=2,
        grid=(n_slots // tme,),
        in_specs=[pl.BlockSpec((tme * ROW_CHUNKS, LANES), lambda i, te, nu: (i, 0)),
                  pl.BlockSpec((1, d, 2 * D_EXPERT), lambda i, te, nu: (te[i], 0, 0)),
                  pl.BlockSpec((1, D_EXPERT, d), lambda i, te, nu: (te[i], 0, 0))],
        out_specs=pl.BlockSpec((tme * ROW_CHUNKS, LANES), lambda i, te, nu: (i, 0)),
    )
    return pl.pallas_call(
        functools.partial(_expert_kernel, tme=tme),
        grid_spec=grid_spec,
        out_shape=jax.ShapeDtypeStruct(xs.shape, jnp.uint32),
        compiler_params=pltpu.CompilerParams(dimension_semantics=("arbitrary",), vmem_limit_bytes=VMEM_LIMIT),
        name="experts",
    )(tile_expert, n_used, xs, lw["w_gate_up"], lw["w_down"])


def _combine_kernel(s0_ref, s1_ref, s2_ref, x1_ref, info_ref, ys_hbm, fn_ref, out_ref, ya, yb, sem, *, tmc):
    i = pl.program_id(0)
    n = pl.num_programs(0)
    half = x1_ref.shape[1] // 2

    def gather(idx_ref, buf, k):
        for r in range(2 * tmc):
            src = pl.multiple_of(idx_ref[0, 0, r], ROW_CHUNKS)
            pltpu.make_async_copy(ys_hbm.at[pl.ds(src, ROW_CHUNKS)], buf.at[pl.ds(r * ROW_CHUNKS, ROW_CHUNKS)],
                                  sem.at[k]).start(priority=r % 2)

    def finish(buf, k, rows):
        pltpu.make_async_copy(ys_hbm.at[pl.ds(0, 2 * tmc * ROW_CHUNKS)], buf, sem.at[k]).wait()
        lo1, hi1 = _unpack_pair(_load_rows(buf, 0, tmc))
        lo2, hi2 = _unpack_pair(_load_rows(buf, tmc, tmc))
        info = info_ref[rows, :]
        w1 = info[:, _L_W1:_L_W1 + 1]
        w2 = info[:, _L_W2:_L_W2 + 1]
        lo = x1_ref[rows, :half] + w1 * lo1 + w2 * lo2
        hi = x1_ref[rows, half:] + w1 * hi1 + w2 * hi2
        if fn_ref is not None:
            ms = (jnp.sum(lo * lo, axis=-1, keepdims=True) + jnp.sum(hi * hi, axis=-1, keepdims=True)) / (2 * half)
            inv = lax.rsqrt(ms + EPS)
            lo = lo * inv * fn_ref[:, :half]
            hi = hi * inv * fn_ref[:, half:]
        out_ref[rows, :half] = lo
        out_ref[rows, half:] = hi

    @pl.when(i == 0)
    def _():
        gather(s0_ref, ya, 0)

    gather(s1_ref, yb, 1)
    finish(ya, 0, slice(0, tmc))

    @pl.when(i + 1 < n)
    def _():
        gather(s2_ref, ya, 0)

    finish(yb, 1, slice(tmc, 2 * tmc))


def _combine_plain_kernel(s0_ref, s1_ref, s2_ref, x1_ref, info_ref, ys_hbm, out_ref, ya, yb, sem, *, tmc):
    _combine_kernel(s0_ref, s1_ref, s2_ref, x1_ref, info_ref, ys_hbm, None, out_ref, ya, yb, sem, tmc=tmc)


def _combine(x1, info, ys, slots, fn, *, tmc, final_norm):
    t, d = x1.shape
    n_tiles = t // tmc
    n = n_tiles // 2
    smem = lambda f: pl.BlockSpec((1, 1, 2 * tmc), f, memory_space=pltpu.SMEM)
    in_specs = [smem(lambda i: (2 * i, 0, 0)), smem(lambda i: (2 * i + 1, 0, 0)),
                smem(lambda i: (jnp.minimum(2 * i + 2, n_tiles - 1), 0, 0)),
                pl.BlockSpec((2 * tmc, d), lambda i: (i, 0)),
                pl.BlockSpec((2 * tmc, LANES), lambda i: (i, 0)),
                pl.BlockSpec(memory_space=pl.ANY)]
    args = [slots, slots, slots, x1, info, ys]
    if final_norm:
        in_specs.append(pl.BlockSpec((1, d), lambda i: (0, 0)))
        args.append(fn)
    return pl.pallas_call(
        functools.partial(_combine_kernel if final_norm else _combine_plain_kernel, tmc=tmc),
        grid=(n,),
        in_specs=in_specs,
        out_specs=pl.BlockSpec((2 * tmc, d), lambda i: (i, 0)),
        out_shape=jax.ShapeDtypeStruct((t, d), F32),
        scratch_shapes=[pltpu.VMEM((2 * tmc * ROW_CHUNKS, LANES), jnp.uint32),
                        pltpu.VMEM((2 * tmc * ROW_CHUNKS, LANES), jnp.uint32), pltpu.SemaphoreType.DMA((2,))],
        compiler_params=pltpu.CompilerParams(dimension_semantics=("arbitrary",), vmem_limit_bytes=VMEM_LIMIT),
        name="combine",
    )(*args)


def _moe(x1, h2p, info, meta, cnt, lw, fn, xs_init, *, tme, tmc, final_norm):
    t = x1.shape[0]
    n_tiles = (2 * t) // tme + N_EXPERTS
    counts = cnt[:, 0].astype(jnp.int32)
    padded = ((counts + tme - 1) // tme) * tme
    ends = jnp.cumsum(padded)
    starts = ends - padded
    mi = meta.astype(jnp.int32)
    expert_ids = jnp.arange(N_EXPERTS, dtype=jnp.int32)

    def slot_of(e, rank):
        start = jnp.sum(jnp.where(e[..., None] == expert_ids, starts, 0), axis=-1)
        return ((start + rank) * ROW_CHUNKS).reshape(t // tmc, 1, tmc)

    slot1 = slot_of(mi[:, _M_E1], mi[:, _M_R1])
    slot2 = slot_of(mi[:, _M_E2], mi[:, _M_R2])
    slots = jnp.concatenate([slot1, slot2], axis=2)
    tile_start = jnp.arange(n_tiles, dtype=jnp.int32) * tme
    tile_expert = jnp.minimum(jnp.sum((ends[None, :] <= tile_start[:, None]).astype(jnp.int32), axis=1),
                              N_EXPERTS - 1)
    n_used = ends[-1:] // tme
    xs = _dispatch(h2p, slots, xs_init, td=tmc)
    ys = _experts(xs, tile_expert, n_used, lw, tme=tme)
    return _combine(x1, info, ys, slots, fn, tmc=tmc, final_norm=final_norm), xs


def _pack_layer(l, w_in, norm_mix, q_norm, w_q_up, kv_norm, w_kv_up, sg_norm, w_s, b_s, conv_w,
                w_a_out, w_b_out, w_c_out, w_o, norm_ffn, w_group_router, b_group_router,
                w_expert_router, b_expert_router, w_gate_up, w_down):
    d = w_in.shape[1]
    wi = w_in[l]
    o_kr = Q_RANK + KV_RANK
    kr_slot = jnp.zeros((d, HEAD_PAD), F32).at[:, QK_NOPE:QK_HEAD].set(wi[:, o_kr:o_kr + QK_ROPE])
    w_in_p = jnp.concatenate([wi[:, :o_kr], kr_slot, wi[:, o_kr + QK_ROPE:]], axis=1).astype(BF16)
    wq = w_q_up[l].reshape(Q_RANK, MLA_HEADS, QK_HEAD)
    wq = jnp.pad(wq, ((0, 0), (0, 0), (0, HEAD_PAD - QK_HEAD))).reshape(Q_RANK, MLA_HEADS * HEAD_PAD)
    wkv = w_kv_up[l].reshape(KV_RANK, MLA_HEADS, QK_NOPE + V_HEAD)
    wk = jnp.pad(wkv[:, :, :QK_NOPE], ((0, 0), (0, 0), (0, HEAD_PAD - QK_NOPE)))
    wk = wk.reshape(KV_RANK, MLA_HEADS * HEAD_PAD)
    wv = wkv[:, :, QK_NOPE:].reshape(KV_RANK, MLA_HEADS * V_HEAD)
    w_r = jnp.zeros((LANES, d), F32)
    w_r = w_r.at[:N_EXPERTS].set(w_expert_router[l].T)
    w_r = w_r.at[GROUP_ROW0:GROUP_ROW0 + N_GROUPS].set(w_group_router[l].T)
    b_r = jnp.zeros((LANES, 1), F32)
    b_r = b_r.at[:N_EXPERTS, 0].set(b_expert_router[l])
    b_r = b_r.at[GROUP_ROW0:GROUP_ROW0 + N_GROUPS, 0].set(b_group_router[l])
    return {
        "norm_mix": norm_mix[l][None, :], "w_in": w_in_p,
        "q_norm": q_norm[l][None, :], "w_q": wq.astype(BF16),
        "kv_norm": kv_norm[l][None, :], "w_k": wk.astype(BF16), "w_v": wv.astype(BF16),
        "sg_norm": sg_norm[l][None, :], "w_s": w_s[l],
        "b_s": jnp.broadcast_to(b_s[l][:, :, None], (SG_GROUPS, CHUNK, SG_GROUP_DIM)),
        "conv_w": conv_w[l],
        "w_a_out": w_a_out[l].astype(BF16), "w_b_out": w_b_out[l].astype(BF16),
        "w_c_out": w_c_out[l].astype(BF16), "w_o": w_o[l].astype(BF16),
        "norm_ffn": norm_ffn[l][None, :], "w_router_t": w_r.astype(BF16), "b_router_t": b_r,
        "w_gate_up": w_gate_up[l].astype(BF16), "w_down": w_down[l].astype(BF16),
    }


def _rope_tables(positions):
    inv = ROPE_BASE ** (-jnp.arange(0, QK_ROPE, 2, dtype=F32) / QK_ROPE)
    ang = inv[:, None] * positions.astype(F32).reshape(-1)[None, :]
    return jnp.cos(ang), jnp.sin(ang)


def kernel(x, positions, norm_mix, w_in, q_norm, w_q_up, kv_norm, w_kv_up, sg_norm, w_s, b_s, conv_w,
           w_a_out, w_b_out, w_c_out, w_o, norm_ffn, w_group_router, b_group_router, w_expert_router,
           b_expert_router, w_gate_up, w_down, final_norm):
    batch, seq, d = x.shape
    depth = w_in.shape[0]
    tm = min(512, seq)
    tq = tm
    tme = 512
    tmc = min(256, seq)
    cos_t, sin_t = _rope_tables(positions)
    x2 = x.reshape(batch * seq, d)
    fn = final_norm[None, :]
    xs = jnp.zeros(((2 * batch * seq + N_EXPERTS * tme) * ROW_CHUNKS, LANES), jnp.uint32)
    for l in range(depth):
        lw = _pack_layer(l, w_in, norm_mix, q_norm, w_q_up, kv_norm, w_kv_up, sg_norm, w_s, b_s, conv_w,
                         w_a_out, w_b_out, w_c_out, w_o, norm_ffn, w_group_router, b_group_router,
                         w_expert_router, b_expert_router, w_gate_up, w_down)
        q, k, v, part, ga = _mixer_pre(x2, lw, cos_t, sin_t, seq=seq, tm=tm)
        o = _attention(q, k, v, batch=batch, seq=seq, tq=tq)
        x1, h2p, info, meta, cnt = _mixer_post(o, part, ga, x2, lw, tm=tm)
        x2, xs = _moe(x1, h2p, info, meta, cnt, lw, fn, xs, tme=tme, tmc=tmc, final_norm=(l == depth - 1))
    return x2.reshape(batch, seq, d)
```

```python
import functools

import jax
import jax.numpy as jnp
from jax import lax
from jax.experimental import pallas as pl
from jax.experimental.pallas import tpu as pltpu

F32 = jnp.float32
BF16 = jnp.bfloat16

EPS = 1e-6
MLA_HEADS = 8
QK_NOPE = 64
QK_ROPE = 32
QK_HEAD = QK_NOPE + QK_ROPE
V_HEAD = 64
Q_RANK = 384
KV_RANK = 256
ROPE_BASE = 10000.0
SG_GROUPS = 4
SG_GROUP_DIM = 128
SG_WIDTH = SG_GROUPS * SG_GROUP_DIM
CHUNK = 128
CONV_WIDTH = 512
CONV_K = 3
N_GROUPS = 4
EXPERTS_PER_GROUP = 8
N_EXPERTS = N_GROUPS * EXPERTS_PER_GROUP
D_EXPERT = 256

LANES = 128
HEAD_PAD = 128
MASK_VALUE = -1e30
LOG2_E = 1.4426950408889634
ROW_CHUNKS = 4
VMEM_LIMIT = 56 * 1024 * 1024

_P1 = Q_RANK + KV_RANK + HEAD_PAD
_P2 = _P1 + 2 * SG_WIDTH
_P3 = _P2 + 3 * CONV_WIDTH
GROUP_ROW0 = N_EXPERTS


def _dot(a, b):
    return jnp.dot(a, b, preferred_element_type=F32)


def _rms(xf, g):
    return xf * lax.rsqrt(jnp.mean(xf * xf, axis=-1, keepdims=True) + EPS) * g


def _gelu(x):
    return 0.5 * x * (1.0 + jnp.tanh(0.7978845608028654 * (x + 0.044715 * (x * x * x))))


def _rope_t(tt, cos, sin):
    lo, mid = QK_NOPE, QK_NOPE + QK_ROPE // 2
    x1, x2 = tt[lo:mid], tt[mid:QK_HEAD]
    return jnp.concatenate([tt[:lo], x1 * cos - x2 * sin, x2 * cos + x1 * sin, tt[QK_HEAD:]], axis=0)


def _const_spec(shape):
    nd = len(shape)
    return pl.BlockSpec(shape, lambda *_: (0,) * nd, pipeline_mode=pl.Buffered(1))


def _mixer_pre_kernel(x_ref, nm_ref, win_ref, qn_ref, wq_ref, kvn_ref, wk_ref, wv_ref, sgn_ref,
                      ws_ref, bs_ref, cw_ref, wb_ref, wc_ref, cos_ref, sin_ref,
                      qt_ref, k_ref, vt_ref, part_ref, ga_ref,
                      halo_ref, ybin_ref, *, tiles_per_seq, tm):
    i = pl.program_id(0)
    d_model = x_ref.shape[1]
    h = _rms(x_ref[...], nm_ref[...]).astype(BF16)

    p1 = _dot(h, win_ref[:, 0:_P1])
    cos, sin = cos_ref[...], sin_ref[...]
    qn = _rms(p1[:, :Q_RANK], qn_ref[...]).astype(BF16)
    qf = _dot(qn, wq_ref[...])
    scale = QK_HEAD ** -0.5 * LOG2_E
    for hd in range(MLA_HEADS):
        sl = slice(hd * HEAD_PAD, (hd + 1) * HEAD_PAD)
        qt_ref[0, 0, sl, :] = (_rope_t(qf[:, sl].T, cos, sin) * scale).astype(BF16)
    kvn = _rms(p1[:, Q_RANK:Q_RANK + KV_RANK], kvn_ref[...]).astype(BF16)
    kr = _rope_t(p1[:, Q_RANK + KV_RANK:_P1].T, cos, sin).T
    kf = _dot(kvn, wk_ref[...])
    for hd in range(MLA_HEADS):
        sl = slice(hd * HEAD_PAD, (hd + 1) * HEAD_PAD)
        k_ref[:, sl] = (kf[:, sl] + kr).astype(BF16)
    vt_ref[0, 0] = _dot(kvn, wv_ref[...]).T.astype(BF16)

    p2 = _dot(h, win_ref[:, _P1:_P2])
    u = _gelu(p2[:, :SG_WIDTH])
    vb = _rms(_gelu(p2[:, SG_WIDTH:]), sgn_ref[...]).astype(BF16)
    r_i = lax.broadcasted_iota(jnp.int32, (CHUNK, CHUNK), 0)
    c_i = lax.broadcasted_iota(jnp.int32, (CHUNK, CHUNK), 1)
    for g in range(SG_GROUPS):
        wg = jnp.where(c_i <= r_i, ws_ref[g], 0.0).astype(BF16)
        gs = slice(g * SG_GROUP_DIM, (g + 1) * SG_GROUP_DIM)
        n_chunks = tm // CHUNK
        v_side = jnp.concatenate([vb[c * CHUNK:(c + 1) * CHUNK, gs] for c in range(n_chunks)], axis=1)
        mixed_side = _dot(wg, v_side)
        for c in range(n_chunks):
            cs = slice(c * CHUNK, (c + 1) * CHUNK)
            mixed = mixed_side[:, c * SG_GROUP_DIM:(c + 1) * SG_GROUP_DIM] + bs_ref[g]
            ybin_ref[cs, gs] = (u[cs, gs] * mixed).astype(BF16)

    @pl.when(i % tiles_per_seq == 0)
    def _():
        halo_ref[...] = jnp.zeros_like(halo_ref)

    p3 = _dot(h, win_ref[:, _P2:_P3])
    z = p3[:, CONV_WIDTH:2 * CONV_WIDTH] * p3[:, 2 * CONV_WIDTH:]
    row = lax.broadcasted_iota(jnp.int32, z.shape, 0)
    halo = halo_ref[...]
    last1 = halo[7:8, :]
    last2 = halo[6:7, :]
    z1 = jnp.where(row == 0, last1, pltpu.roll(z, 1, 0))
    z2 = jnp.where(row == 0, last2, jnp.where(row == 1, last1, pltpu.roll(z, 2, 0)))
    y = cw_ref[0:1, :] * z2 + cw_ref[1:2, :] * z1 + cw_ref[2:3, :] * z
    yc_in = (p3[:, :CONV_WIDTH] * y).astype(BF16)
    halo_ref[...] = z[tm - 8:tm, :]

    p4 = _dot(h, win_ref[:, _P3:_P3 + 3 * d_model])
    yb = _dot(ybin_ref[...], wb_ref[...])
    yc = _dot(yc_in, wc_ref[...])
    part = (jax.nn.sigmoid(p4[:, d_model:2 * d_model]) * yb
            + jax.nn.sigmoid(p4[:, 2 * d_model:]) * yc)
    part_ref[...] = part.astype(BF16)
    ga_ref[...] = jax.nn.sigmoid(p4[:, :d_model]).astype(BF16)


def _mixer_pre(x2, lw, cos_t, sin_t, *, seq, tm):
    t, d = x2.shape
    row = lambda w: pl.BlockSpec((tm, w), lambda i: (i, 0))
    consts = [lw["norm_mix"], lw["w_in"], lw["q_norm"], lw["w_q"], lw["kv_norm"], lw["w_k"], lw["w_v"],
              lw["sg_norm"], lw["w_s"], lw["b_s"], lw["conv_w"], lw["w_b_out"], lw["w_c_out"]]
    tps = seq // tm
    kern = functools.partial(_mixer_pre_kernel, tiles_per_seq=tps, tm=tm)
    slab = lambda w: pl.BlockSpec((1, 1, w, tm), lambda i: (i // tps, i % tps, 0, 0))
    return pl.pallas_call(
        kern,
        grid=(t // tm,),
        in_specs=[row(d)] + [_const_spec(c.shape) for c in consts]
        + [pl.BlockSpec((QK_ROPE // 2, tm), lambda i: (0, i))] * 2,
        out_specs=[slab(MLA_HEADS * HEAD_PAD), row(MLA_HEADS * HEAD_PAD), slab(MLA_HEADS * V_HEAD), row(d), row(d)],
        out_shape=[jax.ShapeDtypeStruct((t // seq, tps, MLA_HEADS * HEAD_PAD, tm), BF16),
                   jax.ShapeDtypeStruct((t, MLA_HEADS * HEAD_PAD), BF16),
                   jax.ShapeDtypeStruct((t // seq, tps, MLA_HEADS * V_HEAD, tm), BF16),
                   jax.ShapeDtypeStruct((t, d), BF16),
                   jax.ShapeDtypeStruct((t, d), BF16)],
        scratch_shapes=[pltpu.VMEM((8, CONV_WIDTH), F32), pltpu.VMEM((tm, SG_WIDTH), BF16)],
        compiler_params=pltpu.CompilerParams(dimension_semantics=("arbitrary",), vmem_limit_bytes=VMEM_LIMIT),
        name="mixer_pre",
    )(x2, *consts, cos_t, sin_t)


ACC_ROWS = V_HEAD + 16


def _attn_kernel(qt_ref, k_ref, vt_ref, o_ref, m_ref, acc_ref, *bufs, tq, nq):
    (sa, mca), (sb, mcb), (sf0, mcf0), (sf1, mcf1) = [bufs[2 * n:2 * n + 2] for n in range(4)]
    buf_a, buf_b, first = (sa, mca), (sb, mcb), ((sf0, mcf0), (sf1, mcf1))

    def score(qi, j, buf, masked):
        s_ref, mc_ref = buf
        start = pl.multiple_of(j * tq, tq)
        for sub in range(2):
            k_j = k_ref[pl.ds(start, tq), sub * HEAD_PAD:(sub + 1) * HEAD_PAD]
            st = _dot(k_j, qt_ref[0, qi, sub * HEAD_PAD:(sub + 1) * HEAD_PAD, :])
            if masked:
                kv_i = lax.broadcasted_iota(jnp.int32, st.shape, 0)
                q_i = lax.broadcasted_iota(jnp.int32, st.shape, 1)
                st = jnp.where(kv_i <= q_i, st, MASK_VALUE)
            s_ref[sub] = st
            mc_ref[sub] = jnp.broadcast_to(jnp.max(st, axis=0, keepdims=True), (8, tq))

    def value(j, buf):
        s_ref, mc_ref = buf
        for sub in range(2):
            m_prev = m_ref[sub]
            m_next = jnp.maximum(m_prev, mc_ref[sub])
            alpha = jnp.exp2(m_prev - m_next)
            pt = jnp.exp2((s_ref[sub] - m_next[0:1, :]).astype(BF16))
            m_ref[sub] = m_next
            vt_j = jnp.concatenate([vt_ref[0, j, sub * V_HEAD:(sub + 1) * V_HEAD, :],
                                    jnp.ones((ACC_ROWS - V_HEAD, tq), BF16)], axis=0)
            acc_ref[sub] = alpha[0:1, :] * acc_ref[sub] + _dot(vt_j, pt)

    def q_tile(qi, par):
        cur, nxt = first[par], first[1 - par]
        m_ref[...] = jnp.full_like(m_ref, MASK_VALUE)
        acc_ref[...] = jnp.zeros_like(acc_ref)

        def prefetch():
            score(jnp.minimum(qi + 1, nq - 1), 0, nxt, False)

        def ramp(first_tail_tile):
            score(qi, 1, buf_a, False)
            value(0, cur)

            def body(p, c):
                score(qi, 2 * p + 2, buf_b, False)
                value(2 * p + 1, buf_a)
                score(qi, 2 * p + 3, buf_a, False)
                value(2 * p + 2, buf_b)
                return c

            lax.fori_loop(0, (first_tail_tile - 1) // 2, body, 0)

        if par == 0:
            @pl.when(qi == 0)
            def _():
                prefetch()
                value(0, cur)

            @pl.when(qi >= 2)
            def _():
                ramp(qi - 1)
                score(qi, qi, buf_b, True)
                value(qi - 1, buf_a)
                prefetch()
                value(qi, buf_b)
        else:
            @pl.when(qi == 1)
            def _():
                score(qi, 1, buf_a, True)
                value(0, cur)
                prefetch()
                value(1, buf_a)

            @pl.when(qi >= 3)
            def _():
                ramp(qi - 2)
                score(qi, qi - 1, buf_b, False)
                value(qi - 2, buf_a)
                score(qi, qi, buf_a, True)
                value(qi - 1, buf_b)
                prefetch()
                value(qi, buf_a)

        ot = jnp.concatenate([acc_ref[sub, 0:V_HEAD, :] / acc_ref[sub, V_HEAD:V_HEAD + 1, :]
                              for sub in range(2)], axis=0)
        o_ref[pl.ds(pl.multiple_of(qi * tq, tq), tq), :] = ot.T.astype(BF16)

    score(0, 0, first[0], True)

    def q_pair(p, carry):
        q_tile(2 * p, 0)
        q_tile(2 * p + 1, 1)
        return carry

    lax.fori_loop(0, nq // 2, q_pair, 0)

def _attention(qt, k, vt, *, batch, seq, tq):
    t = k.shape[0]
    nq = seq // tq
    assert nq % 2 == 0
    pairs = MLA_HEADS // 2
    score_buf = [pltpu.VMEM((2, tq, tq), F32), pltpu.VMEM((2, 8, tq), F32)]
    return pl.pallas_call(
        functools.partial(_attn_kernel, tq=tq, nq=nq),
        grid=(batch, pairs),
        in_specs=[pl.BlockSpec((1, nq, 2 * HEAD_PAD, tq), lambda b, hp: (b, 0, hp, 0)),
                  pl.BlockSpec((seq, 2 * HEAD_PAD), lambda b, hp: (b, hp)),
                  pl.BlockSpec((1, nq, 2 * V_HEAD, tq), lambda b, hp: (b, 0, hp, 0))],
        out_specs=pl.BlockSpec((seq, 2 * V_HEAD), lambda b, hp: (b, hp)),
        out_shape=jax.ShapeDtypeStruct((t, MLA_HEADS * V_HEAD), BF16),
        scratch_shapes=[pltpu.VMEM((2, 8, tq), F32), pltpu.VMEM((2, ACC_ROWS, tq), F32)] + score_buf * 4,
        compiler_params=pltpu.CompilerParams(dimension_semantics=("arbitrary",) * 2, vmem_limit_bytes=VMEM_LIMIT),
        name="attention",
    )(qt, k, vt)


def _pack_pair(lo, hi):
    lo_b = lax.bitcast_convert_type(lo, jnp.uint32)
    hi_b = lax.bitcast_convert_type(hi, jnp.uint32)
    return (hi_b & jnp.uint32(0xFFFF0000)) | (lo_b >> 16)


def _unpack_pair(p):
    lo = lax.bitcast_convert_type(p << 16, F32)
    hi = lax.bitcast_convert_type(p & jnp.uint32(0xFFFF0000), F32)
    return lo, hi


def _store_rows(ref, first, n_rows, packed):
    for c in range(ROW_CHUNKS):
        ref[pl.ds(first * ROW_CHUNKS + c, n_rows, stride=ROW_CHUNKS), :] = packed[:, c * LANES:(c + 1) * LANES]


def _load_rows(ref, first, n_rows):
    return jnp.concatenate([ref[pl.ds(first * ROW_CHUNKS + c, n_rows, stride=ROW_CHUNKS), :]
                            for c in range(ROW_CHUNKS)], axis=1)


def _route_t(lt):
    big = float(LANES)
    g = lt[GROUP_ROW0:GROUP_ROW0 + 8]
    g_row = lax.broadcasted_iota(jnp.int32, g.shape, 0).astype(F32)
    g_log = jnp.where(g_row < float(N_GROUPS), g, MASK_VALUE)
    g_max = jnp.max(g_log, axis=0, keepdims=True)
    g_p = 1.0 / jnp.sum(jnp.exp(g_log - g_max), axis=0, keepdims=True)
    g_idx = jnp.min(jnp.where(g_log == g_max, g_row, big), axis=0, keepdims=True)
    e = lt[0:N_EXPERTS]
    e_row = lax.broadcasted_iota(jnp.int32, e.shape, 0).astype(F32)
    lo = g_idx * float(EXPERTS_PER_GROUP)
    in_grp = (e_row >= lo) & (e_row < lo + float(EXPERTS_PER_GROUP))
    e_log = jnp.where(in_grp, e, MASK_VALUE)
    e_max = jnp.max(e_log, axis=0, keepdims=True)
    e_exp = jnp.exp(e_log - e_max)
    e_prob = e_exp / jnp.sum(e_exp, axis=0, keepdims=True)
    cand = jnp.where(in_grp, e_prob, -1.0)
    p1 = jnp.max(cand, axis=0, keepdims=True)
    i1 = jnp.min(jnp.where(cand == p1, e_row, big), axis=0, keepdims=True)
    cand2 = jnp.where(e_row == i1, -1.0, cand)
    p2 = jnp.max(cand2, axis=0, keepdims=True)
    i2 = jnp.min(jnp.where(cand2 == p2, e_row, big), axis=0, keepdims=True)
    denom = p1 + p2
    return g_p * (p1 / denom), g_p * (p2 / denom), i1, i2, e_row


_L_W1, _L_W2 = 0, 1
_M_E1, _M_E2, _M_R1, _M_R2 = range(4)


def _mixer_post_kernel(o_ref, part_ref, ga_ref, x_ref, wa_ref, wo_ref, nf_ref, wrt_ref, brt_ref,
                       x1_ref, h2p_ref, info_ref, meta_ref, cnt_ref, run_ref, *, tm):
    i = pl.program_id(0)
    half = x_ref.shape[1] // 2

    @pl.when(i == 0)
    def _():
        run_ref[...] = jnp.zeros_like(run_ref)

    ya = _dot(o_ref[...], wa_ref[...])
    merged = ga_ref[...].astype(F32) * ya + part_ref[...].astype(F32)
    x1 = x_ref[...] + _dot(merged.astype(BF16), wo_ref[...])
    x1_ref[...] = x1
    h2 = _rms(x1, nf_ref[...]).astype(BF16)
    h2f = h2.astype(F32)
    _store_rows(h2p_ref, 0, tm, _pack_pair(h2f[:, :half], h2f[:, half:]))
    lt = lax.dot_general(wrt_ref[...], h2, (((1,), (1,)), ((), ())), preferred_element_type=F32) + brt_ref[...]
    w1, w2, i1, i2, e_row = _route_t(lt)

    oh1 = e_row == i1
    oh2 = e_row == i2
    both = jnp.where(oh1, 1.0, 0.0) + jnp.where(oh2, 1.0, 0.0)
    r_i = lax.broadcasted_iota(jnp.int32, (tm, tm), 0)
    c_i = lax.broadcasted_iota(jnp.int32, (tm, tm), 1)
    earlier = jnp.where(r_i < c_i, 1.0, 0.0).astype(BF16)
    run = run_ref[...]
    before = _dot(both.astype(BF16), earlier) + run[:, 0:1]
    rank1 = jnp.sum(jnp.where(oh1, before, 0.0), axis=0, keepdims=True)
    rank2 = jnp.sum(jnp.where(oh2, before, 0.0), axis=0, keepdims=True)
    run = run + jnp.sum(both, axis=1, keepdims=True)
    run_ref[...] = run
    cnt_ref[...] = run

    row8 = lax.broadcasted_iota(jnp.int32, (8, tm), 0)
    meta_ref[0] = jnp.where(row8 == _M_E1, i1, jnp.where(row8 == _M_E2, i2,
                            jnp.where(row8 == _M_R1, rank1, jnp.where(row8 == _M_R2, rank2, 0.0))))
    row128 = lax.broadcasted_iota(jnp.int32, (LANES, tm), 0)
    info_ref[...] = jnp.where(row128 == _L_W1, w1, jnp.where(row128 == _L_W2, w2, 0.0)).T


def _mixer_post(o, part, ga, x2, lw, *, tm):
    t, d = x2.shape
    row = lambda w: pl.BlockSpec((tm, w), lambda i: (i, 0))
    consts = [lw["w_a_out"], lw["w_o"], lw["norm_ffn"], lw["w_router_t"], lw["b_router_t"]]
    return pl.pallas_call(
        functools.partial(_mixer_post_kernel, tm=tm),
        grid=(t // tm,),
        in_specs=[row(o.shape[1]), row(d), row(d), row(d)] + [_const_spec(c.shape) for c in consts],
        out_specs=[row(d), pl.BlockSpec((tm * ROW_CHUNKS, LANES), lambda i: (i, 0)), row(LANES),
                   pl.BlockSpec((1, 8, tm), lambda i: (i, 0, 0)),
                   pl.BlockSpec((N_EXPERTS, LANES), lambda i: (0, 0))],
        out_shape=[jax.ShapeDtypeStruct((t, d), F32), jax.ShapeDtypeStruct((t * ROW_CHUNKS, LANES), jnp.uint32),
                   jax.ShapeDtypeStruct((t, LANES), F32), jax.ShapeDtypeStruct((t // tm, 8, tm), F32),
                   jax.ShapeDtypeStruct((N_EXPERTS, LANES), F32)],
        scratch_shapes=[pltpu.VMEM((N_EXPERTS, LANES), F32)],
        compiler_params=pltpu.CompilerParams(dimension_semantics=("arbitrary",), vmem_limit_bytes=VMEM_LIMIT),
        name="mixer_post",
    )(o, part, ga, x2, *consts)


DISPATCH_RING = 4


def _dispatch_kernel(slots_ref, h2p_hbm, xs_in_hbm, xs_hbm, buf, lsem, rsem, *, td):
    del xs_in_hbm
    i = pl.program_id(0)
    n = pl.num_programs(0)

    def load(tile, b):
        return pltpu.make_async_copy(h2p_hbm.at[pl.ds(tile * (td * ROW_CHUNKS), td * ROW_CHUNKS)], buf.at[b],
                                     lsem.at[b])

    def wait_rows(b):
        for _ in range(2):
            pltpu.make_async_copy(buf.at[b], xs_hbm.at[pl.ds(0, td * ROW_CHUNKS)], rsem.at[b]).wait()

    @pl.when(i == 0)
    def _():
        load(0, 0).start()
        load(1, 1).start()

    for k in range(DISPATCH_RING):
        tile = DISPATCH_RING * i + k
        load(tile, k).wait()
        for r in range(td):
            for a in range(2):
                dst = pl.multiple_of(slots_ref[0, 0, (2 * k + a) * td + r], ROW_CHUNKS)
                pltpu.make_async_copy(buf.at[k, pl.ds(r * ROW_CHUNKS, ROW_CHUNKS)],
                                      xs_hbm.at[pl.ds(dst, ROW_CHUNKS)], rsem.at[k]).start(priority=a)
        if k > 0:
            wait_rows(k - 1)
        else:
            @pl.when(i > 0)
            def _():
                wait_rows(DISPATCH_RING - 1)
        if k + 2 < DISPATCH_RING:
            load(tile + 2, k + 2).start()
        else:
            @pl.when(i + 1 < n)
            def _():
                load(tile + 2, (k + 2) % DISPATCH_RING).start()

    @pl.when(i == n - 1)
    def _():
        wait_rows(DISPATCH_RING - 1)


def _dispatch(h2p, slots, xs_init, *, td):
    t = h2p.shape[0] // ROW_CHUNKS
    steps = t // (DISPATCH_RING * td)
    return pl.pallas_call(
        functools.partial(_dispatch_kernel, td=td),
        grid=(steps,),
        in_specs=[pl.BlockSpec((1, 1, DISPATCH_RING * 2 * td), lambda i: (i, 0, 0), memory_space=pltpu.SMEM),
                  pl.BlockSpec(memory_space=pl.ANY),
                  pl.BlockSpec(memory_space=pl.ANY)],
        out_specs=pl.BlockSpec(memory_space=pl.ANY),
        out_shape=jax.ShapeDtypeStruct(xs_init.shape, jnp.uint32),
        scratch_shapes=[pltpu.VMEM((DISPATCH_RING, td * ROW_CHUNKS, LANES), jnp.uint32),
                        pltpu.SemaphoreType.DMA((DISPATCH_RING,)), pltpu.SemaphoreType.DMA((DISPATCH_RING,))],
        input_output_aliases={2: 0},
        compiler_params=pltpu.CompilerParams(dimension_semantics=("arbitrary",), vmem_limit_bytes=VMEM_LIMIT,
                                             has_side_effects=True),
        name="dispatch",
    )(slots.reshape(steps, 1, DISPATCH_RING * 2 * td), h2p, xs_init)


def _expert_kernel(te_ref, nu_ref, xs_ref, wgu_ref, wdn_ref, ys_ref, wgu_bf, wdn_bf, *, tme):
    i = pl.program_id(0)
    half = wgu_bf.shape[0] // 2

    @pl.when((i == 0) | (te_ref[i] != te_ref[jnp.maximum(i - 1, 0)]))
    def _():
        wgu_bf[...] = wgu_ref[0, 0].astype(BF16)
        wdn_bf[...] = wdn_ref[0, 0].astype(BF16)

    @pl.when(i < nu_ref[0])
    def _():
        lo, hi = _unpack_pair(_load_rows(xs_ref, 0, tme))
        gu = _dot(lo.astype(BF16), wgu_bf[:half, :]) + _dot(hi.astype(BF16), wgu_bf[half:, :])
        act = (jax.nn.silu(gu[:, :D_EXPERT]) * gu[:, D_EXPERT:]).astype(BF16)
        y = _dot(act, wdn_bf[...]).astype(BF16).astype(F32)
        _store_rows(ys_ref, 0, tme, _pack_pair(y[:, :half], y[:, half:]))

    @pl.when(i >= nu_ref[0])
    def _():
        ys_ref[...] = jnp.zeros_like(ys_ref)


def _experts(xs, tile_expert, n_used, lw, *, tme):
    n_slots = xs.shape[0] // ROW_CHUNKS
    layer, w_gate_up, w_down = lw["layer"], lw["w_gate_up"], lw["w_down"]
    d = w_gate_up.shape[2]
    grid_spec = pltpu.PrefetchScalarGridSpec(
        num_scalar_prefetch=2,
        grid=(n_slots // tme,),
        in_specs=[pl.BlockSpec((tme * ROW_CHUNKS, LANES), lambda i, te, nu: (i, 0)),
                  pl.BlockSpec((1, 1, d, 2 * D_EXPERT), lambda i, te, nu: (layer, te[i], 0, 0)),
                  pl.BlockSpec((1, 1, D_EXPERT, d), lambda i, te, nu: (layer, te[i], 0, 0))],
        out_specs=pl.BlockSpec((tme * ROW_CHUNKS, LANES), lambda i, te, nu: (i, 0)),
        scratch_shapes=[pltpu.VMEM((d, 2 * D_EXPERT), BF16), pltpu.VMEM((D_EXPERT, d), BF16)],
    )
    return pl.pallas_call(
        functools.partial(_expert_kernel, tme=tme),
        grid_spec=grid_spec,
        out_shape=jax.ShapeDtypeStruct(xs.shape, jnp.uint32),
        compiler_params=pltpu.CompilerParams(dimension_semantics=("arbitrary",), vmem_limit_bytes=VMEM_LIMIT),
        name="experts",
    )(tile_expert, n_used, xs, w_gate_up, w_down)


def _combine_kernel(s0_ref, s1_ref, s2_ref, x1_ref, info_ref, ys_hbm, fn_ref, out_ref, ya, yb, sem, *, tmc):
    i = pl.program_id(0)
    n = pl.num_programs(0)
    half = x1_ref.shape[1] // 2

    def gather(idx_ref, buf, k):
        for r in range(2 * tmc):
            src = pl.multiple_of(idx_ref[0, 0, r], ROW_CHUNKS)
            pltpu.make_async_copy(ys_hbm.at[pl.ds(src, ROW_CHUNKS)], buf.at[pl.ds(r * ROW_CHUNKS, ROW_CHUNKS)],
                                  sem.at[k]).start(priority=r % 2)

    def finish(buf, k, rows):
        pltpu.make_async_copy(ys_hbm.at[pl.ds(0, 2 * tmc * ROW_CHUNKS)], buf, sem.at[k]).wait()
        lo1, hi1 = _unpack_pair(_load_rows(buf, 0, tmc))
        lo2, hi2 = _unpack_pair(_load_rows(buf, tmc, tmc))
        info = info_ref[rows, :]
        w1 = info[:, _L_W1:_L_W1 + 1]
        w2 = info[:, _L_W2:_L_W2 + 1]
        lo = x1_ref[rows, :half] + w1 * lo1 + w2 * lo2
        hi = x1_ref[rows, half:] + w1 * hi1 + w2 * hi2
        if fn_ref is not None:
            ms = (jnp.sum(lo * lo, axis=-1, keepdims=True) + jnp.sum(hi * hi, axis=-1, keepdims=True)) / (2 * half)
            inv = lax.rsqrt(ms + EPS)
            lo = lo * inv * fn_ref[:, :half]
            hi = hi * inv * fn_ref[:, half:]
        out_ref[rows, :half] = lo
        out_ref[rows, half:] = hi

    @pl.when(i == 0)
    def _():
        gather(s0_ref, ya, 0)

    gather(s1_ref, yb, 1)
    finish(ya, 0, slice(0, tmc))

    @pl.when(i + 1 < n)
    def _():
        gather(s2_ref, ya, 0)

    finish(yb, 1, slice(tmc, 2 * tmc))


def _combine_plain_kernel(s0_ref, s1_ref, s2_ref, x1_ref, info_ref, ys_hbm, out_ref, ya, yb, sem, *, tmc):
    _combine_kernel(s0_ref, s1_ref, s2_ref, x1_ref, info_ref, ys_hbm, None, out_ref, ya, yb, sem, tmc=tmc)


def _combine(x1, info, ys, slots, fn, *, tmc, final_norm):
    t, d = x1.shape
    n_tiles = t // tmc
    n = n_tiles // 2
    smem = lambda f: pl.BlockSpec((1, 1, 2 * tmc), f, memory_space=pltpu.SMEM)
    in_specs = [smem(lambda i: (2 * i, 0, 0)), smem(lambda i: (2 * i + 1, 0, 0)),
                smem(lambda i: (jnp.minimum(2 * i + 2, n_tiles - 1), 0, 0)),
                pl.BlockSpec((2 * tmc, d), lambda i: (i, 0)),
                pl.BlockSpec((2 * tmc, LANES), lambda i: (i, 0)),
                pl.BlockSpec(memory_space=pl.ANY)]
    args = [slots, slots, slots, x1, info, ys]
    if final_norm:
        in_specs.append(pl.BlockSpec((1, d), lambda i: (0, 0)))
        args.append(fn)
    return pl.pallas_call(
        functools.partial(_combine_kernel if final_norm else _combine_plain_kernel, tmc=tmc),
        grid=(n,),
        in_specs=in_specs,
        out_specs=pl.BlockSpec((2 * tmc, d), lambda i: (i, 0)),
        out_shape=jax.ShapeDtypeStruct((t, d), F32),
        scratch_shapes=[pltpu.VMEM((2 * tmc * ROW_CHUNKS, LANES), jnp.uint32),
                        pltpu.VMEM((2 * tmc * ROW_CHUNKS, LANES), jnp.uint32), pltpu.SemaphoreType.DMA((2,))],
        compiler_params=pltpu.CompilerParams(dimension_semantics=("arbitrary",), vmem_limit_bytes=VMEM_LIMIT),
        name="combine",
    )(*args)


def _moe(x1, h2p, info, meta, cnt, lw, fn, xs_init, *, tme, tmc, final_norm):
    t = x1.shape[0]
    n_tiles = (2 * t) // tme + N_EXPERTS
    counts = cnt[:, 0].astype(jnp.int32)
    padded = ((counts + tme - 1) // tme) * tme
    ends = jnp.cumsum(padded)
    starts = ends - padded
    mi = meta.astype(jnp.int32)
    expert_ids = jnp.arange(N_EXPERTS, dtype=jnp.int32)

    def slot_of(e, rank):
        start = jnp.sum(jnp.where(e[..., None] == expert_ids, starts, 0), axis=-1)
        return ((start + rank) * ROW_CHUNKS).reshape(t // tmc, 1, tmc)

    slot1 = slot_of(mi[:, _M_E1], mi[:, _M_R1])
    slot2 = slot_of(mi[:, _M_E2], mi[:, _M_R2])
    slots = jnp.concatenate([slot1, slot2], axis=2)
    tile_start = jnp.arange(n_tiles, dtype=jnp.int32) * tme
    tile_expert = jnp.minimum(jnp.sum((ends[None, :] <= tile_start[:, None]).astype(jnp.int32), axis=1),
                              N_EXPERTS - 1)
    n_used = ends[-1:] // tme
    xs = _dispatch(h2p, slots, xs_init, td=tmc)
    ys = _experts(xs, tile_expert, n_used, lw, tme=tme)
    return _combine(x1, info, ys, slots, fn, tmc=tmc, final_norm=final_norm), xs


def _pack_layer(l, w_in, norm_mix, q_norm, w_q_up, kv_norm, w_kv_up, sg_norm, w_s, b_s, conv_w,
                w_a_out, w_b_out, w_c_out, w_o, norm_ffn, w_group_router, b_group_router,
                w_expert_router, b_expert_router, w_gate_up, w_down):
    d = w_in.shape[1]
    wi = w_in[l]
    o_kr = Q_RANK + KV_RANK
    kr_slot = jnp.zeros((d, HEAD_PAD), F32).at[:, QK_NOPE:QK_HEAD].set(wi[:, o_kr:o_kr + QK_ROPE])
    w_in_p = jnp.concatenate([wi[:, :o_kr], kr_slot, wi[:, o_kr + QK_ROPE:]], axis=1).astype(BF16)
    wq = w_q_up[l].reshape(Q_RANK, MLA_HEADS, QK_HEAD)
    wq = jnp.pad(wq, ((0, 0), (0, 0), (0, HEAD_PAD - QK_HEAD))).reshape(Q_RANK, MLA_HEADS * HEAD_PAD)
    wkv = w_kv_up[l].reshape(KV_RANK, MLA_HEADS, QK_NOPE + V_HEAD)
    wk = jnp.pad(wkv[:, :, :QK_NOPE], ((0, 0), (0, 0), (0, HEAD_PAD - QK_NOPE)))
    wk = wk.reshape(KV_RANK, MLA_HEADS * HEAD_PAD)
    wv = wkv[:, :, QK_NOPE:].reshape(KV_RANK, MLA_HEADS * V_HEAD)
    w_r = jnp.zeros((LANES, d), F32)
    w_r = w_r.at[:N_EXPERTS].set(w_expert_router[l].T)
    w_r = w_r.at[GROUP_ROW0:GROUP_ROW0 + N_GROUPS].set(w_group_router[l].T)
    b_r = jnp.zeros((LANES, 1), F32)
    b_r = b_r.at[:N_EXPERTS, 0].set(b_expert_router[l])
    b_r = b_r.at[GROUP_ROW0:GROUP_ROW0 + N_GROUPS, 0].set(b_group_router[l])
    return {
        "norm_mix": norm_mix[l][None, :], "w_in": w_in_p,
        "q_norm": q_norm[l][None, :], "w_q": wq.astype(BF16),
        "kv_norm": kv_norm[l][None, :], "w_k": wk.astype(BF16), "w_v": wv.astype(BF16),
        "sg_norm": sg_norm[l][None, :], "w_s": w_s[l],
        "b_s": jnp.broadcast_to(b_s[l][:, :, None], (SG_GROUPS, CHUNK, SG_GROUP_DIM)),
        "conv_w": conv_w[l],
        "w_a_out": w_a_out[l].astype(BF16), "w_b_out": w_b_out[l].astype(BF16),
        "w_c_out": w_c_out[l].astype(BF16), "w_o": w_o[l].astype(BF16),
        "norm_ffn": norm_ffn[l][None, :], "w_router_t": w_r.astype(BF16), "b_router_t": b_r,
        "layer": l, "w_gate_up": w_gate_up, "w_down": w_down,
    }


def _rope_tables(positions):
    inv = ROPE_BASE ** (-jnp.arange(0, QK_ROPE, 2, dtype=F32) / QK_ROPE)
    ang = inv[:, None] * positions.astype(F32).reshape(-1)[None, :]
    return jnp.cos(ang), jnp.sin(ang)


def kernel(x, positions, norm_mix, w_in, q_norm, w_q_up, kv_norm, w_kv_up, sg_norm, w_s, b_s, conv_w,
           w_a_out, w_b_out, w_c_out, w_o, norm_ffn, w_group_router, b_group_router, w_expert_router,
           b_expert_router, w_gate_up, w_down, final_norm):
    batch, seq, d = x.shape
    depth = w_in.shape[0]
    tm = min(512, seq)
    tq = tm
    tme = 512
    tmc = min(256, seq)
    cos_t, sin_t = _rope_tables(positions)
    x2 = x.reshape(batch * seq, d)
    fn = final_norm[None, :]
    xs = jnp.zeros(((2 * batch * seq + N_EXPERTS * tme) * ROW_CHUNKS, LANES), jnp.uint32)
    for l in range(depth):
        lw = _pack_layer(l, w_in, norm_mix, q_norm, w_q_up, kv_norm, w_kv_up, sg_norm, w_s, b_s, conv_w,
                         w_a_out, w_b_out, w_c_out, w_o, norm_ffn, w_group_router, b_group_router,
                         w_expert_router, b_expert_router, w_gate_up, w_down)
        q, k, v, part, ga = _mixer_pre(x2, lw, cos_t, sin_t, seq=seq, tm=tm)
        o = _attention(q, k, v, batch=batch, seq=seq, tq=tq)
        x1, h2p, info, meta, cnt = _mixer_post(o, part, ga, x2, lw, tm=tm)
        x2, xs = _moe(x1, h2p, info, meta, cnt, lw, fn, xs, tme=tme, tmc=tmc, final_norm=(l == depth - 1))
    return x2.reshape(batch, seq, d)
```

```python
import functools

import jax
import jax.numpy as jnp
from jax import lax
from jax.experimental import pallas as pl
from jax.experimental.pallas import tpu as pltpu

F32 = jnp.float32
BF16 = jnp.bfloat16

EPS = 1e-6
MLA_HEADS = 8
QK_NOPE = 64
QK_ROPE = 32
QK_HEAD = QK_NOPE + QK_ROPE
V_HEAD = 64
Q_RANK = 384
KV_RANK = 256
ROPE_BASE = 10000.0
SG_GROUPS = 4
SG_GROUP_DIM = 128
SG_WIDTH = SG_GROUPS * SG_GROUP_DIM
CHUNK = 128
CONV_WIDTH = 512
CONV_K = 3
N_GROUPS = 4
EXPERTS_PER_GROUP = 8
N_EXPERTS = N_GROUPS * EXPERTS_PER_GROUP
D_EXPERT = 256

LANES = 128
HEAD_PAD = 128
MASK_VALUE = -1e30
LOG2_E = 1.4426950408889634
ROW_CHUNKS = 4
VMEM_LIMIT = 56 * 1024 * 1024

_P1 = Q_RANK + KV_RANK + HEAD_PAD
_P2 = _P1 + 2 * SG_WIDTH
_P3 = _P2 + 3 * CONV_WIDTH
GROUP_ROW0 = N_EXPERTS


def _dot(a, b):
    return jnp.dot(a, b, preferred_element_type=F32)


def _rms(xf, g):
    return xf * lax.rsqrt(jnp.mean(xf * xf, axis=-1, keepdims=True) + EPS) * g


def _gelu(x):
    return 0.5 * x * (1.0 + jnp.tanh(0.7978845608028654 * (x + 0.044715 * (x * x * x))))


def _rope_t(tt, cos, sin):
    lo, mid = QK_NOPE, QK_NOPE + QK_ROPE // 2
    x1, x2 = tt[lo:mid], tt[mid:QK_HEAD]
    return jnp.concatenate([tt[:lo], x1 * cos - x2 * sin, x2 * cos + x1 * sin, tt[QK_HEAD:]], axis=0)


def _const_spec(shape):
    nd = len(shape)
    return pl.BlockSpec(shape, lambda *_: (0,) * nd, pipeline_mode=pl.Buffered(1))


def _mixer_pre_kernel(x_ref, nm_ref, win_ref, qn_ref, wq_ref, kvn_ref, wk_ref, wv_ref, sgn_ref,
                      ws_ref, bs_ref, cw_ref, wb_ref, wc_ref, cos_ref, sin_ref,
                      qt_ref, k_ref, vt_ref, part_ref, ga_ref,
                      halo_ref, ybin_ref, *, tiles_per_seq, tm):
    i = pl.program_id(0)
    d_model = x_ref.shape[1]
    h = _rms(x_ref[...], nm_ref[...]).astype(BF16)

    p1 = _dot(h, win_ref[:, 0:_P1])
    cos, sin = cos_ref[...], sin_ref[...]
    qn = _rms(p1[:, :Q_RANK], qn_ref[...]).astype(BF16)
    qf = _dot(qn, wq_ref[...])
    scale = QK_HEAD ** -0.5 * LOG2_E
    for hd in range(MLA_HEADS):
        sl = slice(hd * HEAD_PAD, (hd + 1) * HEAD_PAD)
        qt_ref[0, 0, sl, :] = (_rope_t(qf[:, sl].T, cos, sin) * scale).astype(BF16)
    kvn = _rms(p1[:, Q_RANK:Q_RANK + KV_RANK], kvn_ref[...]).astype(BF16)
    kr = _rope_t(p1[:, Q_RANK + KV_RANK:_P1].T, cos, sin).T
    kf = _dot(kvn, wk_ref[...])
    for hd in range(MLA_HEADS):
        sl = slice(hd * HEAD_PAD, (hd + 1) * HEAD_PAD)
        k_ref[:, sl] = (kf[:, sl] + kr).astype(BF16)
    vt_ref[0, 0] = _dot(kvn, wv_ref[...]).T.astype(BF16)

    p2 = _dot(h, win_ref[:, _P1:_P2])
    u = _gelu(p2[:, :SG_WIDTH])
    vb = _rms(_gelu(p2[:, SG_WIDTH:]), sgn_ref[...]).astype(BF16)
    r_i = lax.broadcasted_iota(jnp.int32, (CHUNK, CHUNK), 0)
    c_i = lax.broadcasted_iota(jnp.int32, (CHUNK, CHUNK), 1)
    for g in range(SG_GROUPS):
        wg = jnp.where(c_i <= r_i, ws_ref[g], 0.0).astype(BF16)
        gs = slice(g * SG_GROUP_DIM, (g + 1) * SG_GROUP_DIM)
        n_chunks = tm // CHUNK
        v_side = jnp.concatenate([vb[c * CHUNK:(c + 1) * CHUNK, gs] for c in range(n_chunks)], axis=1)
        mixed_side = _dot(wg, v_side)
        for c in range(n_chunks):
            cs = slice(c * CHUNK, (c + 1) * CHUNK)
            mixed = mixed_side[:, c * SG_GROUP_DIM:(c + 1) * SG_GROUP_DIM] + bs_ref[g]
            ybin_ref[cs, gs] = (u[cs, gs] * mixed).astype(BF16)

    @pl.when(i % tiles_per_seq == 0)
    def _():
        halo_ref[...] = jnp.zeros_like(halo_ref)

    p3 = _dot(h, win_ref[:, _P2:_P3])
    z = p3[:, CONV_WIDTH:2 * CONV_WIDTH] * p3[:, 2 * CONV_WIDTH:]
    row = lax.broadcasted_iota(jnp.int32, z.shape, 0)
    halo = halo_ref[...]
    last1 = halo[7:8, :]
    last2 = halo[6:7, :]
    z1 = jnp.where(row == 0, last1, pltpu.roll(z, 1, 0))
    z2 = jnp.where(row == 0, last2, jnp.where(row == 1, last1, pltpu.roll(z, 2, 0)))
    y = cw_ref[0:1, :] * z2 + cw_ref[1:2, :] * z1 + cw_ref[2:3, :] * z
    yc_in = (p3[:, :CONV_WIDTH] * y).astype(BF16)
    halo_ref[...] = z[tm - 8:tm, :]

    p4 = _dot(h, win_ref[:, _P3:_P3 + 3 * d_model])
    yb = _dot(ybin_ref[...], wb_ref[...])
    yc = _dot(yc_in, wc_ref[...])
    part = (jax.nn.sigmoid(p4[:, d_model:2 * d_model]) * yb
            + jax.nn.sigmoid(p4[:, 2 * d_model:]) * yc)
    part_ref[...] = part.astype(BF16)
    ga_ref[...] = jax.nn.sigmoid(p4[:, :d_model]).astype(BF16)


def _mixer_pre(x2, lw, cos_t, sin_t, *, seq, tm):
    t, d = x2.shape
    row = lambda w: pl.BlockSpec((tm, w), lambda i: (i, 0))
    consts = [lw["norm_mix"], lw["w_in"], lw["q_norm"], lw["w_q"], lw["kv_norm"], lw["w_k"], lw["w_v"],
              lw["sg_norm"], lw["w_s"], lw["b_s"], lw["conv_w"], lw["w_b_out"], lw["w_c_out"]]
    assert lw["conv_w"].shape[0] == CONV_K == 3
    tps = seq // tm
    kern = functools.partial(_mixer_pre_kernel, tiles_per_seq=tps, tm=tm)
    slab = lambda w: pl.BlockSpec((1, 1, w, tm), lambda i: (i // tps, i % tps, 0, 0))
    return pl.pallas_call(
        kern,
        grid=(t // tm,),
        in_specs=[row(d)] + [_const_spec(c.shape) for c in consts]
        + [pl.BlockSpec((QK_ROPE // 2, tm), lambda i: (0, i))] * 2,
        out_specs=[slab(MLA_HEADS * HEAD_PAD), row(MLA_HEADS * HEAD_PAD), slab(MLA_HEADS * V_HEAD), row(d), row(d)],
        out_shape=[jax.ShapeDtypeStruct((t // seq, tps, MLA_HEADS * HEAD_PAD, tm), BF16),
                   jax.ShapeDtypeStruct((t, MLA_HEADS * HEAD_PAD), BF16),
                   jax.ShapeDtypeStruct((t // seq, tps, MLA_HEADS * V_HEAD, tm), BF16),
                   jax.ShapeDtypeStruct((t, d), BF16),
                   jax.ShapeDtypeStruct((t, d), BF16)],
        scratch_shapes=[pltpu.VMEM((8, CONV_WIDTH), F32), pltpu.VMEM((tm, SG_WIDTH), BF16)],
        compiler_params=pltpu.CompilerParams(dimension_semantics=("arbitrary",), vmem_limit_bytes=VMEM_LIMIT),
        name="mixer_pre",
    )(x2, *consts, cos_t, sin_t)


ACC_ROWS = V_HEAD + 16


def _tile_start(j, size):
    return j * size if isinstance(j, int) else pl.multiple_of(j * size, size)


def _attn_kernel(qt_ref, k_ref, vt_ref, o_ref, m_ref, acc_ref, *bufs, tq, nq):
    (sa, mca), (sb, mcb), (sf0, mcf0), (sf1, mcf1) = [bufs[2 * n:2 * n + 2] for n in range(4)]
    buf_a, buf_b, first = (sa, mca), (sb, mcb), ((sf0, mcf0), (sf1, mcf1))

    def score(qi, j, buf, masked):
        s_ref, mc_ref = buf
        start = _tile_start(j, tq)
        for sub in range(2):
            k_j = k_ref[pl.ds(start, tq), sub * HEAD_PAD:(sub + 1) * HEAD_PAD]
            st = _dot(k_j, qt_ref[0, qi, sub * HEAD_PAD:(sub + 1) * HEAD_PAD, :])
            if masked:
                kv_i = lax.broadcasted_iota(jnp.int32, st.shape, 0)
                q_i = lax.broadcasted_iota(jnp.int32, st.shape, 1)
                st = jnp.where(kv_i <= q_i, st, MASK_VALUE)
            s_ref[sub] = st
            mc_ref[sub] = jnp.broadcast_to(jnp.max(st, axis=0, keepdims=True), (8, tq))

    def value(j, buf):
        s_ref, mc_ref = buf
        for sub in range(2):
            m_prev = m_ref[sub]
            m_next = jnp.maximum(m_prev, mc_ref[sub])
            alpha = jnp.exp2(m_prev - m_next)
            pt = jnp.exp2((s_ref[sub] - m_next[0:1, :]).astype(BF16))
            m_ref[sub] = m_next
            vt_j = jnp.concatenate([vt_ref[0, j, sub * V_HEAD:(sub + 1) * V_HEAD, :],
                                    jnp.ones((ACC_ROWS - V_HEAD, tq), BF16)], axis=0)
            acc_ref[sub] = alpha[0:1, :] * acc_ref[sub] + _dot(vt_j, pt)

    def q_tile(qi, par, has_next=True):
        cur, nxt = first[par], first[1 - par]
        m_ref[...] = jnp.full_like(m_ref, MASK_VALUE)
        acc_ref[...] = jnp.zeros_like(acc_ref)

        def prefetch():
            if has_next:
                score(qi + 1, 0, nxt, False)

        def ramp(first_tail_tile):
            score(qi, 1, buf_a, False)
            value(0, cur)

            def body(p, c):
                score(qi, 2 * p + 2, buf_b, False)
                value(2 * p + 1, buf_a)
                score(qi, 2 * p + 3, buf_a, False)
                value(2 * p + 2, buf_b)
                return c

            lax.fori_loop(0, (first_tail_tile - 1) // 2, body, 0)

        if par == 0:
            @pl.when(qi == 0)
            def _():
                prefetch()
                value(0, cur)

            @pl.when(qi >= 2)
            def _():
                ramp(qi - 1)
                score(qi, qi, buf_b, True)
                value(qi - 1, buf_a)
                prefetch()
                value(qi, buf_b)
        else:
            @pl.when(qi == 1)
            def _():
                score(qi, 1, buf_a, True)
                value(0, cur)
                prefetch()
                value(1, buf_a)

            @pl.when(qi >= 3)
            def _():
                ramp(qi - 2)
                score(qi, qi - 1, buf_b, False)
                value(qi - 2, buf_a)
                score(qi, qi, buf_a, True)
                value(qi - 1, buf_b)
                prefetch()
                value(qi, buf_a)

        ot = jnp.concatenate([acc_ref[sub, 0:V_HEAD, :] / acc_ref[sub, V_HEAD:V_HEAD + 1, :]
                              for sub in range(2)], axis=0)
        o_ref[pl.ds(_tile_start(qi, tq), tq), :] = ot.T.astype(BF16)

    score(0, 0, first[0], True)

    def q_pair(p, carry):
        q_tile(2 * p, 0)
        q_tile(2 * p + 1, 1)
        return carry

    lax.fori_loop(0, nq // 2 - 1, q_pair, 0)
    q_tile(nq - 2, 0)
    q_tile(nq - 1, 1, has_next=False)

def _attention(qt, k, vt, *, batch, seq, tq):
    t = k.shape[0]
    nq = seq // tq
    assert nq % 2 == 0
    pairs = MLA_HEADS // 2
    score_buf = [pltpu.VMEM((2, tq, tq), F32), pltpu.VMEM((2, 8, tq), F32)]
    return pl.pallas_call(
        functools.partial(_attn_kernel, tq=tq, nq=nq),
        grid=(batch, pairs),
        in_specs=[pl.BlockSpec((1, nq, 2 * HEAD_PAD, tq), lambda b, hp: (b, 0, hp, 0)),
                  pl.BlockSpec((seq, 2 * HEAD_PAD), lambda b, hp: (b, hp)),
                  pl.BlockSpec((1, nq, 2 * V_HEAD, tq), lambda b, hp: (b, 0, hp, 0))],
        out_specs=pl.BlockSpec((seq, 2 * V_HEAD), lambda b, hp: (b, hp)),
        out_shape=jax.ShapeDtypeStruct((t, MLA_HEADS * V_HEAD), BF16),
        scratch_shapes=[pltpu.VMEM((2, 8, tq), F32), pltpu.VMEM((2, ACC_ROWS, tq), F32)] + score_buf * 4,
        compiler_params=pltpu.CompilerParams(dimension_semantics=("arbitrary",) * 2, vmem_limit_bytes=VMEM_LIMIT),
        name="attention",
    )(qt, k, vt)


def _pack_pair(lo, hi):
    lo_b = lax.bitcast_convert_type(lo, jnp.uint32)
    hi_b = lax.bitcast_convert_type(hi, jnp.uint32)
    return (hi_b & jnp.uint32(0xFFFF0000)) | (lo_b >> 16)


def _unpack_pair(p):
    lo = lax.bitcast_convert_type(p << 16, F32)
    hi = lax.bitcast_convert_type(p & jnp.uint32(0xFFFF0000), F32)
    return lo, hi


def _store_rows(ref, first, n_rows, packed):
    for c in range(ROW_CHUNKS):
        ref[pl.ds(first * ROW_CHUNKS + c, n_rows, stride=ROW_CHUNKS), :] = packed[:, c * LANES:(c + 1) * LANES]


def _load_rows(ref, first, n_rows):
    return jnp.concatenate([ref[pl.ds(first * ROW_CHUNKS + c, n_rows, stride=ROW_CHUNKS), :]
                            for c in range(ROW_CHUNKS)], axis=1)


def _route_t(lt):
    big = float(LANES)
    g = lt[GROUP_ROW0:GROUP_ROW0 + 8]
    g_row = lax.broadcasted_iota(jnp.int32, g.shape, 0).astype(F32)
    g_log = jnp.where(g_row < float(N_GROUPS), g, MASK_VALUE)
    g_max = jnp.max(g_log, axis=0, keepdims=True)
    g_p = 1.0 / jnp.sum(jnp.exp(g_log - g_max), axis=0, keepdims=True)
    g_idx = jnp.min(jnp.where(g_log == g_max, g_row, big), axis=0, keepdims=True)
    e = lt[0:N_EXPERTS]
    e_row = lax.broadcasted_iota(jnp.int32, e.shape, 0).astype(F32)
    lo = g_idx * float(EXPERTS_PER_GROUP)
    in_grp = (e_row >= lo) & (e_row < lo + float(EXPERTS_PER_GROUP))
    e_log = jnp.where(in_grp, e, MASK_VALUE)
    e_max = jnp.max(e_log, axis=0, keepdims=True)
    e_exp = jnp.exp(e_log - e_max)
    e_prob = e_exp / jnp.sum(e_exp, axis=0, keepdims=True)
    cand = jnp.where(in_grp, e_prob, -1.0)
    p1 = jnp.max(cand, axis=0, keepdims=True)
    i1 = jnp.min(jnp.where(cand == p1, e_row, big), axis=0, keepdims=True)
    cand2 = jnp.where(e_row == i1, -1.0, cand)
    p2 = jnp.max(cand2, axis=0, keepdims=True)
    i2 = jnp.min(jnp.where(cand2 == p2, e_row, big), axis=0, keepdims=True)
    denom = p1 + p2
    return g_p * (p1 / denom), g_p * (p2 / denom), i1, i2, e_row


_L_W1, _L_W2 = 0, 1
_M_E1, _M_E2, _M_R1, _M_R2 = range(4)


def _mixer_post_kernel(o_ref, part_ref, ga_ref, x_ref, wa_ref, wo_ref, nf_ref, wrt_ref, brt_ref,
                       x1_ref, h2p_ref, info_ref, meta_ref, cnt_ref, run_ref, *, tm):
    i = pl.program_id(0)
    half = x_ref.shape[1] // 2

    @pl.when(i == 0)
    def _():
        run_ref[...] = jnp.zeros_like(run_ref)

    ya = _dot(o_ref[...], wa_ref[...])
    merged = ga_ref[...].astype(F32) * ya + part_ref[...].astype(F32)
    x1 = x_ref[...] + _dot(merged.astype(BF16), wo_ref[...])
    x1_ref[...] = x1
    h2 = _rms(x1, nf_ref[...]).astype(BF16)
    h2f = h2.astype(F32)
    _store_rows(h2p_ref, 0, tm, _pack_pair(h2f[:, :half], h2f[:, half:]))
    lt = lax.dot_general(wrt_ref[...], h2, (((1,), (1,)), ((), ())), preferred_element_type=F32) + brt_ref[...]
    w1, w2, i1, i2, e_row = _route_t(lt)

    oh1 = e_row == i1
    oh2 = e_row == i2
    both = jnp.where(oh1, 1.0, 0.0) + jnp.where(oh2, 1.0, 0.0)
    r_i = lax.broadcasted_iota(jnp.int32, (tm, tm), 0)
    c_i = lax.broadcasted_iota(jnp.int32, (tm, tm), 1)
    earlier = jnp.where(r_i < c_i, 1.0, 0.0).astype(BF16)
    run = run_ref[...]
    before = _dot(both.astype(BF16), earlier) + run[:, 0:1]
    rank1 = jnp.sum(jnp.where(oh1, before, 0.0), axis=0, keepdims=True)
    rank2 = jnp.sum(jnp.where(oh2, before, 0.0), axis=0, keepdims=True)
    run = run + jnp.sum(both, axis=1, keepdims=True)
    run_ref[...] = run
    cnt_ref[...] = run

    row8 = lax.broadcasted_iota(jnp.int32, (8, tm), 0)
    meta_ref[0] = jnp.where(row8 == _M_E1, i1, jnp.where(row8 == _M_E2, i2,
                            jnp.where(row8 == _M_R1, rank1, jnp.where(row8 == _M_R2, rank2, 0.0))))
    row128 = lax.broadcasted_iota(jnp.int32, (LANES, tm), 0)
    info_ref[...] = jnp.where(row128 == _L_W1, w1, jnp.where(row128 == _L_W2, w2, 0.0)).T


def _mixer_post(o, part, ga, x2, lw, *, tm):
    t, d = x2.shape
    row = lambda w: pl.BlockSpec((tm, w), lambda i: (i, 0))
    consts = [lw["w_a_out"], lw["w_o"], lw["norm_ffn"], lw["w_router_t"], lw["b_router_t"]]
    return pl.pallas_call(
        functools.partial(_mixer_post_kernel, tm=tm),
        grid=(t // tm,),
        in_specs=[row(o.shape[1]), row(d), row(d), row(d)] + [_const_spec(c.shape) for c in consts],
        out_specs=[row(d), pl.BlockSpec((tm * ROW_CHUNKS, LANES), lambda i: (i, 0)), row(LANES),
                   pl.BlockSpec((1, 8, tm), lambda i: (i, 0, 0)),
                   pl.BlockSpec((N_EXPERTS, LANES), lambda i: (0, 0))],
        out_shape=[jax.ShapeDtypeStruct((t, d), F32), jax.ShapeDtypeStruct((t * ROW_CHUNKS, LANES), jnp.uint32),
                   jax.ShapeDtypeStruct((t, LANES), F32), jax.ShapeDtypeStruct((t // tm, 8, tm), F32),
                   jax.ShapeDtypeStruct((N_EXPERTS, LANES), F32)],
        scratch_shapes=[pltpu.VMEM((N_EXPERTS, LANES), F32)],
        compiler_params=pltpu.CompilerParams(dimension_semantics=("arbitrary",), vmem_limit_bytes=VMEM_LIMIT),
        name="mixer_post",
    )(o, part, ga, x2, *consts)


DISPATCH_RING = 4


def _dispatch_kernel(slots_ref, h2p_hbm, xs_in_hbm, xs_hbm, buf, lsem, rsem, *, td):
    del xs_in_hbm
    i = pl.program_id(0)
    n = pl.num_programs(0)

    def load(tile, b):
        return pltpu.make_async_copy(h2p_hbm.at[pl.ds(tile * (td * ROW_CHUNKS), td * ROW_CHUNKS)], buf.at[b],
                                     lsem.at[b])

    def wait_rows(b):
        for _ in range(2):
            pltpu.make_async_copy(buf.at[b], xs_hbm.at[pl.ds(0, td * ROW_CHUNKS)], rsem.at[b]).wait()

    @pl.when(i == 0)
    def _():
        load(0, 0).start()
        load(1, 1).start()

    for k in range(DISPATCH_RING):
        tile = DISPATCH_RING * i + k
        load(tile, k).wait()
        for r in range(td):
            for a in range(2):
                dst = pl.multiple_of(slots_ref[0, 0, (2 * k + a) * td + r], ROW_CHUNKS)
                pltpu.make_async_copy(buf.at[k, pl.ds(r * ROW_CHUNKS, ROW_CHUNKS)],
                                      xs_hbm.at[pl.ds(dst, ROW_CHUNKS)], rsem.at[k]).start(priority=a)
        if k > 0:
            wait_rows(k - 1)
        else:
            @pl.when(i > 0)
            def _():
                wait_rows(DISPATCH_RING - 1)
        if k + 2 < DISPATCH_RING:
            load(tile + 2, k + 2).start()
        else:
            @pl.when(i + 1 < n)
            def _():
                load(tile + 2, (k + 2) % DISPATCH_RING).start()

    @pl.when(i == n - 1)
    def _():
        wait_rows(DISPATCH_RING - 1)


def _dispatch(h2p, slots, xs_init, *, td):
    t = h2p.shape[0] // ROW_CHUNKS
    steps = t // (DISPATCH_RING * td)
    return pl.pallas_call(
        functools.partial(_dispatch_kernel, td=td),
        grid=(steps,),
        in_specs=[pl.BlockSpec((1, 1, DISPATCH_RING * 2 * td), lambda i: (i, 0, 0), memory_space=pltpu.SMEM),
                  pl.BlockSpec(memory_space=pl.ANY),
                  pl.BlockSpec(memory_space=pl.ANY)],
        out_specs=pl.BlockSpec(memory_space=pl.ANY),
        out_shape=jax.ShapeDtypeStruct(xs_init.shape, jnp.uint32),
        scratch_shapes=[pltpu.VMEM((DISPATCH_RING, td * ROW_CHUNKS, LANES), jnp.uint32),
                        pltpu.SemaphoreType.DMA((DISPATCH_RING,)), pltpu.SemaphoreType.DMA((DISPATCH_RING,))],
        input_output_aliases={2: 0},
        compiler_params=pltpu.CompilerParams(dimension_semantics=("arbitrary",), vmem_limit_bytes=VMEM_LIMIT,
                                             has_side_effects=True),
        name="dispatch",
    )(slots.reshape(steps, 1, DISPATCH_RING * 2 * td), h2p, xs_init)


def _expert_kernel(te_ref, nu_ref, xs_ref, wgu_ref, wdn_ref, ys_ref, wgu_bf, wdn_bf, *, tme):
    i = pl.program_id(0)
    half = wgu_bf.shape[0] // 2

    @pl.when((i == 0) | (te_ref[i] != te_ref[jnp.maximum(i - 1, 0)]))
    def _():
        wgu_bf[...] = wgu_ref[0, 0].astype(BF16)
        wdn_bf[...] = wdn_ref[0, 0].astype(BF16)

    @pl.when(i < nu_ref[0])
    def _():
        lo, hi = _unpack_pair(_load_rows(xs_ref, 0, tme))
        gu = _dot(lo.astype(BF16), wgu_bf[:half, :]) + _dot(hi.astype(BF16), wgu_bf[half:, :])
        act = (jax.nn.silu(gu[:, :D_EXPERT]) * gu[:, D_EXPERT:]).astype(BF16)
        y = _dot(act, wdn_bf[...]).astype(BF16).astype(F32)
        _store_rows(ys_ref, 0, tme, _pack_pair(y[:, :half], y[:, half:]))

    @pl.when(i >= nu_ref[0])
    def _():
        ys_ref[...] = jnp.zeros_like(ys_ref)


def _experts(xs, tile_expert, n_used, lw, *, tme):
    n_slots = xs.shape[0] // ROW_CHUNKS
    layer, w_gate_up, w_down = lw["layer"], lw["w_gate_up"], lw["w_down"]
    d = w_gate_up.shape[2]
    grid_spec = pltpu.PrefetchScalarGridSpec(
        num_scalar_prefetch=2,
        grid=(n_slots // tme,),
        in_specs=[pl.BlockSpec((tme * ROW_CHUNKS, LANES), lambda i, te, nu: (i, 0)),
                  pl.BlockSpec((1, 1, d, 2 * D_EXPERT), lambda i, te, nu: (layer, te[i], 0, 0)),
                  pl.BlockSpec((1, 1, D_EXPERT, d), lambda i, te, nu: (layer, te[i], 0, 0))],
        out_specs=pl.BlockSpec((tme * ROW_CHUNKS, LANES), lambda i, te, nu: (i, 0)),
        scratch_shapes=[pltpu.VMEM((d, 2 * D_EXPERT), BF16), pltpu.VMEM((D_EXPERT, d), BF16)],
    )
    return pl.pallas_call(
        functools.partial(_expert_kernel, tme=tme),
        grid_spec=grid_spec,
        out_shape=jax.ShapeDtypeStruct(xs.shape, jnp.uint32),
        compiler_params=pltpu.CompilerParams(dimension_semantics=("arbitrary",), vmem_limit_bytes=VMEM_LIMIT),
        name="experts",
    )(tile_expert, n_used, xs, w_gate_up, w_down)


def _combine_kernel(s0_ref, s1_ref, s2_ref, x1_ref, info_ref, ys_hbm, fn_ref, out_ref, ya, yb, sem, *, tmc):
    i = pl.program_id(0)
    n = pl.num_programs(0)
    half = x1_ref.shape[1] // 2

    def gather(idx_ref, buf, k):
        for r in range(2 * tmc):
            src = pl.multiple_of(idx_ref[0, 0, r], ROW_CHUNKS)
            pltpu.make_async_copy(ys_hbm.at[pl.ds(src, ROW_CHUNKS)], buf.at[pl.ds(r * ROW_CHUNKS, ROW_CHUNKS)],
                                  sem.at[k]).start(priority=r % 2)

    def finish(buf, k, rows):
        pltpu.make_async_copy(ys_hbm.at[pl.ds(0, 2 * tmc * ROW_CHUNKS)], buf, sem.at[k]).wait()
        lo1, hi1 = _unpack_pair(_load_rows(buf, 0, tmc))
        lo2, hi2 = _unpack_pair(_load_rows(buf, tmc, tmc))
        info = info_ref[rows, :]
        w1 = info[:, _L_W1:_L_W1 + 1]
        w2 = info[:, _L_W2:_L_W2 + 1]
        lo = x1_ref[rows, :half] + w1 * lo1 + w2 * lo2
        hi = x1_ref[rows, half:] + w1 * hi1 + w2 * hi2
        if fn_ref is not None:
            ms = (jnp.sum(lo * lo, axis=-1, keepdims=True) + jnp.sum(hi * hi, axis=-1, keepdims=True)) / (2 * half)
            inv = lax.rsqrt(ms + EPS)
            lo = lo * inv * fn_ref[:, :half]
            hi = hi * inv * fn_ref[:, half:]
        out_ref[rows, :half] = lo
        out_ref[rows, half:] = hi

    @pl.when(i == 0)
    def _():
        gather(s0_ref, ya, 0)

    gather(s1_ref, yb, 1)
    finish(ya, 0, slice(0, tmc))

    @pl.when(i + 1 < n)
    def _():
        gather(s2_ref, ya, 0)

    finish(yb, 1, slice(tmc, 2 * tmc))


def _combine_plain_kernel(s0_ref, s1_ref, s2_ref, x1_ref, info_ref, ys_hbm, out_ref, ya, yb, sem, *, tmc):
    _combine_kernel(s0_ref, s1_ref, s2_ref, x1_ref, info_ref, ys_hbm, None, out_ref, ya, yb, sem, tmc=tmc)


def _combine(x1, info, ys, slots, fn, *, tmc, final_norm):
    t, d = x1.shape
    n_tiles = t // tmc
    n = n_tiles // 2
    smem = lambda f: pl.BlockSpec((1, 1, 2 * tmc), f, memory_space=pltpu.SMEM)
    in_specs = [smem(lambda i: (2 * i, 0, 0)), smem(lambda i: (2 * i + 1, 0, 0)),
                smem(lambda i: (jnp.minimum(2 * i + 2, n_tiles - 1), 0, 0)),
                pl.BlockSpec((2 * tmc, d), lambda i: (i, 0)),
                pl.BlockSpec((2 * tmc, LANES), lambda i: (i, 0)),
                pl.BlockSpec(memory_space=pl.ANY)]
    args = [slots, slots, slots, x1, info, ys]
    if final_norm:
        in_specs.append(pl.BlockSpec((1, d), lambda i: (0, 0)))
        args.append(fn)
    return pl.pallas_call(
        functools.partial(_combine_kernel if final_norm else _combine_plain_kernel, tmc=tmc),
        grid=(n,),
        in_specs=in_specs,
        out_specs=pl.BlockSpec((2 * tmc, d), lambda i: (i, 0)),
        out_shape=jax.ShapeDtypeStruct((t, d), F32),
        scratch_shapes=[pltpu.VMEM((2 * tmc * ROW_CHUNKS, LANES), jnp.uint32),
                        pltpu.VMEM((2 * tmc * ROW_CHUNKS, LANES), jnp.uint32), pltpu.SemaphoreType.DMA((2,))],
        compiler_params=pltpu.CompilerParams(dimension_semantics=("arbitrary",), vmem_limit_bytes=VMEM_LIMIT),
        name="combine",
    )(*args)


def _moe(x1, h2p, info, meta, cnt, lw, fn, xs_init, *, tme, tmc, final_norm):
    t = x1.shape[0]
    n_tiles = (2 * t) // tme + N_EXPERTS
    counts = cnt[:, 0].astype(jnp.int32)
    padded = ((counts + tme - 1) // tme) * tme
    ends = jnp.cumsum(padded)
    starts = ends - padded
    mi = meta.astype(jnp.int32)
    expert_ids = jnp.arange(N_EXPERTS, dtype=jnp.int32)

    def slot_of(e, rank):
        start = jnp.sum(jnp.where(e[..., None] == expert_ids, starts, 0), axis=-1)
        return ((start + rank) * ROW_CHUNKS).reshape(t // tmc, 1, tmc)

    slot1 = slot_of(mi[:, _M_E1], mi[:, _M_R1])
    slot2 = slot_of(mi[:, _M_E2], mi[:, _M_R2])
    slots = jnp.concatenate([slot1, slot2], axis=2)
    tile_start = jnp.arange(n_tiles, dtype=jnp.int32) * tme
    tile_expert = jnp.minimum(jnp.sum((ends[None, :] <= tile_start[:, None]).astype(jnp.int32), axis=1),
                              N_EXPERTS - 1)
    n_used = ends[-1:] // tme
    xs = _dispatch(h2p, slots, xs_init, td=tmc)
    ys = _experts(xs, tile_expert, n_used, lw, tme=tme)
    return _combine(x1, info, ys, slots, fn, tmc=tmc, final_norm=final_norm), xs


def _pack_layer(l, w_in, norm_mix, q_norm, w_q_up, kv_norm, w_kv_up, sg_norm, w_s, b_s, conv_w,
                w_a_out, w_b_out, w_c_out, w_o, norm_ffn, w_group_router, b_group_router,
                w_expert_router, b_expert_router, w_gate_up, w_down):
    d = w_in.shape[1]
    wi = w_in[l]
    o_kr = Q_RANK + KV_RANK
    kr_slot = jnp.zeros((d, HEAD_PAD), F32).at[:, QK_NOPE:QK_HEAD].set(wi[:, o_kr:o_kr + QK_ROPE])
    w_in_p = jnp.concatenate([wi[:, :o_kr], kr_slot, wi[:, o_kr + QK_ROPE:]], axis=1).astype(BF16)
    wq = w_q_up[l].reshape(Q_RANK, MLA_HEADS, QK_HEAD)
    wq = jnp.pad(wq, ((0, 0), (0, 0), (0, HEAD_PAD - QK_HEAD))).reshape(Q_RANK, MLA_HEADS * HEAD_PAD)
    wkv = w_kv_up[l].reshape(KV_RANK, MLA_HEADS, QK_NOPE + V_HEAD)
    wk = jnp.pad(wkv[:, :, :QK_NOPE], ((0, 0), (0, 0), (0, HEAD_PAD - QK_NOPE)))
    wk = wk.reshape(KV_RANK, MLA_HEADS * HEAD_PAD)
    wv = wkv[:, :, QK_NOPE:].reshape(KV_RANK, MLA_HEADS * V_HEAD)
    w_r = jnp.zeros((LANES, d), F32)
    w_r = w_r.at[:N_EXPERTS].set(w_expert_router[l].T)
    w_r = w_r.at[GROUP_ROW0:GROUP_ROW0 + N_GROUPS].set(w_group_router[l].T)
    b_r = jnp.zeros((LANES, 1), F32)
    b_r = b_r.at[:N_EXPERTS, 0].set(b_expert_router[l])
    b_r = b_r.at[GROUP_ROW0:GROUP_ROW0 + N_GROUPS, 0].set(b_group_router[l])
    return {
        "norm_mix": norm_mix[l][None, :], "w_in": w_in_p,
        "q_norm": q_norm[l][None, :], "w_q": wq.astype(BF16),
        "kv_norm": kv_norm[l][None, :], "w_k": wk.astype(BF16), "w_v": wv.astype(BF16),
        "sg_norm": sg_norm[l][None, :], "w_s": w_s[l],
        "b_s": jnp.broadcast_to(b_s[l][:, :, None], (SG_GROUPS, CHUNK, SG_GROUP_DIM)),
        "conv_w": conv_w[l],
        "w_a_out": w_a_out[l].astype(BF16), "w_b_out": w_b_out[l].astype(BF16),
        "w_c_out": w_c_out[l].astype(BF16), "w_o": w_o[l].astype(BF16),
        "norm_ffn": norm_ffn[l][None, :], "w_router_t": w_r.astype(BF16), "b_router_t": b_r,
        "layer": l, "w_gate_up": w_gate_up, "w_down": w_down,
    }


def _rope_tables(positions):
    inv = ROPE_BASE ** (-jnp.arange(0, QK_ROPE, 2, dtype=F32) / QK_ROPE)
    ang = inv[:, None] * positions.astype(F32).reshape(-1)[None, :]
    return jnp.cos(ang), jnp.sin(ang)


def kernel(x, positions, norm_mix, w_in, q_norm, w_q_up, kv_norm, w_kv_up, sg_norm, w_s, b_s, conv_w,
           w_a_out, w_b_out, w_c_out, w_o, norm_ffn, w_group_router, b_group_router, w_expert_router,
           b_expert_router, w_gate_up, w_down, final_norm):
    batch, seq, d = x.shape
    depth = w_in.shape[0]
    tm = min(512, seq)
    tq = tm
    tme = 512
    tmc = min(256, seq)
    cos_t, sin_t = _rope_tables(positions)
    x2 = x.reshape(batch * seq, d)
    fn = final_norm[None, :]
    xs = jnp.zeros(((2 * batch * seq + N_EXPERTS * tme) * ROW_CHUNKS, LANES), jnp.uint32)
    for l in range(depth):
        lw = _pack_layer(l, w_in, norm_mix, q_norm, w_q_up, kv_norm, w_kv_up, sg_norm, w_s, b_s, conv_w,
                         w_a_out, w_b_out, w_c_out, w_o, norm_ffn, w_group_router, b_group_router,
                         w_expert_router, b_expert_router, w_gate_up, w_down)
        q, k, v, part, ga = _mixer_pre(x2, lw, cos_t, sin_t, seq=seq, tm=tm)
        o = _attention(q, k, v, batch=batch, seq=seq, tq=tq)
        x1, h2p, info, meta, cnt = _mixer_post(o, part, ga, x2, lw, tm=tm)
        x2, xs = _moe(x1, h2p, info, meta, cnt, lw, fn, xs, tme=tme, tmc=tmc, final_norm=(l == depth - 1))
    return x2.reshape(batch, seq, d)
```

```python
import functools

import jax
import jax.numpy as jnp
from jax import lax
from jax.experimental import pallas as pl
from jax.experimental.pallas import tpu as pltpu

F32 = jnp.float32
BF16 = jnp.bfloat16

EPS = 1e-6
MLA_HEADS = 8
QK_NOPE = 64
QK_ROPE = 32
QK_HEAD = QK_NOPE + QK_ROPE
V_HEAD = 64
Q_RANK = 384
KV_RANK = 256
ROPE_BASE = 10000.0
SG_GROUPS = 4
SG_GROUP_DIM = 128
SG_WIDTH = SG_GROUPS * SG_GROUP_DIM
CHUNK = 128
CONV_WIDTH = 512
CONV_K = 3
N_GROUPS = 4
EXPERTS_PER_GROUP = 8
N_EXPERTS = N_GROUPS * EXPERTS_PER_GROUP
D_EXPERT = 256

LANES = 128
HEAD_PAD = 128
MASK_VALUE = -1e30
LOG2_E = 1.4426950408889634
ROW_CHUNKS = 8
VMEM_LIMIT = 56 * 1024 * 1024

_P1 = Q_RANK + KV_RANK + HEAD_PAD
_P2 = _P1 + 2 * SG_WIDTH
_P3 = _P2 + 3 * CONV_WIDTH
GROUP_ROW0 = N_EXPERTS


def _dot(a, b):
    return jnp.dot(a, b, preferred_element_type=F32)


def _rms(xf, g):
    return xf * lax.rsqrt(jnp.mean(xf * xf, axis=-1, keepdims=True) + EPS) * g


def _gelu(x):
    return 0.5 * x * (1.0 + jnp.tanh(0.7978845608028654 * (x + 0.044715 * (x * x * x))))


def _rope_t(tt, cos, sin):
    lo, mid = QK_NOPE, QK_NOPE + QK_ROPE // 2
    x1, x2 = tt[lo:mid], tt[mid:QK_HEAD]
    return jnp.concatenate([tt[:lo], x1 * cos - x2 * sin, x2 * cos + x1 * sin, tt[QK_HEAD:]], axis=0)


def _const_spec(shape):
    nd = len(shape)
    return pl.BlockSpec(shape, lambda *_: (0,) * nd, pipeline_mode=pl.Buffered(1))


def _mixer_pre_kernel(x_ref, nm_ref, win_ref, qn_ref, wq_ref, kvn_ref, wk_ref, wv_ref, sgn_ref,
                      ws_ref, bs_ref, cw_ref, wb_ref, wc_ref, cos_ref, sin_ref,
                      qt_ref, k_ref, vt_ref, part_ref, ga_ref,
                      halo_ref, ybin_ref, *, tiles_per_seq, tm):
    i = pl.program_id(0)
    d_model = x_ref.shape[1]
    h = _rms(x_ref[...], nm_ref[...]).astype(BF16)

    p1 = _dot(h, win_ref[:, 0:_P1])
    cos, sin = cos_ref[...], sin_ref[...]
    qn = _rms(p1[:, :Q_RANK], qn_ref[...]).astype(BF16)
    qf = _dot(qn, wq_ref[...])
    scale = QK_HEAD ** -0.5 * LOG2_E
    for hd in range(MLA_HEADS):
        sl = slice(hd * HEAD_PAD, (hd + 1) * HEAD_PAD)
        qt_ref[0, 0, sl, :] = (_rope_t(qf[:, sl].T, cos, sin) * scale).astype(BF16)
    kvn = _rms(p1[:, Q_RANK:Q_RANK + KV_RANK], kvn_ref[...]).astype(BF16)
    kr = _rope_t(p1[:, Q_RANK + KV_RANK:_P1].T, cos, sin).T
    kf = _dot(kvn, wk_ref[...])
    for hd in range(MLA_HEADS):
        sl = slice(hd * HEAD_PAD, (hd + 1) * HEAD_PAD)
        k_ref[:, sl] = (kf[:, sl] + kr).astype(BF16)
    vt_ref[0, 0] = _dot(kvn, wv_ref[...]).T.astype(BF16)

    p2 = _dot(h, win_ref[:, _P1:_P2])
    u = _gelu(p2[:, :SG_WIDTH])
    vb = _rms(_gelu(p2[:, SG_WIDTH:]), sgn_ref[...]).astype(BF16)
    r_i = lax.broadcasted_iota(jnp.int32, (CHUNK, CHUNK), 0)
    c_i = lax.broadcasted_iota(jnp.int32, (CHUNK, CHUNK), 1)
    for g in range(SG_GROUPS):
        wg = jnp.where(c_i <= r_i, ws_ref[g], 0.0).astype(BF16)
        gs = slice(g * SG_GROUP_DIM, (g + 1) * SG_GROUP_DIM)
        n_chunks = tm // CHUNK
        v_side = jnp.concatenate([vb[c * CHUNK:(c + 1) * CHUNK, gs] for c in range(n_chunks)], axis=1)
        mixed_side = _dot(wg, v_side)
        for c in range(n_chunks):
            cs = slice(c * CHUNK, (c + 1) * CHUNK)
            mixed = mixed_side[:, c * SG_GROUP_DIM:(c + 1) * SG_GROUP_DIM] + bs_ref[g]
            ybin_ref[cs, gs] = (u[cs, gs] * mixed).astype(BF16)

    @pl.when(i % tiles_per_seq == 0)
    def _():
        halo_ref[...] = jnp.zeros_like(halo_ref)

    p3 = _dot(h, win_ref[:, _P2:_P3])
    z = p3[:, CONV_WIDTH:2 * CONV_WIDTH] * p3[:, 2 * CONV_WIDTH:]
    row = lax.broadcasted_iota(jnp.int32, z.shape, 0)
    halo = halo_ref[...]
    last1 = halo[7:8, :]
    last2 = halo[6:7, :]
    z1 = jnp.where(row == 0, last1, pltpu.roll(z, 1, 0))
    z2 = jnp.where(row == 0, last2, jnp.where(row == 1, last1, pltpu.roll(z, 2, 0)))
    y = cw_ref[0:1, :] * z2 + cw_ref[1:2, :] * z1 + cw_ref[2:3, :] * z
    yc_in = (p3[:, :CONV_WIDTH] * y).astype(BF16)
    halo_ref[...] = z[tm - 8:tm, :]

    p4 = _dot(h, win_ref[:, _P3:_P3 + 3 * d_model])
    yb = _dot(ybin_ref[...], wb_ref[...])
    yc = _dot(yc_in, wc_ref[...])
    part = (jax.nn.sigmoid(p4[:, d_model:2 * d_model]) * yb
            + jax.nn.sigmoid(p4[:, 2 * d_model:]) * yc)
    part_ref[...] = part.astype(BF16)
    ga_ref[...] = jax.nn.sigmoid(p4[:, :d_model]).astype(BF16)


def _mixer_pre(x2, lw, cos_t, sin_t, *, seq, tm):
    t, d = x2.shape
    row = lambda w: pl.BlockSpec((tm, w), lambda i: (i, 0))
    consts = [lw["norm_mix"], lw["w_in"], lw["q_norm"], lw["w_q"], lw["kv_norm"], lw["w_k"], lw["w_v"],
              lw["sg_norm"], lw["w_s"], lw["b_s"], lw["conv_w"], lw["w_b_out"], lw["w_c_out"]]
    assert lw["conv_w"].shape[0] == CONV_K == 3
    tps = seq // tm
    kern = functools.partial(_mixer_pre_kernel, tiles_per_seq=tps, tm=tm)
    slab = lambda w: pl.BlockSpec((1, 1, w, tm), lambda i: (i // tps, i % tps, 0, 0))
    return pl.pallas_call(
        kern,
        grid=(t // tm,),
        in_specs=[row(d)] + [_const_spec(c.shape) for c in consts]
        + [pl.BlockSpec((QK_ROPE // 2, tm), lambda i: (0, i))] * 2,
        out_specs=[slab(MLA_HEADS * HEAD_PAD), row(MLA_HEADS * HEAD_PAD), slab(MLA_HEADS * V_HEAD), row(d), row(d)],
        out_shape=[jax.ShapeDtypeStruct((t // seq, tps, MLA_HEADS * HEAD_PAD, tm), BF16),
                   jax.ShapeDtypeStruct((t, MLA_HEADS * HEAD_PAD), BF16),
                   jax.ShapeDtypeStruct((t // seq, tps, MLA_HEADS * V_HEAD, tm), BF16),
                   jax.ShapeDtypeStruct((t, d), BF16),
                   jax.ShapeDtypeStruct((t, d), BF16)],
        scratch_shapes=[pltpu.VMEM((8, CONV_WIDTH), F32), pltpu.VMEM((tm, SG_WIDTH), BF16)],
        compiler_params=pltpu.CompilerParams(dimension_semantics=("arbitrary",), vmem_limit_bytes=VMEM_LIMIT),
        name="mixer_pre",
    )(x2, *consts, cos_t, sin_t)


ACC_ROWS = V_HEAD + 16


def _tile_start(j, size):
    return j * size if isinstance(j, int) else pl.multiple_of(j * size, size)


def _attn_kernel(qt_ref, k_ref, vt_ref, o_ref, m_ref, acc_ref, *bufs, tq, nq):
    (sa, mca), (sb, mcb), (sf0, mcf0), (sf1, mcf1) = [bufs[2 * n:2 * n + 2] for n in range(4)]
    buf_a, buf_b, first = (sa, mca), (sb, mcb), ((sf0, mcf0), (sf1, mcf1))

    def score(qi, j, buf, masked):
        s_ref, mc_ref = buf
        start = _tile_start(j, tq)
        for sub in range(2):
            k_j = k_ref[pl.ds(start, tq), sub * HEAD_PAD:(sub + 1) * HEAD_PAD]
            st = _dot(k_j, qt_ref[0, qi, sub * HEAD_PAD:(sub + 1) * HEAD_PAD, :])
            if masked:
                kv_i = lax.broadcasted_iota(jnp.int32, st.shape, 0)
                q_i = lax.broadcasted_iota(jnp.int32, st.shape, 1)
                st = jnp.where(kv_i <= q_i, st, MASK_VALUE)
            s_ref[sub] = st
            mc_ref[sub] = jnp.broadcast_to(jnp.max(st, axis=0, keepdims=True), (8, tq))

    def value(j, buf):
        s_ref, mc_ref = buf
        for sub in range(2):
            m_prev = m_ref[sub]
            m_next = jnp.maximum(m_prev, mc_ref[sub])
            alpha = jnp.exp2(m_prev - m_next)
            pt = jnp.exp2((s_ref[sub] - m_next[0:1, :]).astype(BF16))
            m_ref[sub] = m_next
            vt_j = jnp.concatenate([vt_ref[0, j, sub * V_HEAD:(sub + 1) * V_HEAD, :],
                                    jnp.ones((ACC_ROWS - V_HEAD, tq), BF16)], axis=0)
            acc_ref[sub] = alpha[0:1, :] * acc_ref[sub] + _dot(vt_j, pt)

    def q_tile(qi, par, has_next=True):
        cur, nxt = first[par], first[1 - par]
        m_ref[...] = jnp.full_like(m_ref, MASK_VALUE)
        acc_ref[...] = jnp.zeros_like(acc_ref)

        def prefetch():
            if has_next:
                score(qi + 1, 0, nxt, False)

        def ramp(first_tail_tile):
            score(qi, 1, buf_a, False)
            value(0, cur)

            def body(p, c):
                score(qi, 2 * p + 2, buf_b, False)
                value(2 * p + 1, buf_a)
                score(qi, 2 * p + 3, buf_a, False)
                value(2 * p + 2, buf_b)
                return c

            lax.fori_loop(0, (first_tail_tile - 1) // 2, body, 0)

        if par == 0:
            @pl.when(qi == 0)
            def _():
                prefetch()
                value(0, cur)

            @pl.when(qi >= 2)
            def _():
                ramp(qi - 1)
                score(qi, qi, buf_b, True)
                value(qi - 1, buf_a)
                prefetch()
                value(qi, buf_b)
        else:
            @pl.when(qi == 1)
            def _():
                score(qi, 1, buf_a, True)
                value(0, cur)
                prefetch()
                value(1, buf_a)

            @pl.when(qi >= 3)
            def _():
                ramp(qi - 2)
                score(qi, qi - 1, buf_b, False)
                value(qi - 2, buf_a)
                score(qi, qi, buf_a, True)
                value(qi - 1, buf_b)
                prefetch()
                value(qi, buf_a)

        ot = jnp.concatenate([acc_ref[sub, 0:V_HEAD, :] / acc_ref[sub, V_HEAD:V_HEAD + 1, :]
                              for sub in range(2)], axis=0)
        o_ref[pl.ds(_tile_start(qi, tq), tq), :] = ot.T.astype(BF16)

    score(0, 0, first[0], True)

    def q_pair(p, carry):
        q_tile(2 * p, 0)
        q_tile(2 * p + 1, 1)
        return carry

    lax.fori_loop(0, nq // 2 - 1, q_pair, 0)
    q_tile(nq - 2, 0)
    q_tile(nq - 1, 1, has_next=False)

def _attention(qt, k, vt, *, batch, seq, tq):
    t = k.shape[0]
    nq = seq // tq
    assert nq % 2 == 0
    pairs = MLA_HEADS // 2
    score_buf = [pltpu.VMEM((2, tq, tq), F32), pltpu.VMEM((2, 8, tq), F32)]
    return pl.pallas_call(
        functools.partial(_attn_kernel, tq=tq, nq=nq),
        grid=(batch, pairs),
        in_specs=[pl.BlockSpec((1, nq, 2 * HEAD_PAD, tq), lambda b, hp: (b, 0, hp, 0)),
                  pl.BlockSpec((seq, 2 * HEAD_PAD), lambda b, hp: (b, hp)),
                  pl.BlockSpec((1, nq, 2 * V_HEAD, tq), lambda b, hp: (b, 0, hp, 0))],
        out_specs=pl.BlockSpec((seq, 2 * V_HEAD), lambda b, hp: (b, hp)),
        out_shape=jax.ShapeDtypeStruct((t, MLA_HEADS * V_HEAD), BF16),
        scratch_shapes=[pltpu.VMEM((2, 8, tq), F32), pltpu.VMEM((2, ACC_ROWS, tq), F32)] + score_buf * 4,
        compiler_params=pltpu.CompilerParams(dimension_semantics=("arbitrary",) * 2, vmem_limit_bytes=VMEM_LIMIT),
        name="attention",
    )(qt, k, vt)


def _store_rows(ref, first, n_rows, rows):
    for c in range(ROW_CHUNKS):
        ref[pl.ds(first * ROW_CHUNKS + c, n_rows, stride=ROW_CHUNKS), :] = rows[:, c * LANES:(c + 1) * LANES]


def _load_rows(ref, first, n_rows):
    return jnp.concatenate([ref[pl.ds(first * ROW_CHUNKS + c, n_rows, stride=ROW_CHUNKS), :]
                            for c in range(ROW_CHUNKS)], axis=1)


def _route_t(lt):
    big = float(LANES)
    g = lt[GROUP_ROW0:GROUP_ROW0 + 8]
    g_row = lax.broadcasted_iota(jnp.int32, g.shape, 0).astype(F32)
    g_log = jnp.where(g_row < float(N_GROUPS), g, MASK_VALUE)
    g_max = jnp.max(g_log, axis=0, keepdims=True)
    g_p = 1.0 / jnp.sum(jnp.exp(g_log - g_max), axis=0, keepdims=True)
    g_idx = jnp.min(jnp.where(g_log == g_max, g_row, big), axis=0, keepdims=True)
    e = lt[0:N_EXPERTS]
    e_row = lax.broadcasted_iota(jnp.int32, e.shape, 0).astype(F32)
    lo = g_idx * float(EXPERTS_PER_GROUP)
    in_grp = (e_row >= lo) & (e_row < lo + float(EXPERTS_PER_GROUP))
    e_log = jnp.where(in_grp, e, MASK_VALUE)
    e_max = jnp.max(e_log, axis=0, keepdims=True)
    e_exp = jnp.exp(e_log - e_max)
    e_prob = e_exp / jnp.sum(e_exp, axis=0, keepdims=True)
    cand = jnp.where(in_grp, e_prob, -1.0)
    p1 = jnp.max(cand, axis=0, keepdims=True)
    i1 = jnp.min(jnp.where(cand == p1, e_row, big), axis=0, keepdims=True)
    cand2 = jnp.where(e_row == i1, -1.0, cand)
    p2 = jnp.max(cand2, axis=0, keepdims=True)
    i2 = jnp.min(jnp.where(cand2 == p2, e_row, big), axis=0, keepdims=True)
    denom = p1 + p2
    return g_p * (p1 / denom), g_p * (p2 / denom), i1, i2, e_row


_L_W1, _L_W2 = 0, 1
_M_E1, _M_E2, _M_R1, _M_R2 = range(4)


def _mixer_post_kernel(o_ref, part_ref, ga_ref, x_ref, wa_ref, wo_ref, nf_ref, wrt_ref, brt_ref,
                       x1_ref, h2p_ref, info_ref, meta_ref, cnt_ref, run_ref, *, tm):
    i = pl.program_id(0)
    @pl.when(i == 0)
    def _():
        run_ref[...] = jnp.zeros_like(run_ref)

    ya = _dot(o_ref[...], wa_ref[...])
    merged = ga_ref[...].astype(F32) * ya + part_ref[...].astype(F32)
    x1 = x_ref[...] + _dot(merged.astype(BF16), wo_ref[...])
    x1_ref[...] = x1
    h2 = _rms(x1, nf_ref[...]).astype(BF16)
    _store_rows(h2p_ref, 0, tm, h2.astype(F32))
    lt = lax.dot_general(wrt_ref[...], h2, (((1,), (1,)), ((), ())), preferred_element_type=F32) + brt_ref[...]
    w1, w2, i1, i2, e_row = _route_t(lt)

    oh1 = e_row == i1
    oh2 = e_row == i2
    both = jnp.where(oh1, 1.0, 0.0) + jnp.where(oh2, 1.0, 0.0)
    r_i = lax.broadcasted_iota(jnp.int32, (tm, tm), 0)
    c_i = lax.broadcasted_iota(jnp.int32, (tm, tm), 1)
    earlier = jnp.where(r_i < c_i, 1.0, 0.0).astype(BF16)
    run = run_ref[...]
    before = _dot(both.astype(BF16), earlier) + run[:, 0:1]
    rank1 = jnp.sum(jnp.where(oh1, before, 0.0), axis=0, keepdims=True)
    rank2 = jnp.sum(jnp.where(oh2, before, 0.0), axis=0, keepdims=True)
    run = run + jnp.sum(both, axis=1, keepdims=True)
    run_ref[...] = run
    cnt_ref[...] = run

    row8 = lax.broadcasted_iota(jnp.int32, (8, tm), 0)
    meta_ref[0] = jnp.where(row8 == _M_E1, i1, jnp.where(row8 == _M_E2, i2,
                            jnp.where(row8 == _M_R1, rank1, jnp.where(row8 == _M_R2, rank2, 0.0))))
    row128 = lax.broadcasted_iota(jnp.int32, (LANES, tm), 0)
    info_ref[...] = jnp.where(row128 == _L_W1, w1, jnp.where(row128 == _L_W2, w2, 0.0)).T


def _mixer_post(o, part, ga, x2, lw, *, tm):
    t, d = x2.shape
    row = lambda w: pl.BlockSpec((tm, w), lambda i: (i, 0))
    consts = [lw["w_a_out"], lw["w_o"], lw["norm_ffn"], lw["w_router_t"], lw["b_router_t"]]
    return pl.pallas_call(
        functools.partial(_mixer_post_kernel, tm=tm),
        grid=(t // tm,),
        in_specs=[row(o.shape[1]), row(d), row(d), row(d)] + [_const_spec(c.shape) for c in consts],
        out_specs=[row(d), pl.BlockSpec((tm * ROW_CHUNKS, LANES), lambda i: (i, 0)), row(LANES),
                   pl.BlockSpec((1, 8, tm), lambda i: (i, 0, 0)),
                   pl.BlockSpec((N_EXPERTS, LANES), lambda i: (0, 0))],
        out_shape=[jax.ShapeDtypeStruct((t, d), F32), jax.ShapeDtypeStruct((t * ROW_CHUNKS, LANES), F32),
                   jax.ShapeDtypeStruct((t, LANES), F32), jax.ShapeDtypeStruct((t // tm, 8, tm), F32),
                   jax.ShapeDtypeStruct((N_EXPERTS, LANES), F32)],
        scratch_shapes=[pltpu.VMEM((N_EXPERTS, LANES), F32)],
        compiler_params=pltpu.CompilerParams(dimension_semantics=("arbitrary",), vmem_limit_bytes=VMEM_LIMIT),
        name="mixer_post",
    )(o, part, ga, x2, *consts)


DISPATCH_RING = 4


def _dispatch_kernel(slots_ref, h2p_hbm, xs_in_hbm, xs_hbm, buf, lsem, rsem, *, td):
    del xs_in_hbm
    i = pl.program_id(0)
    n = pl.num_programs(0)

    def load(tile, b):
        return pltpu.make_async_copy(h2p_hbm.at[pl.ds(tile * (td * ROW_CHUNKS), td * ROW_CHUNKS)], buf.at[b],
                                     lsem.at[b])

    def wait_rows(b):
        for _ in range(2):
            pltpu.make_async_copy(buf.at[b], xs_hbm.at[pl.ds(0, td * ROW_CHUNKS)], rsem.at[b]).wait()

    @pl.when(i == 0)
    def _():
        load(0, 0).start()
        load(1, 1).start()

    for k in range(DISPATCH_RING):
        tile = DISPATCH_RING * i + k
        load(tile, k).wait()
        for r in range(td):
            for a in range(2):
                dst = pl.multiple_of(slots_ref[0, 0, (2 * k + a) * td + r], ROW_CHUNKS)
                pltpu.make_async_copy(buf.at[k, pl.ds(r * ROW_CHUNKS, ROW_CHUNKS)],
                                      xs_hbm.at[pl.ds(dst, ROW_CHUNKS)], rsem.at[k]).start(priority=a)
        if k > 0:
            wait_rows(k - 1)
        else:
            @pl.when(i > 0)
            def _():
                wait_rows(DISPATCH_RING - 1)
        if k + 2 < DISPATCH_RING:
            load(tile + 2, k + 2).start()
        else:
            @pl.when(i + 1 < n)
            def _():
                load(tile + 2, (k + 2) % DISPATCH_RING).start()

    @pl.when(i == n - 1)
    def _():
        wait_rows(DISPATCH_RING - 1)


def _dispatch(h2p, slots, xs_init, *, td):
    t = h2p.shape[0] // ROW_CHUNKS
    steps = t // (DISPATCH_RING * td)
    return pl.pallas_call(
        functools.partial(_dispatch_kernel, td=td),
        grid=(steps,),
        in_specs=[pl.BlockSpec((1, 1, DISPATCH_RING * 2 * td), lambda i: (i, 0, 0), memory_space=pltpu.SMEM),
                  pl.BlockSpec(memory_space=pl.ANY),
                  pl.BlockSpec(memory_space=pl.ANY)],
        out_specs=pl.BlockSpec(memory_space=pl.ANY),
        out_shape=jax.ShapeDtypeStruct(xs_init.shape, F32),
        scratch_shapes=[pltpu.VMEM((DISPATCH_RING, td * ROW_CHUNKS, LANES), F32),
                        pltpu.SemaphoreType.DMA((DISPATCH_RING,)), pltpu.SemaphoreType.DMA((DISPATCH_RING,))],
        input_output_aliases={2: 0},
        compiler_params=pltpu.CompilerParams(dimension_semantics=("arbitrary",), vmem_limit_bytes=VMEM_LIMIT,
                                             has_side_effects=True),
        name="dispatch",
    )(slots.reshape(steps, 1, DISPATCH_RING * 2 * td), h2p, xs_init)


def _expert_kernel(te_ref, nu_ref, xs_ref, wgu_ref, wdn_ref, ys_ref, wgu_bf, wdn_bf, *, tme):
    i = pl.program_id(0)

    @pl.when((i == 0) | (te_ref[i] != te_ref[jnp.maximum(i - 1, 0)]))
    def _():
        wgu_bf[...] = wgu_ref[0, 0].astype(BF16)
        wdn_bf[...] = wdn_ref[0, 0].astype(BF16)

    @pl.when(i < nu_ref[0])
    def _():
        gu = _dot(_load_rows(xs_ref, 0, tme).astype(BF16), wgu_bf[...])
        act = (jax.nn.silu(gu[:, :D_EXPERT]) * gu[:, D_EXPERT:]).astype(BF16)
        y = _dot(act, wdn_bf[...]).astype(BF16).astype(F32)
        _store_rows(ys_ref, 0, tme, y)

    @pl.when(i >= nu_ref[0])
    def _():
        ys_ref[...] = jnp.zeros_like(ys_ref)


def _experts(xs, tile_expert, n_used, lw, *, tme):
    n_slots = xs.shape[0] // ROW_CHUNKS
    layer, w_gate_up, w_down = lw["layer"], lw["w_gate_up"], lw["w_down"]
    d = w_gate_up.shape[2]
    grid_spec = pltpu.PrefetchScalarGridSpec(
        num_scalar_prefetch=2,
        grid=(n_slots // tme,),
        in_specs=[pl.BlockSpec((tme * ROW_CHUNKS, LANES), lambda i, te, nu: (i, 0)),
                  pl.BlockSpec((1, 1, d, 2 * D_EXPERT), lambda i, te, nu: (layer, te[i], 0, 0)),
                  pl.BlockSpec((1, 1, D_EXPERT, d), lambda i, te, nu: (layer, te[i], 0, 0))],
        out_specs=pl.BlockSpec((tme * ROW_CHUNKS, LANES), lambda i, te, nu: (i, 0)),
        scratch_shapes=[pltpu.VMEM((d, 2 * D_EXPERT), BF16), pltpu.VMEM((D_EXPERT, d), BF16)],
    )
    return pl.pallas_call(
        functools.partial(_expert_kernel, tme=tme),
        grid_spec=grid_spec,
        out_shape=jax.ShapeDtypeStruct(xs.shape, F32),
        compiler_params=pltpu.CompilerParams(dimension_semantics=("arbitrary",), vmem_limit_bytes=VMEM_LIMIT),
        name="experts",
    )(tile_expert, n_used, xs, w_gate_up, w_down)


def _combine_kernel(s0_ref, s1_ref, s2_ref, x1_ref, info_ref, ys_hbm, fn_ref, out_ref, ya, yb, sem, *, tmc):
    i = pl.program_id(0)
    n = pl.num_programs(0)
    def gather(idx_ref, buf, k):
        for r in range(2 * tmc):
            src = pl.multiple_of(idx_ref[0, 0, r], ROW_CHUNKS)
            pltpu.make_async_copy(ys_hbm.at[pl.ds(src, ROW_CHUNKS)], buf.at[pl.ds(r * ROW_CHUNKS, ROW_CHUNKS)],
                                  sem.at[k]).start(priority=r % 2)

    def finish(buf, k, rows):
        pltpu.make_async_copy(ys_hbm.at[pl.ds(0, 2 * tmc * ROW_CHUNKS)], buf, sem.at[k]).wait()
        info = info_ref[rows, :]
        w1 = info[:, _L_W1:_L_W1 + 1]
        w2 = info[:, _L_W2:_L_W2 + 1]
        out = x1_ref[rows, :] + w1 * _load_rows(buf, 0, tmc) + w2 * _load_rows(buf, tmc, tmc)
        if fn_ref is not None:
            out = _rms(out, fn_ref[...])
        out_ref[rows, :] = out

    @pl.when(i == 0)
    def _():
        gather(s0_ref, ya, 0)

    gather(s1_ref, yb, 1)
    finish(ya, 0, slice(0, tmc))

    @pl.when(i + 1 < n)
    def _():
        gather(s2_ref, ya, 0)

    finish(yb, 1, slice(tmc, 2 * tmc))


def _combine_plain_kernel(s0_ref, s1_ref, s2_ref, x1_ref, info_ref, ys_hbm, out_ref, ya, yb, sem, *, tmc):
    _combine_kernel(s0_ref, s1_ref, s2_ref, x1_ref, info_ref, ys_hbm, None, out_ref, ya, yb, sem, tmc=tmc)


def _combine(x1, info, ys, slots, fn, *, tmc, final_norm):
    t, d = x1.shape
    n_tiles = t // tmc
    n = n_tiles // 2
    smem = lambda f: pl.BlockSpec((1, 1, 2 * tmc), f, memory_space=pltpu.SMEM)
    in_specs = [smem(lambda i: (2 * i, 0, 0)), smem(lambda i: (2 * i + 1, 0, 0)),
                smem(lambda i: (jnp.minimum(2 * i + 2, n_tiles - 1), 0, 0)),
                pl.BlockSpec((2 * tmc, d), lambda i: (i, 0)),
                pl.BlockSpec((2 * tmc, LANES), lambda i: (i, 0)),
                pl.BlockSpec(memory_space=pl.ANY)]
    args = [slots, slots, slots, x1, info, ys]
    if final_norm:
        in_specs.append(pl.BlockSpec((1, d), lambda i: (0, 0)))
        args.append(fn)
    return pl.pallas_call(
        functools.partial(_combine_kernel if final_norm else _combine_plain_kernel, tmc=tmc),
        grid=(n,),
        in_specs=in_specs,
        out_specs=pl.BlockSpec((2 * tmc, d), lambda i: (i, 0)),
        out_shape=jax.ShapeDtypeStruct((t, d), F32),
        scratch_shapes=[pltpu.VMEM((2 * tmc * ROW_CHUNKS, LANES), F32),
                        pltpu.VMEM((2 * tmc * ROW_CHUNKS, LANES), F32), pltpu.SemaphoreType.DMA((2,))],
        compiler_params=pltpu.CompilerParams(dimension_semantics=("arbitrary",), vmem_limit_bytes=VMEM_LIMIT),
        name="combine",
    )(*args)


def _moe(x1, h2p, info, meta, cnt, lw, fn, xs_init, *, tme, tmc, final_norm):
    t = x1.shape[0]
    n_tiles = (2 * t) // tme + N_EXPERTS
    counts = cnt[:, 0].astype(jnp.int32)
    padded = ((counts + tme - 1) // tme) * tme
    ends = jnp.cumsum(padded)
    starts = ends - padded
    mi = meta.astype(jnp.int32)
    expert_ids = jnp.arange(N_EXPERTS, dtype=jnp.int32)

    def slot_of(e, rank):
        start = jnp.sum(jnp.where(e[..., None] == expert_ids, starts, 0), axis=-1)
        return ((start + rank) * ROW_CHUNKS).reshape(t // tmc, 1, tmc)

    slot1 = slot_of(mi[:, _M_E1], mi[:, _M_R1])
    slot2 = slot_of(mi[:, _M_E2], mi[:, _M_R2])
    slots = jnp.concatenate([slot1, slot2], axis=2)
    tile_start = jnp.arange(n_tiles, dtype=jnp.int32) * tme
    tile_expert = jnp.minimum(jnp.sum((ends[None, :] <= tile_start[:, None]).astype(jnp.int32), axis=1),
                              N_EXPERTS - 1)
    n_used = ends[-1:] // tme
    xs = _dispatch(h2p, slots, xs_init, td=tmc)
    ys = _experts(xs, tile_expert, n_used, lw, tme=tme)
    return _combine(x1, info, ys, slots, fn, tmc=tmc, final_norm=final_norm), xs


def _pack_layer(l, w_in, norm_mix, q_norm, w_q_up, kv_norm, w_kv_up, sg_norm, w_s, b_s, conv_w,
                w_a_out, w_b_out, w_c_out, w_o, norm_ffn, w_group_router, b_group_router,
                w_expert_router, b_expert_router, w_gate_up, w_down):
    d = w_in.shape[1]
    wi = w_in[l]
    o_kr = Q_RANK + KV_RANK
    kr_slot = jnp.zeros((d, HEAD_PAD), F32).at[:, QK_NOPE:QK_HEAD].set(wi[:, o_kr:o_kr + QK_ROPE])
    w_in_p = jnp.concatenate([wi[:, :o_kr], kr_slot, wi[:, o_kr + QK_ROPE:]], axis=1).astype(BF16)
    wq = w_q_up[l].reshape(Q_RANK, MLA_HEADS, QK_HEAD)
    wq = jnp.pad(wq, ((0, 0), (0, 0), (0, HEAD_PAD - QK_HEAD))).reshape(Q_RANK, MLA_HEADS * HEAD_PAD)
    wkv = w_kv_up[l].reshape(KV_RANK, MLA_HEADS, QK_NOPE + V_HEAD)
    wk = jnp.pad(wkv[:, :, :QK_NOPE], ((0, 0), (0, 0), (0, HEAD_PAD - QK_NOPE)))
    wk = wk.reshape(KV_RANK, MLA_HEADS * HEAD_PAD)
    wv = wkv[:, :, QK_NOPE:].reshape(KV_RANK, MLA_HEADS * V_HEAD)
    w_r = jnp.zeros((LANES, d), F32)
    w_r = w_r.at[:N_EXPERTS].set(w_expert_router[l].T)
    w_r = w_r.at[GROUP_ROW0:GROUP_ROW0 + N_GROUPS].set(w_group_router[l].T)
    b_r = jnp.zeros((LANES, 1), F32)
    b_r = b_r.at[:N_EXPERTS, 0].set(b_expert_router[l])
    b_r = b_r.at[GROUP_ROW0:GROUP_ROW0 + N_GROUPS, 0].set(b_group_router[l])
    return {
        "norm_mix": norm_mix[l][None, :], "w_in": w_in_p,
        "q_norm": q_norm[l][None, :], "w_q": wq.astype(BF16),
        "kv_norm": kv_norm[l][None, :], "w_k": wk.astype(BF16), "w_v": wv.astype(BF16),
        "sg_norm": sg_norm[l][None, :], "w_s": w_s[l],
        "b_s": jnp.broadcast_to(b_s[l][:, :, None], (SG_GROUPS, CHUNK, SG_GROUP_DIM)),
        "conv_w": conv_w[l],
        "w_a_out": w_a_out[l].astype(BF16), "w_b_out": w_b_out[l].astype(BF16),
        "w_c_out": w_c_out[l].astype(BF16), "w_o": w_o[l].astype(BF16),
        "norm_ffn": norm_ffn[l][None, :], "w_router_t": w_r.astype(BF16), "b_router_t": b_r,
        "layer": l, "w_gate_up": w_gate_up, "w_down": w_down,
    }


def _rope_tables(positions):
    inv = ROPE_BASE ** (-jnp.arange(0, QK_ROPE, 2, dtype=F32) / QK_ROPE)
    ang = inv[:, None] * positions.astype(F32).reshape(-1)[None, :]
    return jnp.cos(ang), jnp.sin(ang)


def kernel(x, positions, norm_mix, w_in, q_norm, w_q_up, kv_norm, w_kv_up, sg_norm, w_s, b_s, conv_w,
           w_a_out, w_b_out, w_c_out, w_o, norm_ffn, w_group_router, b_group_router, w_expert_router,
           b_expert_router, w_gate_up, w_down, final_norm):
    batch, seq, d = x.shape
    depth = w_in.shape[0]
    tm = min(512, seq)
    tq = tm
    tme = 512
    tmc = min(256, seq)
    cos_t, sin_t = _rope_tables(positions)
    x2 = x.reshape(batch * seq, d)
    fn = final_norm[None, :]
    xs = jnp.zeros(((2 * batch * seq + N_EXPERTS * tme) * ROW_CHUNKS, LANES), F32)
    for l in range(depth):
        lw = _pack_layer(l, w_in, norm_mix, q_norm, w_q_up, kv_norm, w_kv_up, sg_norm, w_s, b_s, conv_w,
                         w_a_out, w_b_out, w_c_out, w_o, norm_ffn, w_group_router, b_group_router,
                         w_expert_router, b_expert_router, w_gate_up, w_down)
        q, k, v, part, ga = _mixer_pre(x2, lw, cos_t, sin_t, seq=seq, tm=tm)
        o = _attention(q, k, v, batch=batch, seq=seq, tq=tq)
        x1, h2p, info, meta, cnt = _mixer_post(o, part, ga, x2, lw, tm=tm)
        x2, xs = _moe(x1, h2p, info, meta, cnt, lw, fn, xs, tme=tme, tmc=tmc, final_norm=(l == depth - 1))
    return x2.reshape(batch, seq, d)
```
